```python
import math
import jax, jax.numpy as jnp
from jax import lax
import numpy as np

D_MODEL = 1024
BATCH = 2
SEQ = 8192
DEPTH = 2
DEC_BATCH = 32
DEC_SEQ = 1
PAST_LEN = 16384
PAGE_SIZE = 128

N_EVEN = (DEPTH + 1) // 2
N_ODD = DEPTH // 2
H_RET = 4
DK_RET = 128
DV_RET = 128
RET_CHUNK = 128
ROPE_BASE = 10000.0
H_DIFF = 4
DH_DIFF = 64
DV_DIFF = 2 * DH_DIFF
Q_BLOCK = 128
IN_SPLITS = (H_RET * DK_RET, H_RET * DK_RET, H_RET * DV_RET, H_RET * DV_RET,
             H_DIFF * 2 * DH_DIFF, H_DIFF * 2 * DH_DIFF, H_DIFF * DV_DIFF)
IN_EVEN = sum(IN_SPLITS)
W_MIX = H_RET * DV_RET + H_DIFF * DV_DIFF
SSM_GROUP = 16
W_SSM = D_MODEL
N_GROUPS = W_SSM // SSM_GROUP
SSM_STATE = 64
SCAN_CHUNK = 128
DT_MIN = 1e-3
DT_MAX = 1e-1
D_FF = 2816
N_EXPERTS = 8
TOP_K = 2
D_FF_EXPERT = 3584
N_ADA = 6
NORM_EPS = 1e-6

kernel_name = 'hybrid_retention_diffattn_s5_moe_decoder_step'


def rms_norm(x, g):
    xf = x.astype(jnp.float32)
    y = xf * lax.rsqrt(jnp.mean(xf * xf, axis=-1, keepdims=True) + NORM_EPS)
    return (y * g.astype(jnp.float32)).astype(x.dtype)


def modulate(x, g, shift, scale):
    return rms_norm(x, g) * (1.0 + scale[:, None, :]) + shift[:, None, :]


def rotary(x, pos):
    half = x.shape[-1] // 2
    inv = ROPE_BASE ** (-jnp.arange(half, dtype=jnp.float32) / half)
    ang = pos.astype(jnp.float32)[:, None] * inv[None, :]
    cos = jnp.cos(ang)[:, None, :]
    sin = jnp.sin(ang)[:, None, :]
    x1, x2 = x[..., :half], x[..., half:]
    return jnp.concatenate([x1 * cos - x2 * sin, x1 * sin + x2 * cos], axis=-1)


def retention_chunkwise(q, k, v, s0):
    bsz, seq_len = q.shape[:2]
    chunk = math.gcd(seq_len, RET_CHUNK)
    n_chunks = seq_len // chunk
    log_g = jnp.log1p(-jnp.exp2(-5.0 - jnp.arange(H_RET, dtype=jnp.float32)))
    idx = jnp.arange(chunk, dtype=jnp.float32)
    rel = idx[:, None] - idx[None, :]
    decay_mask = jnp.where(rel >= 0, jnp.exp(log_g[:, None, None] * jnp.maximum(rel, 0.0)), 0.0)
    q_decay = jnp.exp(log_g[None, :] * (idx[:, None] + 1.0))
    k_decay = jnp.exp(log_g[None, :] * (chunk - 1.0 - idx[:, None]))
    chunk_decay = jnp.exp(log_g * chunk)

    def split(t):
        return t.reshape((bsz, n_chunks, chunk) + t.shape[2:]).swapaxes(0, 1)

    def step(s, blk):
        qc, kc, vc = blk
        att = jnp.einsum('bchd,bshd->bhcs', qc, kc) * decay_mask
        o = (jnp.einsum('bhcs,bshe->bche', att, vc)
             + jnp.einsum('bchd,bhde->bche', qc * q_decay[:, :, None], s))
        s = chunk_decay[:, None, None] * s + jnp.einsum('bchd,bche->bhde', kc * k_decay[:, :, None], vc)
        return s, o

    s_final, outs = lax.scan(step, s0, (split(q), split(k), split(v)))
    return outs.swapaxes(0, 1).reshape(bsz, seq_len, H_RET, DV_RET), s_final


def diff_attention(q, k, v, q_pos, lam):
    bsz, lq = q.shape[:2]
    lk = k.shape[1]
    qb = math.gcd(lq, Q_BLOCK)
    nb = lq // qb
    k_pos = jnp.arange(lk)
    scale = DH_DIFF ** -0.5
    qs = q.reshape(bsz, nb, qb, H_DIFF, 2, DH_DIFF).swapaxes(0, 1)
    ps = q_pos.reshape(nb, qb)

    def block(args):
        qblk, pblk = args
        s = jnp.einsum('bqhcd,bkhcd->bhcqk', qblk, k).astype(jnp.float32) * scale
        mask = k_pos[None, :] <= pblk[:, None]
        s = jnp.where(mask, s, -1e30)
        p = jax.nn.softmax(s, axis=-1)
        a = p[:, :, 0] - lam * p[:, :, 1]
        return jnp.einsum('bhqk,bkhe->bqhe', a.astype(v.dtype), v)

    out = lax.map(block, (qs, ps))
    return out.swapaxes(0, 1).reshape(bsz, lq, H_DIFF, DV_DIFF)


def even_mixer(h, pos, s0, kv_past, w_in, w_out, g_ret, g_q, g_k, g_d,
               lq1, lk1, lq2, lk2, lam_init):
    bsz, seq_len, _ = h.shape
    f32 = jnp.float32
    offsets = [int(o) for o in np.cumsum(IN_SPLITS)[:-1]]
    qa, ka, va, ga, qb, kb, vb = jnp.split(h @ w_in, offsets, axis=-1)
    qa = rotary(qa.reshape(bsz, seq_len, H_RET, DK_RET).astype(f32), pos)
    ka = rotary(ka.reshape(bsz, seq_len, H_RET, DK_RET).astype(f32), pos) * DK_RET ** -0.5
    va = va.reshape(bsz, seq_len, H_RET, DV_RET).astype(f32)
    o_ret, s_new = retention_chunkwise(qa, ka, va, s0.astype(f32))
    o_ret = rms_norm(o_ret, g_ret).astype(h.dtype) * jax.nn.silu(ga.reshape(bsz, seq_len, H_RET, DV_RET))
    qb = rms_norm(qb.reshape(bsz, seq_len, H_DIFF, 2, DH_DIFF), g_q)
    kb = rms_norm(kb.reshape(bsz, seq_len, H_DIFF, 2, DH_DIFF), g_k)
    vb = vb.reshape(bsz, seq_len, H_DIFF, DV_DIFF)
    lam = (jnp.exp(jnp.sum(lq1.astype(f32) * lk1.astype(f32)))
           - jnp.exp(jnp.sum(lq2.astype(f32) * lk2.astype(f32))) + lam_init)
    if kv_past is None:
        k_all, v_all = kb, vb
    else:
        k_all = jnp.concatenate([kv_past[0], kb], axis=1)
        v_all = jnp.concatenate([kv_past[1], vb], axis=1)
    o_diff = diff_attention(qb, k_all, v_all, pos, lam)
    o_diff = rms_norm(o_diff, g_d) * (1.0 - lam_init)
    mixed = jnp.concatenate([o_ret.reshape(bsz, seq_len, -1), o_diff.reshape(bsz, seq_len, -1)], axis=-1)
    return mixed @ w_out, s_new.astype(s0.dtype), kb, vb


def s5_discretise(lam_re, lam_im, log_step, b_re, b_im):
    f32 = jnp.float32
    lr = lam_re.astype(f32)
    li = lam_im.astype(f32)
    step = jnp.exp(log_step.astype(f32))[:, None]
    mag = jnp.exp(lr * step)
    ab_re = mag * jnp.cos(li * step)
    ab_im = mag * jnp.sin(li * step)
    nr = ab_re - 1.0
    ni = ab_im
    den = lr * lr + li * li
    kr = (nr * lr + ni * li) / den
    ki = (ni * lr - nr * li) / den
    br = b_re.astype(f32)
    bi = b_im.astype(f32)
    bb_re = kr[..., None] * br - ki[..., None] * bi
    bb_im = kr[..., None] * bi + ki[..., None] * br
    return ab_re, ab_im, bb_re, bb_im


def complex_affine_combine(e1, e2):
    ar1, ai1, br1, bi1 = e1
    ar2, ai2, br2, bi2 = e2
    return (ar2 * ar1 - ai2 * ai1,
            ar2 * ai1 + ai2 * ar1,
            ar2 * br1 - ai2 * bi1 + br2,
            ar2 * bi1 + ai2 * br1 + bi2)


def s5_scan(u, h0_re, h0_im, lam_re, lam_im, log_step, b_re, b_im, c_re, c_im, d_skip):
    f32 = jnp.float32
    bsz, seq_len = u.shape[:2]
    ab_re, ab_im, bb_re, bb_im = s5_discretise(lam_re, lam_im, log_step, b_re, b_im)
    chunk = math.gcd(seq_len, SCAN_CHUNK)
    n_chunks = seq_len // chunk
    a_re = jnp.broadcast_to(ab_re, (bsz, chunk) + ab_re.shape)
    a_im = jnp.broadcast_to(ab_im, (bsz, chunk) + ab_im.shape)
    cr = c_re.astype(f32)
    ci = c_im.astype(f32)
    dd = d_skip.astype(f32)
    us = u.astype(f32).reshape(bsz, n_chunks, chunk, N_GROUPS, SSM_GROUP).swapaxes(0, 1)

    def step(carry, uc):
        hr0, hi0 = carry
        bu_re = jnp.einsum('bcgp,gnp->bcgn', uc, bb_re)
        bu_im = jnp.einsum('bcgp,gnp->bcgn', uc, bb_im)
        acr, aci, hr, hi = lax.associative_scan(complex_affine_combine, (a_re, a_im, bu_re, bu_im), axis=1)
        hr, hi = (hr + acr * hr0[:, None] - aci * hi0[:, None],
                  hi + acr * hi0[:, None] + aci * hr0[:, None])
        y = (jnp.einsum('gpn,bcgn->bcgp', cr, hr) - jnp.einsum('gpn,bcgn->bcgp', ci, hi) + dd * uc)
        return (hr[:, -1], hi[:, -1]), y

    (hr, hi), ys = lax.scan(step, (h0_re.astype(f32), h0_im.astype(f32)), us)
    y = ys.swapaxes(0, 1).reshape(bsz, seq_len, N_GROUPS, SSM_GROUP)
    return y, hr, hi


def odd_mixer(h, h0_re, h0_im, w_in, lam_re, lam_im, log_step, b_re, b_im, c_re, c_im, d_skip, w_glu):
    bsz, seq_len, _ = h.shape
    u = (h @ w_in).reshape(bsz, seq_len, N_GROUPS, SSM_GROUP)
    y, hr, hi = s5_scan(u, h0_re, h0_im, lam_re, lam_im, log_step, b_re, b_im, c_re, c_im, d_skip)
    z = jax.nn.gelu(y.reshape(bsz, seq_len, W_SSM).astype(h.dtype))
    val, gate = jnp.split(z @ w_glu, 2, axis=-1)
    return val * jax.nn.sigmoid(gate), hr.astype(h0_re.dtype), hi.astype(h0_im.dtype)


def swiglu(h, w1, w3, w2):
    return (jax.nn.silu(h @ w1) * (h @ w3)) @ w2


def moe_swiglu(h, w_router, b_router, w1, w3, w2):
    logits = (h @ w_router).astype(jnp.float32) + b_router.astype(jnp.float32)
    top_v, top_i = lax.top_k(logits, TOP_K)
    gates = jax.nn.softmax(top_v, axis=-1)
    combine = jnp.sum(jax.nn.one_hot(top_i, N_EXPERTS, dtype=jnp.float32) * gates[..., None], axis=-2)
    out = jnp.zeros_like(h)
    for e in range(N_EXPERTS):
        out = out + combine[..., e:e + 1].astype(h.dtype) * swiglu(h, w1[e], w3[e], w2[e])
    return out


def setup_inputs(seed: int = 0) -> dict:
    key = jax.random.key(seed)
    ks = iter(jax.random.split(key, 48))
    f32 = jnp.float32

    def nrm(shape, scale=1.0):
        return scale * jax.random.normal(next(ks), shape, f32)

    n_pages = PAST_LEN // PAGE_SIZE
    n_used = DEC_BATCH * n_pages
    n_pool = n_used + max(1, n_used // 4)
    page_table = jax.random.permutation(next(ks), n_pool)[:n_used].reshape(DEC_BATCH, n_pages).astype(jnp.int32)

    x_prompt = nrm((BATCH, SEQ, D_MODEL))
    x_sample = nrm((DEC_BATCH, DEC_SEQ, D_MODEL))
    state_ret = nrm((N_EVEN, DEC_BATCH, H_RET, DK_RET, DV_RET))
    cache_k = nrm((N_EVEN, n_pool, PAGE_SIZE, H_DIFF, 2, DH_DIFF))
    cache_v = nrm((N_EVEN, n_pool, PAGE_SIZE, H_DIFF, DV_DIFF))
    state_ssm_re = nrm((N_ODD, DEC_BATCH, N_GROUPS, SSM_STATE))
    state_ssm_im = nrm((N_ODD, DEC_BATCH, N_GROUPS, SSM_STATE))
    c_prompt = nrm((BATCH, D_MODEL))
    c_sample = nrm((DEC_BATCH, D_MODEL))

    w_ada = nrm((DEPTH, D_MODEL, N_ADA * D_MODEL), 0.5 * D_MODEL ** -0.5)
    b_ada = nrm((DEPTH, N_ADA * D_MODEL), 0.02)
    g_norm1 = 1.0 + nrm((DEPTH, D_MODEL), 0.02)
    g_norm2 = 1.0 + nrm((DEPTH, D_MODEL), 0.02)

    w_in_mix = nrm((N_EVEN, D_MODEL, IN_EVEN), D_MODEL ** -0.5)
    w_out_mix = nrm((N_EVEN, W_MIX, D_MODEL), W_MIX ** -0.5)
    g_ret = 1.0 + nrm((N_EVEN, DV_RET), 0.02)
    g_qnorm = 1.0 + nrm((N_EVEN, DH_DIFF), 0.02)
    g_knorm = 1.0 + nrm((N_EVEN, DH_DIFF), 0.02)
    g_dnorm = 1.0 + nrm((N_EVEN, DV_DIFF), 0.02)
    lam_q1 = nrm((N_EVEN, DH_DIFF), 0.1)
    lam_k1 = nrm((N_EVEN, DH_DIFF), 0.1)
    lam_q2 = nrm((N_EVEN, DH_DIFF), 0.1)
    lam_k2 = nrm((N_EVEN, DH_DIFF), 0.1)
    w_ffn1 = nrm((N_EVEN, D_MODEL, D_FF), D_MODEL ** -0.5)
    w_ffn3 = nrm((N_EVEN, D_MODEL, D_FF), D_MODEL ** -0.5)
    w_ffn2 = nrm((N_EVEN, D_FF, D_MODEL), D_FF ** -0.5)

    w_ssm_in = nrm((N_ODD, D_MODEL, W_SSM), D_MODEL ** -0.5)
    lam_re = -0.5 + nrm((N_ODD, N_GROUPS, SSM_STATE), 0.01)
    lam_im = jnp.pi * jnp.arange(SSM_STATE, dtype=f32) + nrm((N_ODD, N_GROUPS, SSM_STATE), 0.01)
    log_step = jax.random.uniform(next(ks), (N_ODD, N_GROUPS), f32, math.log(DT_MIN), math.log(DT_MAX))
    b_re = nrm((N_ODD, N_GROUPS, SSM_STATE, SSM_GROUP), SSM_GROUP ** -0.5)
    b_im = nrm((N_ODD, N_GROUPS, SSM_STATE, SSM_GROUP), SSM_GROUP ** -0.5)
    c_re = nrm((N_ODD, N_GROUPS, SSM_GROUP, SSM_STATE), SSM_STATE ** -0.5)
    c_im = nrm((N_ODD, N_GROUPS, SSM_GROUP, SSM_STATE), SSM_STATE ** -0.5)
    d_skip = nrm((N_ODD, N_GROUPS, SSM_GROUP))
    w_glu = nrm((N_ODD, W_SSM, 2 * D_MODEL), W_SSM ** -0.5)
    w_router = nrm((N_ODD, D_MODEL, N_EXPERTS), D_MODEL ** -0.5)
    b_router = nrm((N_ODD, N_EXPERTS), 0.01)
    w_moe1 = nrm((N_ODD, N_EXPERTS, D_MODEL, D_FF_EXPERT), D_MODEL ** -0.5)
    w_moe3 = nrm((N_ODD, N_EXPERTS, D_MODEL, D_FF_EXPERT), D_MODEL ** -0.5)
    w_moe2 = nrm((N_ODD, N_EXPERTS, D_FF_EXPERT, D_MODEL), D_FF_EXPERT ** -0.5)

    return {'x_prompt': x_prompt, 'x_sample': x_sample, 'state_ret': state_ret,
            'cache_k': cache_k, 'cache_v': cache_v, 'state_ssm_re': state_ssm_re,
            'state_ssm_im': state_ssm_im, 'page_table': page_table,
            'c_prompt': c_prompt, 'c_sample': c_sample,
            'w_ada': w_ada, 'b_ada': b_ada, 'g_norm1': g_norm1, 'g_norm2': g_norm2,
            'w_in_mix': w_in_mix, 'w_out_mix': w_out_mix, 'g_ret': g_ret,
            'g_qnorm': g_qnorm, 'g_knorm': g_knorm, 'g_dnorm': g_dnorm,
            'lam_q1': lam_q1, 'lam_k1': lam_k1, 'lam_q2': lam_q2, 'lam_k2': lam_k2,
            'w_ffn1': w_ffn1, 'w_ffn3': w_ffn3, 'w_ffn2': w_ffn2,
            'w_ssm_in': w_ssm_in, 'lam_re': lam_re, 'lam_im': lam_im, 'log_step': log_step,
            'b_re': b_re, 'b_im': b_im, 'c_re': c_re, 'c_im': c_im, 'd_skip': d_skip,
            'w_glu': w_glu, 'w_router': w_router, 'b_router': b_router,
            'w_moe1': w_moe1, 'w_moe3': w_moe3, 'w_moe2': w_moe2}


def reference(x_prompt, x_sample, state_ret, cache_k, cache_v, state_ssm_re, state_ssm_im,
              page_table, c_prompt, c_sample, w_ada, b_ada, g_norm1, g_norm2,
              w_in_mix, w_out_mix, g_ret, g_qnorm, g_knorm, g_dnorm,
              lam_q1, lam_k1, lam_q2, lam_k2, w_ffn1, w_ffn3, w_ffn2,
              w_ssm_in, lam_re, lam_im, log_step, b_re, b_im, c_re, c_im, d_skip,
              w_glu, w_router, b_router, w_moe1, w_moe3, w_moe2):

    def trunk(x, c, pos, ret0, ssm_re0, ssm_im0, paged):
        rets, ks_new, vs_new, srs, sis = [], [], [], [], []
        for layer in range(DEPTH):
            mod = jax.nn.silu(c) @ w_ada[layer] + b_ada[layer]
            sh1, sc1, gt1, sh2, sc2, gt2 = jnp.split(mod, N_ADA, axis=-1)
            h = modulate(x, g_norm1[layer], sh1, sc1)
            i = layer // 2
            if layer % 2 == 0:
                kv_past = None
                if paged:
                    n_seq = page_table.shape[0]
                    k_past = cache_k[i, page_table].reshape(n_seq, -1, H_DIFF, 2, DH_DIFF)
                    v_past = cache_v[i, page_table].reshape(n_seq, -1, H_DIFF, DV_DIFF)
                    kv_past = (k_past, v_past)
                lam_init = 0.8 - 0.6 * math.exp(-0.3 * layer)
                m, s_new, k_new, v_new = even_mixer(
                    h, pos, ret0[i], kv_past, w_in_mix[i], w_out_mix[i], g_ret[i],
                    g_qnorm[i], g_knorm[i], g_dnorm[i], lam_q1[i], lam_k1[i],
                    lam_q2[i], lam_k2[i], lam_init)
                rets.append(s_new)
                ks_new.append(k_new)
                vs_new.append(v_new)
                x = x + gt1[:, None, :] * m
                h = modulate(x, g_norm2[layer], sh2, sc2)
                x = x + gt2[:, None, :] * swiglu(h, w_ffn1[i], w_ffn3[i], w_ffn2[i])
            else:
                m, hr, hi = odd_mixer(
                    h, ssm_re0[i], ssm_im0[i], w_ssm_in[i], lam_re[i], lam_im[i], log_step[i],
                    b_re[i], b_im[i], c_re[i], c_im[i], d_skip[i], w_glu[i])
                srs.append(hr)
                sis.append(hi)
                x = x + gt1[:, None, :] * m
                h = modulate(x, g_norm2[layer], sh2, sc2)
                x = x + gt2[:, None, :] * moe_swiglu(h, w_router[i], b_router[i],
                                                     w_moe1[i], w_moe3[i], w_moe2[i])
        return (x, jnp.stack(rets), jnp.stack(ks_new), jnp.stack(vs_new),
                jnp.stack(srs), jnp.stack(sis))

    bsz, seq_len = x_prompt.shape[:2]
    ret0 = jnp.zeros((N_EVEN, bsz, H_RET, DK_RET, DV_RET), x_prompt.dtype)
    ssm0 = jnp.zeros((N_ODD, bsz, N_GROUPS, SSM_STATE), x_prompt.dtype)
    y_prompt, ret_p, k_p, v_p, sr_p, si_p = trunk(
        x_prompt, c_prompt, jnp.arange(seq_len), ret0, ssm0, ssm0, False)
    y_sample, ret_s, k_s, v_s, sr_s, si_s = trunk(
        x_sample, c_sample, PAST_LEN + jnp.arange(x_sample.shape[1]),
        state_ret, state_ssm_re, state_ssm_im, True)
    return (y_prompt, y_sample, ret_p, ret_s, k_p, v_p, k_s, v_s, sr_p, si_p, sr_s, si_s)
```

```python
import functools
import math

import jax
import jax.numpy as jnp
from jax import lax
from jax.experimental import pallas as pl
from jax.experimental.pallas import tpu as pltpu

F32 = jnp.float32
BF16 = jnp.bfloat16

D_MODEL = 1024
H_RET = 4
DK_RET = 128
DV_RET = 128
RET_CHUNK = 128
ROPE_BASE = 10000.0
H_DIFF = 4
DH_DIFF = 64
DV_DIFF = 128
PAGE_SIZE = 128
W_RET = H_RET * DK_RET
W_DIFF = H_DIFF * 2 * DH_DIFF
IN_EVEN = 4 * W_RET + 3 * W_DIFF
SSM_GROUP = 16
N_GROUPS = D_MODEL // SSM_GROUP
SSM_STATE = 64
N_PAIRS = N_GROUPS // 2
D_FF = 2816
N_EXPERTS = 8
D_FF_EXPERT = 3584
N_ADA = 6
NORM_EPS = 1e-6
LANES = 128
NEG_BIG = -1e30

VMEM_LIMIT_BYTES = 56 * 1024 * 1024


def _params(*sem):
    return pltpu.CompilerParams(dimension_semantics=sem, vmem_limit_bytes=VMEM_LIMIT_BYTES)


def _bdot(a, b):
    return jnp.dot(a, b, preferred_element_type=F32)


def _dot_nt(a, b):
    return lax.dot_general(a, b, (((1,), (1,)), ((), ())), preferred_element_type=F32)


def _dot_tn(a, b):
    return lax.dot_general(a, b, (((0,), (0,)), ((), ())), preferred_element_type=F32)


def _rms(x):
    return x * lax.rsqrt(jnp.mean(x * x, axis=-1, keepdims=True) + NORM_EPS)


def _modulate(x, g, shift, scale):
    return (_rms(x) * g) * (1.0 + scale) + shift


def _split_dot(x, w_hi, w_lo):
    x_hi = x.astype(BF16)
    x_lo = (x - x_hi.astype(F32)).astype(BF16)
    return _bdot(x_hi, w_hi) + _bdot(x_lo, w_hi) + _bdot(x_hi, w_lo)


def _mm(x, w_refs, rows=slice(None)):
    if len(w_refs) == 1:
        return _bdot(x.astype(BF16), w_refs[0][rows, :])
    return _split_dot(x.astype(F32), w_refs[0][rows, :], w_refs[1][rows, :])


def _wsplit(w, precise):
    w = w.astype(F32)
    hi = w.astype(BF16)
    if not precise:
        return (hi,)
    return (hi, (w - hi.astype(F32)).astype(BF16))


def _ada_kernel(c_ref, w_ref, b_ref, o_ref):
    w = w_ref[...]
    w_hi = w.astype(BF16)
    w_lo = (w - w_hi.astype(F32)).astype(BF16)
    o_ref[...] = _split_dot(jax.nn.silu(c_ref[...]), w_hi, w_lo) + b_ref[...]


def _ada(c_all, w_ada, b_ada):
    depth, _, n = w_ada.shape
    rows = c_all.shape[0]
    tn = 1536
    return pl.pallas_call(
        _ada_kernel,
        grid=(depth, n // tn),
        in_specs=[
            pl.BlockSpec((rows, D_MODEL), lambda l, j: (0, 0)),
            pl.BlockSpec((None, D_MODEL, tn), lambda l, j: (l, 0, j)),
            pl.BlockSpec((None, 1, tn), lambda l, j: (l, 0, j)),
        ],
        out_specs=pl.BlockSpec((None, rows, tn), lambda l, j: (l, 0, j)),
        out_shape=jax.ShapeDtypeStruct((depth, rows, n), F32),
        compiler_params=_params("parallel", "parallel"),
        name="ada",
    )(c_all, w_ada, b_ada.reshape(depth, 1, n))


def _vec_spec(arr, tm, tiles_per_seq):
    if arr.shape[1] == 1:
        return pl.BlockSpec((None, 1, arr.shape[2]), lambda i, *_: (i // tiles_per_seq, 0, 0))
    return pl.BlockSpec((None, tm, arr.shape[2]), lambda i, *_: (0, 0, 0))


def _pos_spec(arr, tm, tiles_per_seq):
    if arr.shape[0] == 1:
        return pl.BlockSpec((1, arr.shape[1]), lambda i, *_: (0, 0))
    return pl.BlockSpec((tm, arr.shape[1]), lambda i, *_: (i % tiles_per_seq, 0))


def _full_spec(arr):
    nd = arr.ndim
    return pl.BlockSpec(arr.shape, lambda *_: (0,) * nd)


def _in_even_kernel(x_ref, g_ref, sh_ref, sc_ref, cos_ref, sin_ref, gq_ref, gk_ref,
                    seg_ref, *rest):
    w_refs = rest[:-4]
    ret_ref, qb_ref, kb_ref, vb_ref = rest[-4:]
    y = _mm(_modulate(x_ref[...], g_ref[...], sh_ref[...], sc_ref[...]), w_refs)
    cos = cos_ref[...]
    sin = sin_ref[...]
    for part in range(2):
        for hd in range(H_RET):
            lo = part * W_RET + hd * DK_RET
            blk = y[:, lo:lo + DK_RET]
            rot = blk * cos + pltpu.roll(blk, DK_RET // 2, axis=1) * sin
            if part == 1:
                rot = rot * (DK_RET ** -0.5)
            ret_ref[:, lo:lo + DK_RET] = rot
    ret_ref[:, 2 * W_RET:4 * W_RET] = y[:, 2 * W_RET:4 * W_RET]
    seg = seg_ref[...]
    base = 4 * W_RET

    def qk_norm(blk, g):
        sq = blk * blk
        sq_hi = sq.astype(BF16)
        sq_lo = (sq - sq_hi.astype(F32)).astype(BF16)
        ms = (_bdot(sq_hi, seg) + _bdot(sq_lo, seg)) * (1.0 / DH_DIFF)
        return blk * lax.rsqrt(ms + NORM_EPS) * g

    qb_ref[...] = qk_norm(y[:, base:base + W_DIFF], gq_ref[...]).astype(qb_ref.dtype)
    kb_ref[...] = qk_norm(y[:, base + W_DIFF:base + 2 * W_DIFF], gk_ref[...])
    vb_ref[...] = y[:, base + 2 * W_DIFF:base + 3 * W_DIFF]


def _in_even(x2d, g, sh, sc, cos_t, sin_t, ws, gq, gk, seg, tm, tiles_per_seq, q_dtype):
    m = x2d.shape[0]
    row = lambda n: pl.BlockSpec((tm, n), lambda i: (i, 0))
    return pl.pallas_call(
        _in_even_kernel,
        grid=(m // tm,),
        in_specs=[
            row(D_MODEL), _full_spec(g), _vec_spec(sh, tm, tiles_per_seq), _vec_spec(sc, tm, tiles_per_seq),
            _pos_spec(cos_t, tm, tiles_per_seq), _pos_spec(sin_t, tm, tiles_per_seq),
            _full_spec(gq), _full_spec(gk), _full_spec(seg),
        ] + [_full_spec(w) for w in ws],
        out_specs=[row(4 * W_RET), row(W_DIFF), row(W_DIFF), row(W_DIFF)],
        out_shape=[
            jax.ShapeDtypeStruct((m, 4 * W_RET), F32),
            jax.ShapeDtypeStruct((m, W_DIFF), q_dtype),
            jax.ShapeDtypeStruct((m, W_DIFF), F32),
            jax.ShapeDtypeStruct((m, W_DIFF), F32),
        ],
        compiler_params=_params("parallel"),
        name="in_even",
    )(x2d, g, sh, sc, cos_t, sin_t, gq, gk, seg, *ws)


def _ret_kernel(q_ref, k_ref, v_ref, ga_ref, dm_ref, qd_ref, kd_ref, cd_ref, g_ref,
                o_ref, sfin_ref, s_scr, *, n_chunks):
    j = pl.program_id(1)

    @pl.when(j == 0)
    def _():
        s_scr[...] = jnp.zeros_like(s_scr)

    g = g_ref[...]
    for hd in range(H_RET):
        sl = slice(hd * DK_RET, (hd + 1) * DK_RET)
        dm = dm_ref[hd]
        qd = qd_ref[hd]
        kd = kd_ref[hd]
        cd = cd_ref[hd]

        def body(c, carry, sl=sl, dm=dm, qd=qd, kd=kd, cd=cd, hd=hd):
            r = pl.multiple_of(c * RET_CHUNK, RET_CHUNK)
            q = q_ref[pl.ds(r, RET_CHUNK), sl]
            k = k_ref[pl.ds(r, RET_CHUNK), sl]
            v = v_ref[pl.ds(r, RET_CHUNK), sl]
            vb = v.astype(BF16)
            s = s_scr[hd]
            att = _dot_nt(q.astype(BF16), k.astype(BF16)) * dm
            o = _bdot(att.astype(BF16), vb) + _bdot((q * qd).astype(BF16), s.astype(BF16))
            s_scr[hd] = cd * s + _dot_tn((k * kd).astype(BF16), vb)
            on = _rms(o) * g
            o_ref[pl.ds(r, RET_CHUNK), sl] = (on * jax.nn.silu(ga_ref[pl.ds(r, RET_CHUNK), sl])).astype(BF16)
            return carry

        lax.fori_loop(0, n_chunks, body, 0)

    @pl.when(j == pl.num_programs(1) - 1)
    def _():
        sfin_ref[...] = s_scr[...]


def _retention_prompt(ret_in, bsz, seq_len, dm, qd, kd, cd, g_ret):
    n_chunks = min(8, seq_len // RET_CHUNK)
    rows = n_chunks * RET_CHUNK
    nj = seq_len // rows
    col = lambda c: pl.BlockSpec((rows, W_RET), lambda b, j: (b * nj + j, c))
    return pl.pallas_call(
        functools.partial(_ret_kernel, n_chunks=n_chunks),
        grid=(bsz, nj),
        in_specs=[col(0), col(1), col(2), col(3), _full_spec(dm), _full_spec(qd), _full_spec(kd),
                  _full_spec(cd), _full_spec(g_ret)],
        out_specs=[
            pl.BlockSpec((rows, W_RET), lambda b, j: (b * nj + j, 0)),
            pl.BlockSpec((None, H_RET, DK_RET, DV_RET), lambda b, j: (b, 0, 0, 0)),
        ],
        out_shape=[
            jax.ShapeDtypeStruct((bsz * seq_len, W_RET), BF16),
            jax.ShapeDtypeStruct((bsz, H_RET, DK_RET, DV_RET), F32),
        ],
        scratch_shapes=[pltpu.VMEM((H_RET, DK_RET, DV_RET), F32)],
        compiler_params=_params("parallel", "arbitrary"),
        name="retention_prompt",
    )(ret_in, ret_in, ret_in, ret_in, dm, qd, kd, cd, g_ret)


def _ret_step_kernel(ret_ref, s_ref, qd_ref, cd_ref, g_ref, o_ref, snew_ref):
    g = g_ref[...]
    rows = lax.broadcasted_iota(jnp.int32, (DK_RET, DK_RET), 0)
    cols = lax.broadcasted_iota(jnp.int32, (DK_RET, DK_RET), 1)
    diag = rows == cols
    for hd in range(H_RET):
        lo = hd * DK_RET
        q = ret_ref[:, lo:lo + DK_RET]
        k = ret_ref[:, W_RET + lo:W_RET + lo + DK_RET]
        v = ret_ref[:, 2 * W_RET + lo:2 * W_RET + lo + DK_RET]
        ga = ret_ref[:, 3 * W_RET + lo:3 * W_RET + lo + DK_RET]
        s = s_ref[hd]
        s_hi = s.astype(BF16)
        s_lo = (s - s_hi.astype(F32)).astype(BF16)
        att = jnp.sum(q * k, axis=-1, keepdims=True)
        q8 = jnp.broadcast_to(q * qd_ref[hd], (8, DK_RET))
        o = att * v + _split_dot(q8, s_hi, s_lo)[0:1, :]
        k_diag = jnp.where(diag, jnp.broadcast_to(k, (DK_RET, DK_RET)), 0.0)
        v_rows = jnp.broadcast_to(v, (DK_RET, DV_RET))
        v_hi = v_rows.astype(BF16)
        v_lo = (v_rows - v_hi.astype(F32)).astype(BF16)
        snew_ref[hd] = cd_ref[hd] * s + _split_dot(k_diag, v_hi, v_lo)
        on = _rms(o) * g
        o_ref[:, lo:lo + DK_RET] = on * jax.nn.silu(ga)


def _retention_step(ret_in, state, qd1, cd, g_ret):
    bsz = state.shape[0]
    ret3 = ret_in.reshape(bsz, 1, 4 * W_RET)
    o, s_new = pl.pallas_call(
        _ret_step_kernel,
        grid=(bsz,),
        in_specs=[
            pl.BlockSpec((None, 1, 4 * W_RET), lambda b: (b, 0, 0)),
            pl.BlockSpec((None, H_RET, DK_RET, DV_RET), lambda b: (b, 0, 0, 0)),
            _full_spec(qd1), _full_spec(cd), _full_spec(g_ret),
        ],
        out_specs=[
            pl.BlockSpec((None, 1, W_RET), lambda b: (b, 0, 0)),
            pl.BlockSpec((None, H_RET, DK_RET, DV_RET), lambda b: (b, 0, 0, 0)),
        ],
        out_shape=[
            jax.ShapeDtypeStruct((bsz, 1, W_RET), F32),
            jax.ShapeDtypeStruct((bsz, H_RET, DK_RET, DV_RET), F32),
        ],
        compiler_params=_params("parallel"),
        name="retention_step",
    )(ret3, state, qd1, cd, g_ret)
    return o.reshape(bsz, W_RET), s_new


def _lambda_value(lam_ref, lam_init):
    lv = lam_ref[...]
    s1 = jnp.sum(lv[0:1, :] * lv[1:2, :], axis=-1, keepdims=True)
    s2 = jnp.sum(lv[2:3, :] * lv[3:4, :], axis=-1, keepdims=True)
    return jnp.exp(s1) - jnp.exp(s2) + lam_init


def _diff_kernel(qi_ref, ki_ref, q_ref, k_ref, v_ref, lam_ref, g_ref, o_ref,
                 qs_scr, m_scr, l_scr, acc_scr, *, tq, lam_init):
    p = pl.program_id(2)
    qi = qi_ref[p]
    ki = ki_ref[p]

    @pl.when(ki == 0)
    def _():
        q = q_ref[...]
        lane = lax.broadcasted_iota(jnp.int32, q.shape, 1)
        zero = jnp.zeros_like(q)
        qs_scr[0:tq, :] = jnp.where(lane < DH_DIFF, q, zero)
        qs_scr[tq:2 * tq, :] = jnp.where(lane >= DH_DIFF, q, zero)
        m_scr[...] = jnp.full_like(m_scr, NEG_BIG)
        l_scr[...] = jnp.zeros_like(l_scr)
        acc_scr[...] = jnp.zeros_like(acc_scr)

    k = k_ref[...].astype(BF16)
    v = v_ref[...].astype(BF16)
    s = _dot_nt(qs_scr[...], k) * (DH_DIFF ** -0.5)
    row = lax.broadcasted_iota(jnp.int32, s.shape, 0)
    col = lax.broadcasted_iota(jnp.int32, s.shape, 1)
    row = jnp.where(row >= tq, row - tq, row)
    s = jnp.where(jnp.logical_or(ki < qi, col <= row), s, NEG_BIG)
    m_prev = m_scr[...]
    m_new = jnp.maximum(m_prev, jnp.max(s, axis=-1, keepdims=True))
    alpha = jnp.exp(m_prev - m_new)
    pr = jnp.exp(s - m_new)
    l_scr[...] = alpha * l_scr[...] + jnp.sum(pr, axis=-1, keepdims=True)
    acc_scr[...] = alpha * acc_scr[...] + _bdot(pr.astype(BF16), v)
    m_scr[...] = m_new

    @pl.when(ki == qi)
    def _():
        a = acc_scr[...] / l_scr[...]
        lam = _lambda_value(lam_ref, lam_init)
        o = a[0:tq, :] - lam * a[tq:2 * tq, :]
        o_ref[...] = (_rms(o) * g_ref[...] * (1.0 - lam_init)).astype(BF16)


def _diff_prompt(qb, kb, vb, lam_vec, g_d, bsz, seq_len, lam_init):
    tq = min(512, seq_len)
    nq = seq_len // tq
    pairs = [(i, j) for i in range(nq) for j in range(i + 1)]
    qi = jnp.asarray([a for a, _ in pairs], jnp.int32)
    ki = jnp.asarray([b for _, b in pairs], jnp.int32)
    grid_spec = pltpu.PrefetchScalarGridSpec(
        num_scalar_prefetch=2,
        grid=(bsz, H_DIFF, len(pairs)),
        in_specs=[
            pl.BlockSpec((tq, 2 * DH_DIFF), lambda b, h, p, qi, ki: (b * nq + qi[p], h)),
            pl.BlockSpec((tq, 2 * DH_DIFF), lambda b, h, p, qi, ki: (b * nq + ki[p], h)),
            pl.BlockSpec((tq, DV_DIFF), lambda b, h, p, qi, ki: (b * nq + ki[p], h)),
            pl.BlockSpec(lam_vec.shape, lambda b, h, p, qi, ki: (0, 0)),
            pl.BlockSpec(g_d.shape, lambda b, h, p, qi, ki: (0, 0)),
        ],
        out_specs=pl.BlockSpec((tq, DV_DIFF), lambda b, h, p, qi, ki: (b * nq + qi[p], h)),
        scratch_shapes=[
            pltpu.VMEM((2 * tq, 2 * DH_DIFF), BF16),
            pltpu.VMEM((2 * tq, 1), F32),
            pltpu.VMEM((2 * tq, 1), F32),
            pltpu.VMEM((2 * tq, DV_DIFF), F32),
        ],
    )
    return pl.pallas_call(
        functools.partial(_diff_kernel, tq=tq, lam_init=lam_init),
        grid_spec=grid_spec,
        out_shape=jax.ShapeDtypeStruct((bsz * seq_len, H_DIFF * DV_DIFF), BF16),
        compiler_params=_params("parallel", "parallel", "arbitrary"),
        name="diff_prompt",
    )(qi, ki, qb, kb, vb, lam_vec, g_d)


PAGES_PER_STEP = 8


def _diff_step_kernel(pt_ref, q_ref, kn_ref, vn_ref, lam_ref, g_ref, *rest, lam_init):
    k_refs = rest[:PAGES_PER_STEP]
    v_refs = rest[PAGES_PER_STEP:2 * PAGES_PER_STEP]
    o_ref = rest[2 * PAGES_PER_STEP]
    qbd_scr, m_scr, l_scr, acc_scr = rest[2 * PAGES_PER_STEP + 1:]
    j = pl.program_id(1)
    n_rows = 2 * H_DIFF

    @pl.when(j == 0)
    def _():
        q = jnp.broadcast_to(q_ref[...].astype(F32), (n_rows, W_DIFF))
        r = lax.broadcasted_iota(jnp.int32, (n_rows, W_DIFF), 0)
        lane = lax.broadcasted_iota(jnp.int32, (n_rows, W_DIFF), 1)
        want = (r % H_DIFF) * (2 * DH_DIFF) + (r // H_DIFF) * DH_DIFF
        keep = jnp.logical_and(lane >= want, lane < want + DH_DIFF)
        qbd_scr[...] = jnp.where(keep, q, 0.0).astype(BF16)
        m_scr[...] = jnp.full_like(m_scr, NEG_BIG)
        l_scr[...] = jnp.zeros_like(l_scr)
        acc_scr[...] = jnp.zeros_like(acc_scr)

    qbd = qbd_scr[...]
    for pg in range(PAGES_PER_STEP):
        k = k_refs[pg][...].astype(BF16)
        v = v_refs[pg][...].astype(BF16)
        s = _dot_nt(qbd, k) * (DH_DIFF ** -0.5)
        m_prev = m_scr[...]
        m_new = jnp.maximum(m_prev, jnp.max(s, axis=-1, keepdims=True))
        alpha = jnp.exp(m_prev - m_new)
        pr = jnp.exp(s - m_new)
        l_scr[...] = alpha * l_scr[...] + jnp.sum(pr, axis=-1, keepdims=True)
        acc_scr[...] = alpha * acc_scr[...] + _bdot(pr.astype(BF16), v)
        m_scr[...] = m_new

    @pl.when(j == pl.num_programs(1) - 1)
    def _():
        kn = jnp.broadcast_to(kn_ref[...].astype(BF16).astype(F32), (n_rows, W_DIFF))
        s = jnp.sum(qbd.astype(F32) * kn, axis=-1, keepdims=True) * (DH_DIFF ** -0.5)
        m_prev = m_scr[...]
        m_new = jnp.maximum(m_prev, s)
        alpha = jnp.exp(m_prev - m_new)
        pr = jnp.exp(s - m_new)
        l_fin = alpha * l_scr[...] + pr
        vn = vn_ref[...].astype(BF16).astype(F32)
        acc = alpha * acc_scr[...] + pr.astype(BF16).astype(F32) * vn
        a = acc / l_fin
        lam = _lambda_value(lam_ref, lam_init)
        d = a[0:H_DIFF, :] - lam * a[H_DIFF:n_rows, :]
        r = lax.broadcasted_iota(jnp.int32, (H_DIFF, W_DIFF), 0)
        lane = lax.broadcasted_iota(jnp.int32, (H_DIFF, W_DIFF), 1)
        own = jnp.logical_and(lane >= r * DV_DIFF, lane < (r + 1) * DV_DIFF)
        flat = jnp.sum(jnp.where(own, d, 0.0), axis=0, keepdims=True)
        g = g_ref[...]
        for hd in range(H_DIFF):
            blk = flat[:, hd * DV_DIFF:(hd + 1) * DV_DIFF]
            o_ref[:, hd * DV_DIFF:(hd + 1) * DV_DIFF] = _rms(blk) * g * (1.0 - lam_init)


def _diff_step(qb, kb_new, vb_new, cache_k, cache_v, page_table, lam_vec, g_d, lam_init):
    bsz, n_pages = page_table.shape
    n_pool = cache_k.shape[0]
    pps = PAGES_PER_STEP
    assert n_pages % pps == 0
    ck = cache_k.reshape(n_pool, PAGE_SIZE, W_DIFF)
    cv = cache_v.reshape(n_pool, PAGE_SIZE, H_DIFF * DV_DIFF)
    row3 = lambda a: a.reshape(bsz, 1, a.shape[-1])
    row_spec = pl.BlockSpec((None, 1, W_DIFF), lambda b, j, pt: (b, 0, 0))

    def page_spec(pg):
        return pl.BlockSpec((None, PAGE_SIZE, W_DIFF), lambda b, j, pt, pg=pg: (pt[b, j * pps + pg], 0, 0))

    grid_spec = pltpu.PrefetchScalarGridSpec(
        num_scalar_prefetch=1,
        grid=(bsz, n_pages // pps),
        in_specs=[row_spec, row_spec, row_spec,
                  pl.BlockSpec(lam_vec.shape, lambda b, j, pt: (0, 0)),
                  pl.BlockSpec(g_d.shape, lambda b, j, pt: (0, 0))]
                 + [page_spec(pg) for pg in range(pps)] + [page_spec(pg) for pg in range(pps)],
        out_specs=pl.BlockSpec((None, 1, W_DIFF), lambda b, j, pt: (b, 0, 0)),
        scratch_shapes=[
            pltpu.VMEM((2 * H_DIFF, W_DIFF), BF16),
            pltpu.VMEM((2 * H_DIFF, 1), F32),
            pltpu.VMEM((2 * H_DIFF, 1), F32),
            pltpu.VMEM((2 * H_DIFF, W_DIFF), F32),
        ],
    )
    o = pl.pallas_call(
        functools.partial(_diff_step_kernel, lam_init=lam_init),
        grid_spec=grid_spec,
        out_shape=jax.ShapeDtypeStruct((bsz, 1, W_DIFF), F32),
        compiler_params=_params("parallel", "arbitrary"),
        name="diff_step",
    )(page_table, row3(qb), row3(kb_new), row3(vb_new), lam_vec, g_d, *([ck] * pps), *([cv] * pps))
    return o.reshape(bsz, W_DIFF)


def _out_even_kernel(a_ref, b_ref, x_ref, gt_ref, *rest):
    w_refs = rest[:-1]
    o_ref = rest[-1]
    m = _mm(a_ref[...], w_refs, slice(0, W_RET)) + _mm(b_ref[...], w_refs, slice(W_RET, 2 * W_RET))
    o_ref[...] = x_ref[...] + gt_ref[...] * m


def _out_even(o_ret, o_diff, ws, x2d, gt, tm, tiles_per_seq):
    m = x2d.shape[0]
    row = lambda n: pl.BlockSpec((tm, n), lambda i: (i, 0))
    return pl.pallas_call(
        _out_even_kernel,
        grid=(m // tm,),
        in_specs=[row(W_RET), row(W_RET), row(D_MODEL), _vec_spec(gt, tm, tiles_per_seq)]
                 + [_full_spec(w) for w in ws],
        out_specs=row(D_MODEL),
        out_shape=jax.ShapeDtypeStruct((m, D_MODEL), F32),
        compiler_params=_params("parallel"),
        name="out_even",
    )(o_ret, o_diff, x2d, gt, *ws)


def _ffn_kernel(x_ref, g_ref, sh_ref, sc_ref, gt_ref, *rest, n_w):
    w1, w3, w2 = rest[0:n_w], rest[n_w:2 * n_w], rest[2 * n_w:3 * n_w]
    o_ref = rest[3 * n_w]
    x = x_ref[...]
    h = _modulate(x, g_ref[...], sh_ref[...], sc_ref[...])
    if n_w == 1:
        h = h.astype(BF16)
    u = jax.nn.silu(_mm(h, w1)) * _mm(h, w3)
    o_ref[...] = x + gt_ref[...] * _mm(u, w2)


def _ffn(x2d, g, sh, sc, gt, w1s, w3s, w2s, tm, tiles_per_seq):
    m = x2d.shape[0]
    row = pl.BlockSpec((tm, D_MODEL), lambda i: (i, 0))
    vs = lambda a: _vec_spec(a, tm, tiles_per_seq)
    ws = (*w1s, *w3s, *w2s)
    return pl.pallas_call(
        functools.partial(_ffn_kernel, n_w=len(w1s)),
        grid=(m // tm,),
        in_specs=[row, _full_spec(g), vs(sh), vs(sc), vs(gt)] + [_full_spec(w) for w in ws],
        out_specs=row,
        out_shape=jax.ShapeDtypeStruct((m, D_MODEL), F32),
        compiler_params=_params("parallel"),
        name="ffn",
    )(x2d, g, sh, sc, gt, *ws)


def _modmm_kernel(x_ref, g_ref, sh_ref, sc_ref, *rest):
    w_refs = rest[:-1]
    o_ref = rest[-1]
    o_ref[...] = _mm(_modulate(x_ref[...], g_ref[...], sh_ref[...], sc_ref[...]), w_refs)


def _modmm(x2d, g, sh, sc, ws, tm, tiles_per_seq):
    m = x2d.shape[0]
    n = ws[0].shape[1]
    vs = lambda a: _vec_spec(a, tm, tiles_per_seq)
    return pl.pallas_call(
        _modmm_kernel,
        grid=(m // tm,),
        in_specs=[pl.BlockSpec((tm, D_MODEL), lambda i: (i, 0)), _full_spec(g), vs(sh), vs(sc)]
                 + [_full_spec(w) for w in ws],
        out_specs=pl.BlockSpec((tm, n), lambda i: (i, 0)),
        out_shape=jax.ShapeDtypeStruct((m, n), F32),
        compiler_params=_params("parallel"),
        name="modmm",
    )(x2d, g, sh, sc, *ws)


def _mm_at(x, w_refs, idx):
    if len(w_refs) == 1:
        return _bdot(x.astype(BF16), w_refs[0][idx])
    return _split_dot(x, w_refs[0][idx], w_refs[1][idx])


def _s5_input_drive(u, bmat_refs, j):
    return _mm_at(u[:, j * LANES:(j + 1) * LANES], bmat_refs, j)


def _s5_output_block(h_re_of, h_im_of, cmat_refs, jb):
    acc = None
    for q in range(4):
        r = 4 * jb + q
        t = (_mm_at(h_re_of(r), cmat_refs, (r, slice(0, LANES), slice(None)))
             + _mm_at(h_im_of(r), cmat_refs, (r, slice(LANES, 2 * LANES), slice(None))))
        acc = t if acc is None else acc + t
    return acc


def _s5_prompt_kernel(u_ref, are_ref, aim_ref, bmat_ref, cmat_ref, d_ref, z_ref, hre_ref, him_ref,
                      sre, sim, st_re, st_im, *, tc):
    j = pl.program_id(1)

    @pl.when(j == 0)
    def _():
        st_re[...] = jnp.zeros_like(st_re)
        st_im[...] = jnp.zeros_like(st_im)

    u = u_ref[...]
    half = 4 * LANES
    for jb in range(8):
        res = _s5_input_drive(u, (bmat_ref,), jb)
        for q in range(4):
            r = 4 * jb + q
            sre[pl.ds(r, tc, stride=N_PAIRS), :] = res[:, q * LANES:(q + 1) * LANES]
            sim[pl.ds(r, tc, stride=N_PAIRS), :] = res[:, half + q * LANES:half + (q + 1) * LANES]

    a_re = are_ref[...]
    a_im = aim_ref[...]

    def step(t, carry):
        h_re, h_im = carry
        base = pl.multiple_of(t * N_PAIRS, N_PAIRS)
        n_re = a_re * h_re - a_im * h_im + sre[pl.ds(base, N_PAIRS), :]
        n_im = a_re * h_im + a_im * h_re + sim[pl.ds(base, N_PAIRS), :]
        sre[pl.ds(base, N_PAIRS), :] = n_re
        sim[pl.ds(base, N_PAIRS), :] = n_im
        return n_re, n_im

    h_re, h_im = lax.fori_loop(0, tc, step, (st_re[...], st_im[...]))
    st_re[...] = h_re
    st_im[...] = h_im

    for jb in range(8):
        y = _s5_output_block(lambda r: sre[pl.ds(r, tc, stride=N_PAIRS), :],
                             lambda r: sim[pl.ds(r, tc, stride=N_PAIRS), :], (cmat_ref,), jb)
        sl = slice(jb * LANES, (jb + 1) * LANES)
        y = y + d_ref[:, sl] * u[:, sl]
        z_ref[:, sl] = jax.nn.gelu(y, approximate=True).astype(BF16)

    @pl.when(j == pl.num_programs(1) - 1)
    def _():
        hre_ref[...] = h_re
        him_ref[...] = h_im


def _s5_prompt(u2d, bsz, seq_len, a_re, a_im, bmat, cmat, d_row):
    tc = min(256, seq_len)
    nj = seq_len // tc
    state_spec = pl.BlockSpec((None, N_PAIRS, LANES), lambda b, j: (b, 0, 0))
    return pl.pallas_call(
        functools.partial(_s5_prompt_kernel, tc=tc),
        grid=(bsz, nj),
        in_specs=[pl.BlockSpec((tc, D_MODEL), lambda b, j: (b * nj + j, 0)),
                  _full_spec(a_re), _full_spec(a_im), _full_spec(bmat), _full_spec(cmat), _full_spec(d_row)],
        out_specs=[pl.BlockSpec((tc, D_MODEL), lambda b, j: (b * nj + j, 0)), state_spec, state_spec],
        out_shape=[
            jax.ShapeDtypeStruct((bsz * seq_len, D_MODEL), BF16),
            jax.ShapeDtypeStruct((bsz, N_PAIRS, LANES), F32),
            jax.ShapeDtypeStruct((bsz, N_PAIRS, LANES), F32),
        ],
        scratch_shapes=[
            pltpu.VMEM((tc * N_PAIRS, LANES), F32),
            pltpu.VMEM((tc * N_PAIRS, LANES), F32),
            pltpu.VMEM((N_PAIRS, LANES), F32),
            pltpu.VMEM((N_PAIRS, LANES), F32),
        ],
        compiler_params=_params("parallel", "arbitrary"),
        name="s5_prompt",
    )(u2d, a_re, a_im, bmat, cmat, d_row)


def _s5_step_kernel(u_ref, h0re_ref, h0im_ref, are_ref, aim_ref, bhi_ref, blo_ref, chi_ref, clo_ref, d_ref,
                    z_ref, hre_ref, him_ref):
    u = u_ref[...]
    half = 4 * LANES
    bmat_refs = (bhi_ref, blo_ref)
    cmat_refs = (chi_ref, clo_ref)
    for jb in range(8):
        res = _s5_input_drive(u, bmat_refs, jb)
        sl = slice(jb * half, (jb + 1) * half)
        a_re = are_ref[:, sl]
        a_im = aim_ref[:, sl]
        h_re = h0re_ref[:, sl]
        h_im = h0im_ref[:, sl]
        hre_ref[:, sl] = a_re * h_re - a_im * h_im + res[:, 0:half]
        him_ref[:, sl] = a_re * h_im + a_im * h_re + res[:, half:2 * half]
    for jb in range(8):
        y = _s5_output_block(lambda r: hre_ref[:, r * LANES:(r + 1) * LANES],
                             lambda r: him_ref[:, r * LANES:(r + 1) * LANES], cmat_refs, jb)
        sl = slice(jb * LANES, (jb + 1) * LANES)
        y = y + d_ref[:, sl] * u[:, sl]
        z_ref[:, sl] = jax.nn.gelu(y, approximate=True)


def _s5_step(u2d, h0_re, h0_im, a_re_row, a_im_row, bmats, cmats, d_row):
    bsz = u2d.shape[0]
    n_state = N_GROUPS * SSM_STATE
    args = (u2d, h0_re, h0_im, a_re_row, a_im_row, *bmats, *cmats, d_row)
    return pl.pallas_call(
        _s5_step_kernel,
        grid=(1,),
        in_specs=[_full_spec(a) for a in args],
        out_specs=[pl.BlockSpec((bsz, D_MODEL), lambda i: (0, 0)),
                   pl.BlockSpec((bsz, n_state), lambda i: (0, 0)),
                   pl.BlockSpec((bsz, n_state), lambda i: (0, 0))],
        out_shape=[
            jax.ShapeDtypeStruct((bsz, D_MODEL), F32),
            jax.ShapeDtypeStruct((bsz, n_state), F32),
            jax.ShapeDtypeStruct((bsz, n_state), F32),
        ],
        compiler_params=_params("arbitrary"),
        name="s5_step",
    )(*args)


def _glu_router_kernel(z_ref, x_ref, gt_ref, g_ref, sh_ref, sc_ref, wrh_ref, wrl_ref, br_ref, *rest):
    wg_refs = rest[:-3]
    x_out, h_out, route_out = rest[-3:]
    vg = _mm(z_ref[...], wg_refs)
    val = vg[:, 0:D_MODEL]
    gate = vg[:, D_MODEL:2 * D_MODEL]
    x = x_ref[...] + gt_ref[...] * (val * jax.nn.sigmoid(gate))
    x_out[...] = x
    h = _modulate(x, g_ref[...], sh_ref[...], sc_ref[...])
    h_out[...] = h.astype(BF16)
    logits = _split_dot(h, wrh_ref[...], wrl_ref[...]) + br_ref[...]
    lane = lax.broadcasted_iota(jnp.int32, logits.shape, 1)
    v1 = jnp.max(logits, axis=-1, keepdims=True)
    i1 = jnp.min(jnp.where(logits == v1, lane, LANES), axis=-1, keepdims=True)
    others = jnp.where(lane == i1, NEG_BIG, logits)
    v2 = jnp.max(others, axis=-1, keepdims=True)
    i2 = jnp.min(jnp.where(others == v2, lane, LANES), axis=-1, keepdims=True)
    e = jnp.exp(v2 - v1)
    g1 = 1.0 / (1.0 + e)
    g2 = e / (1.0 + e)
    route = jnp.where(lane == i1, g1, 0.0) + jnp.where(lane == i2, g2, 0.0)
    flag = jnp.logical_or(lane == i1 + N_EXPERTS, lane == i2 + N_EXPERTS)
    route_out[...] = route + jnp.where(flag, 1.0, 0.0)


def _glu_router(z, wgs, x2d, gt, g, sh, sc, wr_hi, wr_lo, br, tm, tiles_per_seq):
    m = x2d.shape[0]
    row = lambda n: pl.BlockSpec((tm, n), lambda i: (i, 0))
    vs = lambda a: _vec_spec(a, tm, tiles_per_seq)
    return pl.pallas_call(
        _glu_router_kernel,
        grid=(m // tm,),
        in_specs=[row(D_MODEL), row(D_MODEL), vs(gt), _full_spec(g), vs(sh), vs(sc),
                  _full_spec(wr_hi), _full_spec(wr_lo), _full_spec(br)] + [_full_spec(w) for w in wgs],
        out_specs=[row(D_MODEL), row(D_MODEL), row(LANES)],
        out_shape=[
            jax.ShapeDtypeStruct((m, D_MODEL), F32),
            jax.ShapeDtypeStruct((m, D_MODEL), BF16),
            jax.ShapeDtypeStruct((m, LANES), F32),
        ],
        compiler_params=_params("parallel"),
        name="glu_router",
    )(z, x2d, gt, g, sh, sc, wr_hi, wr_lo, br, *wgs)


def _moe_kernel(te_ref, nv_ref, xs_ref, gate_ref, w1_ref, w3_ref, w2_ref, o_ref, acc_scr):
    i = pl.program_id(0)
    f = pl.program_id(1)
    nf = pl.num_programs(1)
    valid = i < nv_ref[0]

    @pl.when(valid)
    def _():
        x = xs_ref[...]
        a = _bdot(x, w1_ref[...].astype(BF16))
        b = _bdot(x, w3_ref[...].astype(BF16))
        u = (jax.nn.silu(a) * b).astype(BF16)
        contrib = _bdot(u, w2_ref[...].astype(BF16))

        @pl.when(f == 0)
        def _():
            acc_scr[...] = contrib

        @pl.when(f > 0)
        def _():
            acc_scr[...] = acc_scr[...] + contrib

        @pl.when(f == nf - 1)
        def _():
            o_ref[...] = acc_scr[...] * gate_ref[...]

    @pl.when(jnp.logical_and(jnp.logical_not(valid), f == nf - 1))
    def _():
        o_ref[...] = jnp.zeros_like(o_ref)


def _moe_experts(xs, row_gate, tile_expert, n_valid, w1, w3, w2, tm):
    rows = xs.shape[0]
    n_tiles = rows // tm
    tf = 512
    nf = D_FF_EXPERT // tf

    def fsel(i, f, nv):
        return jnp.where(i < nv[0], f, nf - 1)

    grid_spec = pltpu.PrefetchScalarGridSpec(
        num_scalar_prefetch=2,
        grid=(n_tiles, nf),
        in_specs=[
            pl.BlockSpec((tm, D_MODEL), lambda i, f, te, nv: (i, 0)),
            pl.BlockSpec((tm, 1), lambda i, f, te, nv: (i, 0)),
            pl.BlockSpec((None, D_MODEL, tf), lambda i, f, te, nv: (te[i], 0, fsel(i, f, nv))),
            pl.BlockSpec((None, D_MODEL, tf), lambda i, f, te, nv: (te[i], 0, fsel(i, f, nv))),
            pl.BlockSpec((None, tf, D_MODEL), lambda i, f, te, nv: (te[i], fsel(i, f, nv), 0)),
        ],
        out_specs=pl.BlockSpec((tm, D_MODEL), lambda i, f, te, nv: (i, 0)),
        scratch_shapes=[pltpu.VMEM((tm, D_MODEL), F32)],
    )
    return pl.pallas_call(
        _moe_kernel,
        grid_spec=grid_spec,
        out_shape=jax.ShapeDtypeStruct((rows, D_MODEL), F32),
        compiler_params=_params("parallel", "arbitrary"),
        name="moe_experts",
    )(tile_expert, n_valid, xs, row_gate, w1, w3, w2)


def _moe_residual_kernel(x_ref, gt_ref, a_ref, b_ref, o_ref):
    o_ref[...] = x_ref[...] + gt_ref[...] * (a_ref[...] + b_ref[...])


def _moe_residual(x2d, gt, ya, yb, tm, tiles_per_seq):
    m = x2d.shape[0]
    row = pl.BlockSpec((tm, D_MODEL), lambda i: (i, 0))
    return pl.pallas_call(
        _moe_residual_kernel,
        grid=(m // tm,),
        in_specs=[row, _vec_spec(gt, tm, tiles_per_seq), row, row],
        out_specs=row,
        out_shape=jax.ShapeDtypeStruct((m, D_MODEL), F32),
        compiler_params=_params("parallel"),
        name="moe_residual",
    )(x2d, gt, ya, yb)


def _moe(h_bf, route, x2d, gt, w1, w3, w2, tm_rows, tm, tiles_per_seq):
    m = h_bf.shape[0]
    comb = route[:, 0:N_EXPERTS]
    sel = route[:, N_EXPERTS:2 * N_EXPERTS] > 0.5
    seli = sel.astype(jnp.int32)
    counts = jnp.sum(seli, axis=0)
    padded = ((counts + tm_rows - 1) // tm_rows) * tm_rows
    ends = jnp.cumsum(padded)
    offsets = ends - padded
    rank = jnp.cumsum(seli, axis=0) - 1
    pos = offsets[None, :] + rank
    n_rows = ((2 * m + N_EXPERTS * (tm_rows - 1)) // tm_rows + 1) * tm_rows
    n_tiles = n_rows // tm_rows
    flat_pos = jnp.where(sel, pos, n_rows).reshape(-1)
    tok = jnp.broadcast_to(jnp.arange(m, dtype=jnp.int32)[:, None], (m, N_EXPERTS)).reshape(-1)
    row_token = jnp.zeros((n_rows,), jnp.int32).at[flat_pos].set(tok, mode="drop")
    row_gate = jnp.zeros((n_rows,), F32).at[flat_pos].set(comb.reshape(-1), mode="drop")
    tile_start = jnp.arange(n_tiles, dtype=jnp.int32) * tm_rows
    n_valid = (ends[-1] // tm_rows).astype(jnp.int32)
    tile_expert = jnp.searchsorted(ends, tile_start, side="right").astype(jnp.int32)
    last_e = jnp.searchsorted(ends, ends[-1] - 1, side="right").astype(jnp.int32)
    tile_expert = jnp.where(tile_start < ends[-1], tile_expert, last_e)
    xs = jnp.take(h_bf, row_token, axis=0)
    ys = _moe_experts(xs, row_gate.reshape(n_rows, 1), tile_expert, n_valid.reshape(1), w1, w3, w2, tm_rows)
    slots = jnp.sort(jnp.where(sel, pos, n_rows), axis=1)[:, 0:2]
    ya = jnp.take(ys, slots[:, 0], axis=0)
    yb = jnp.take(ys, slots[:, 1], axis=0)
    return _moe_residual(x2d, gt, ya, yb, tm, tiles_per_seq)


def _retention_constants(chunk):
    log_g = jnp.log1p(-jnp.exp2(-5.0 - jnp.arange(H_RET, dtype=F32)))
    idx = jnp.arange(chunk, dtype=F32)
    rel = idx[:, None] - idx[None, :]
    dm = jnp.where(rel >= 0, jnp.exp(log_g[:, None, None] * jnp.maximum(rel, 0.0)), 0.0)
    q_decay = jnp.exp(log_g[:, None] * (idx[None, :] + 1.0))
    k_decay = jnp.exp(log_g[:, None] * (chunk - 1.0 - idx[None, :]))
    chunk_decay = jnp.exp(log_g * chunk)
    qd = jnp.broadcast_to(q_decay[:, :, None], (H_RET, chunk, DK_RET))
    kd = jnp.broadcast_to(k_decay[:, :, None], (H_RET, chunk, DK_RET))
    cd = jnp.broadcast_to(chunk_decay[:, None, None], (H_RET, 1, DV_RET))
    return dm, qd, kd, cd


def _rotary_tables(pos):
    half = DK_RET // 2
    inv = ROPE_BASE ** (-jnp.arange(half, dtype=F32) / half)
    ang = pos.astype(F32)[:, None] * inv[None, :]
    cos = jnp.cos(ang)
    sin = jnp.sin(ang)
    return jnp.concatenate([cos, cos], axis=-1), jnp.concatenate([-sin, sin], axis=-1)


def _s5_matrices(lam_re, lam_im, log_step, b_re, b_im, c_re, c_im, d_skip):
    lr = lam_re.astype(F32)
    li = lam_im.astype(F32)
    step = jnp.exp(log_step.astype(F32))[:, None]
    mag = jnp.exp(lr * step)
    ab_re = mag * jnp.cos(li * step)
    ab_im = mag * jnp.sin(li * step)
    nr = ab_re - 1.0
    ni = ab_im
    den = lr * lr + li * li
    kr = (nr * lr + ni * li) / den
    ki = (ni * lr - nr * li) / den
    br = b_re.astype(F32)
    bi = b_im.astype(F32)
    bb_re = kr[..., None] * br - ki[..., None] * bi
    bb_im = kr[..., None] * bi + ki[..., None] * br
    eye8 = jnp.eye(8, dtype=F32)

    def drive(bb):
        t = bb.reshape(8, 8, SSM_STATE, SSM_GROUP).transpose(0, 1, 3, 2)
        t = t[:, :, :, None, :] * eye8[None, :, None, :, None]
        return t.reshape(8, LANES, 8 * SSM_STATE)

    bmat = jnp.concatenate([drive(bb_re), drive(bb_im)], axis=-1)
    r_idx = jnp.arange(N_PAIRS)[:, None, None]
    gl_idx = jnp.arange(2)[None, :, None]
    gb_idx = jnp.arange(8)[None, None, :]
    onehot = (gb_idx == 2 * (r_idx % 4) + gl_idx).astype(F32)

    def readout(c):
        t = c.astype(F32).reshape(N_PAIRS, 2, SSM_GROUP, SSM_STATE).transpose(0, 1, 3, 2)
        t = t[:, :, :, None, :] * onehot[:, :, None, :, None]
        return t.reshape(N_PAIRS, LANES, LANES)

    cmat = jnp.concatenate([readout(c_re), -readout(c_im)], axis=1)
    return (ab_re.reshape(N_PAIRS, LANES), ab_im.reshape(N_PAIRS, LANES), bmat, cmat,
            d_skip.astype(F32).reshape(1, D_MODEL))


def _trunk(x, mods, pos, ret0, ssm0, paged, p):
    bsz, seq_len, _ = x.shape
    m = bsz * seq_len
    prompt = seq_len > 1
    depth = p["w_ada"].shape[0]
    x2d = x.reshape(m, D_MODEL)
    if prompt:
        tm = min(512, seq_len)
        tps = seq_len // tm
        tm_ffn = min(256, seq_len)
        tps_ffn = seq_len // tm_ffn
        vec = lambda v: v.reshape(bsz, 1, D_MODEL)
    else:
        tm = tm_ffn = m
        tps = tps_ffn = 1
        vec = lambda v: v.reshape(1, bsz, D_MODEL)
    row = lambda v: v.reshape(1, -1).astype(F32)
    wsp = lambda w: _wsplit(w, not prompt)
    rets, ks_new, vs_new, srs, sis = [], [], [], [], []
    for layer in range(depth):
        sh1, sc1, gt1, sh2, sc2, gt2 = [vec(v) for v in jnp.split(mods[layer], N_ADA, axis=-1)]
        i = layer // 2
        if layer % 2 == 0:
            lam_init = 0.8 - 0.6 * math.exp(-0.3 * layer)
            cos_t, sin_t = _rotary_tables(pos)
            seg = (jnp.arange(W_DIFF)[:, None] // DH_DIFF == jnp.arange(W_DIFF)[None, :] // DH_DIFF).astype(BF16)
            gq = jnp.tile(row(p["g_qnorm"][i]), (1, W_DIFF // DH_DIFF))
            gk = jnp.tile(row(p["g_knorm"][i]), (1, W_DIFF // DH_DIFF))
            ret_in, qb, kb, vb = _in_even(
                x2d, row(p["g_norm1"][layer]), sh1, sc1, cos_t, sin_t, wsp(p["w_in_mix"][i]),
                gq, gk, seg, tm, tps, BF16 if prompt else F32)
            lam_vec = jnp.stack([p["lam_q1"][i], p["lam_k1"][i], p["lam_q2"][i], p["lam_k2"][i]]).astype(F32)
            g_ret = row(p["g_ret"][i])
            g_d = row(p["g_dnorm"][i])
            if prompt:
                dm, qd, kd, cd = _retention_constants(math.gcd(seq_len, RET_CHUNK))
                o_ret, s_new = _retention_prompt(ret_in, bsz, seq_len, dm, qd, kd, cd, g_ret)
                o_diff = _diff_prompt(qb, kb, vb, lam_vec, g_d, bsz, seq_len, lam_init)
            else:
                _, qd, _, cd = _retention_constants(1)
                o_ret, s_new = _retention_step(ret_in, ret0[i].astype(F32), qd, cd, g_ret)
                o_diff = _diff_step(qb, kb, vb, paged["cache_k"][i], paged["cache_v"][i],
                                    paged["page_table"], lam_vec, g_d, lam_init)
            rets.append(s_new)
            ks_new.append(kb.reshape(bsz, seq_len, H_DIFF, 2, DH_DIFF))
            vs_new.append(vb.reshape(bsz, seq_len, H_DIFF, DV_DIFF))
            x2d = _out_even(o_ret, o_diff, wsp(p["w_out_mix"][i]), x2d, gt1, tm, tps)
            x2d = _ffn(x2d, row(p["g_norm2"][layer]), sh2, sc2, gt2, wsp(p["w_ffn1"][i]),
                       wsp(p["w_ffn3"][i]), wsp(p["w_ffn2"][i]), tm_ffn, tps_ffn)
        else:
            a_re, a_im, bmat, cmat, d_row = _s5_matrices(
                p["lam_re"][i], p["lam_im"][i], p["log_step"][i], p["b_re"][i], p["b_im"][i],
                p["c_re"][i], p["c_im"][i], p["d_skip"][i])
            u2d = _modmm(x2d, row(p["g_norm1"][layer]), sh1, sc1, wsp(p["w_ssm_in"][i]), tm, tps)
            n_state = N_GROUPS * SSM_STATE
            if prompt:
                z, h_re, h_im = _s5_prompt(u2d, bsz, seq_len, a_re, a_im, bmat.astype(BF16), cmat.astype(BF16),
                                           d_row)
            else:
                z, h_re, h_im = _s5_step(
                    u2d, ssm0[0][i].astype(F32).reshape(bsz, n_state), ssm0[1][i].astype(F32).reshape(bsz, n_state),
                    a_re.reshape(1, n_state), a_im.reshape(1, n_state), wsp(bmat), wsp(cmat), d_row)
            srs.append(h_re.reshape(bsz, N_GROUPS, SSM_STATE))
            sis.append(h_im.reshape(bsz, N_GROUPS, SSM_STATE))
            wr = jnp.zeros((D_MODEL, LANES), F32).at[:, 0:N_EXPERTS].set(p["w_router"][i].astype(F32))
            wr_hi = wr.astype(BF16)
            wr_lo = (wr - wr_hi.astype(F32)).astype(BF16)
            br = jnp.full((1, LANES), NEG_BIG, F32).at[0, 0:N_EXPERTS].set(p["b_router"][i].astype(F32))
            x2d, h_bf, route = _glu_router(
                z, wsp(p["w_glu"][i]), x2d, gt1, row(p["g_norm2"][layer]), sh2, sc2,
                wr_hi, wr_lo, br, tm_ffn, tps_ffn)
            x2d = _moe(h_bf, route, x2d, gt2, p["w_moe1"][i], p["w_moe3"][i], p["w_moe2"][i],
                       512 if prompt else 32, tm, tps)
    return (x2d.reshape(bsz, seq_len, D_MODEL), jnp.stack(rets), jnp.stack(ks_new), jnp.stack(vs_new),
            jnp.stack(srs), jnp.stack(sis))


def kernel(x_prompt, x_sample, state_ret, cache_k, cache_v, state_ssm_re, state_ssm_im, page_table, c_prompt, c_sample, w_ada, b_ada, g_norm1, g_norm2, w_in_mix, w_out_mix, g_ret, g_qnorm, g_knorm, g_dnorm, lam_q1, lam_k1, lam_q2, lam_k2, w_ffn1, w_ffn3, w_ffn2, w_ssm_in, lam_re, lam_im, log_step, b_re, b_im, c_re, c_im, d_skip, w_glu, w_router, b_router, w_moe1, w_moe3, w_moe2):
    p = dict(w_ada=w_ada, g_norm1=g_norm1, g_norm2=g_norm2, w_in_mix=w_in_mix, w_out_mix=w_out_mix,
             g_ret=g_ret, g_qnorm=g_qnorm, g_knorm=g_knorm, g_dnorm=g_dnorm, lam_q1=lam_q1, lam_k1=lam_k1,
             lam_q2=lam_q2, lam_k2=lam_k2, w_ffn1=w_ffn1, w_ffn3=w_ffn3, w_ffn2=w_ffn2, w_ssm_in=w_ssm_in,
             lam_re=lam_re, lam_im=lam_im, log_step=log_step, b_re=b_re, b_im=b_im, c_re=c_re, c_im=c_im,
             d_skip=d_skip, w_glu=w_glu, w_router=w_router, b_router=b_router,
             w_moe1=w_moe1, w_moe3=w_moe3, w_moe2=w_moe2)
    n_prompt = c_prompt.shape[0]
    n_sample = c_sample.shape[0]
    n_all = n_prompt + n_sample
    rows = ((n_all + 7) // 8) * 8
    c_all = jnp.zeros((rows, D_MODEL), F32).at[0:n_prompt].set(c_prompt).at[n_prompt:n_all].set(c_sample)
    mods = _ada(c_all, w_ada, b_ada)
    seq_len = x_prompt.shape[1]
    past_len = page_table.shape[1] * PAGE_SIZE
    y_p, ret_p, k_p, v_p, sr_p, si_p = _trunk(
        x_prompt, mods[:, 0:n_prompt], jnp.arange(seq_len), None, None, None, p)
    paged = dict(cache_k=cache_k, cache_v=cache_v, page_table=page_table)
    y_s, ret_s, k_s, v_s, sr_s, si_s = _trunk(
        x_sample, mods[:, n_prompt:n_all], past_len + jnp.arange(x_sample.shape[1]),
        state_ret, (state_ssm_re, state_ssm_im), paged, p)
    return (y_p, y_s, ret_p, ret_s, k_p, v_p, k_s, v_s, sr_p, si_p, sr_s, si_s)
```

```python
import functools
import math

import jax
import jax.numpy as jnp
import numpy as np
from jax import lax
from jax.experimental import pallas as pl
from jax.experimental.pallas import tpu as pltpu

F32 = jnp.float32
BF16 = jnp.bfloat16

D_MODEL = 1024
H_RET = 4
DK_RET = 128
DV_RET = 128
RET_CHUNK = 128
ROPE_BASE = 10000.0
H_DIFF = 4
DH_DIFF = 64
DV_DIFF = 128
PAGE_SIZE = 128
W_RET = H_RET * DK_RET
W_DIFF = H_DIFF * 2 * DH_DIFF
IN_EVEN = 4 * W_RET + 3 * W_DIFF
SSM_GROUP = 16
N_GROUPS = D_MODEL // SSM_GROUP
SSM_STATE = 64
N_PAIRS = N_GROUPS // 2
D_FF = 2816
N_EXPERTS = 8
D_FF_EXPERT = 3584
N_ADA = 6
NORM_EPS = 1e-6
LANES = 128
NEG_BIG = -1e30

VMEM_LIMIT_BYTES = 56 * 1024 * 1024


def _params(*sem):
    return pltpu.CompilerParams(dimension_semantics=sem, vmem_limit_bytes=VMEM_LIMIT_BYTES)


def _bdot(a, b):
    return jnp.dot(a, b, preferred_element_type=F32)


def _dot_nt(a, b):
    return lax.dot_general(a, b, (((1,), (1,)), ((), ())), preferred_element_type=F32)


def _dot_tn(a, b):
    return lax.dot_general(a, b, (((0,), (0,)), ((), ())), preferred_element_type=F32)


def _rms(x):
    return x * lax.rsqrt(jnp.mean(x * x, axis=-1, keepdims=True) + NORM_EPS)


def _modulate(x, g, shift, scale):
    return (_rms(x) * g) * (1.0 + scale) + shift


def _hi_lo(x):
    bits = lax.bitcast_convert_type(x, jnp.uint32) & jnp.uint32(0xFFFF0000)
    hi = lax.bitcast_convert_type(bits, F32)
    return hi.astype(BF16), (x - hi).astype(BF16)


def _split_dot(x, w_hi, w_lo):
    x_hi, x_lo = _hi_lo(x)
    return _bdot(x_hi, w_hi) + _bdot(x_lo, w_hi) + _bdot(x_hi, w_lo)


def _mm(x, w_refs, rows=slice(None)):
    if len(w_refs) == 1:
        return _bdot(x.astype(BF16), w_refs[0][rows, :])
    return _split_dot(x.astype(F32), w_refs[0][rows, :], w_refs[1][rows, :])


def _wsplit(w, precise):
    w = w.astype(F32)
    if not precise:
        return (w.astype(BF16),)
    return _hi_lo(w)


def _ada_kernel(c_ref, w_ref, b_ref, o_ref):
    w_hi, w_lo = _hi_lo(w_ref[...])
    o_ref[...] = _split_dot(jax.nn.silu(c_ref[...]), w_hi, w_lo) + b_ref[...]


def _ada(c_all, w_ada, b_ada):
    depth, _, n = w_ada.shape
    rows = c_all.shape[0]
    tn = 1536
    return pl.pallas_call(
        _ada_kernel,
        grid=(depth, n // tn),
        in_specs=[
            pl.BlockSpec((rows, D_MODEL), lambda l, j: (0, 0)),
            pl.BlockSpec((None, D_MODEL, tn), lambda l, j: (l, 0, j)),
            pl.BlockSpec((None, 1, tn), lambda l, j: (l, 0, j)),
        ],
        out_specs=pl.BlockSpec((None, rows, tn), lambda l, j: (l, 0, j)),
        out_shape=jax.ShapeDtypeStruct((depth, rows, n), F32),
        compiler_params=_params("parallel", "parallel"),
        name="ada",
    )(c_all, w_ada, b_ada.reshape(depth, 1, n))


def _vec_spec(arr, tm, tiles_per_seq):
    if arr.shape[1] == 1:
        return pl.BlockSpec((None, 1, arr.shape[2]), lambda i, *_: (i // tiles_per_seq, 0, 0))
    return pl.BlockSpec((None, tm, arr.shape[2]), lambda i, *_: (0, 0, 0))


def _pos_spec(arr, tm, tiles_per_seq):
    if arr.shape[0] == 1:
        return pl.BlockSpec((1, arr.shape[1]), lambda i, *_: (0, 0))
    return pl.BlockSpec((tm, arr.shape[1]), lambda i, *_: (i % tiles_per_seq, 0))


def _full_spec(arr):
    nd = arr.ndim
    return pl.BlockSpec(arr.shape, lambda *_: (0,) * nd)


def _in_even_kernel(x_ref, g_ref, sh_ref, sc_ref, cos_ref, sin_ref, gq_ref, gk_ref,
                    seg_ref, *rest):
    w_refs = rest[:-4]
    ret_ref, qb_ref, kb_ref, vb_ref = rest[-4:]
    y = _mm(_modulate(x_ref[...], g_ref[...], sh_ref[...], sc_ref[...]), w_refs)
    cos = cos_ref[...]
    sin = sin_ref[...]
    for part in range(2):
        for hd in range(H_RET):
            lo = part * W_RET + hd * DK_RET
            blk = y[:, lo:lo + DK_RET]
            rot = blk * cos + pltpu.roll(blk, DK_RET // 2, axis=1) * sin
            if part == 1:
                rot = rot * (DK_RET ** -0.5)
            ret_ref[:, lo:lo + DK_RET] = rot
    ret_ref[:, 2 * W_RET:4 * W_RET] = y[:, 2 * W_RET:4 * W_RET]
    seg = seg_ref[...]
    base = 4 * W_RET

    def qk_norm(blk, g):
        sq_hi, sq_lo = _hi_lo(blk * blk)
        ms = (_bdot(sq_hi, seg) + _bdot(sq_lo, seg)) * (1.0 / DH_DIFF)
        return blk * lax.rsqrt(ms + NORM_EPS) * g

    qb_ref[...] = qk_norm(y[:, base:base + W_DIFF], gq_ref[...]).astype(qb_ref.dtype)
    kb_ref[...] = qk_norm(y[:, base + W_DIFF:base + 2 * W_DIFF], gk_ref[...])
    vb_ref[...] = y[:, base + 2 * W_DIFF:base + 3 * W_DIFF]


def _in_even(x2d, g, sh, sc, cos_t, sin_t, ws, gq, gk, seg, tm, tiles_per_seq, q_dtype):
    m = x2d.shape[0]
    row = lambda n: pl.BlockSpec((tm, n), lambda i: (i, 0))
    return pl.pallas_call(
        _in_even_kernel,
        grid=(m // tm,),
        in_specs=[
            row(D_MODEL), _full_spec(g), _vec_spec(sh, tm, tiles_per_seq), _vec_spec(sc, tm, tiles_per_seq),
            _pos_spec(cos_t, tm, tiles_per_seq), _pos_spec(sin_t, tm, tiles_per_seq),
            _full_spec(gq), _full_spec(gk), _full_spec(seg),
        ] + [_full_spec(w) for w in ws],
        out_specs=[row(4 * W_RET), row(W_DIFF), row(W_DIFF), row(W_DIFF)],
        out_shape=[
            jax.ShapeDtypeStruct((m, 4 * W_RET), F32),
            jax.ShapeDtypeStruct((m, W_DIFF), q_dtype),
            jax.ShapeDtypeStruct((m, W_DIFF), F32),
            jax.ShapeDtypeStruct((m, W_DIFF), F32),
        ],
        compiler_params=_params("parallel"),
        name="in_even",
    )(x2d, g, sh, sc, cos_t, sin_t, gq, gk, seg, *ws)


def _ret_kernel(q_ref, k_ref, v_ref, ga_ref, dm_ref, qd_ref, kd_ref, cd_ref, g_ref,
                o_ref, sfin_ref, s_scr, *, n_chunks):
    j = pl.program_id(1)

    @pl.when(j == 0)
    def _():
        s_scr[...] = jnp.zeros_like(s_scr)

    g = g_ref[...]
    for hd in range(H_RET):
        sl = slice(hd * DK_RET, (hd + 1) * DK_RET)
        dm = dm_ref[hd]
        qd = qd_ref[hd]
        kd = kd_ref[hd]
        cd = cd_ref[hd]

        def body(c, carry, sl=sl, dm=dm, qd=qd, kd=kd, cd=cd, hd=hd):
            r = pl.multiple_of(c * RET_CHUNK, RET_CHUNK)
            q = q_ref[pl.ds(r, RET_CHUNK), sl]
            k = k_ref[pl.ds(r, RET_CHUNK), sl]
            v = v_ref[pl.ds(r, RET_CHUNK), sl]
            vb = v.astype(BF16)
            s = s_scr[hd]
            att = _dot_nt(q.astype(BF16), k.astype(BF16)) * dm
            o = _bdot(att.astype(BF16), vb) + _bdot((q * qd).astype(BF16), s.astype(BF16))
            s_scr[hd] = cd * s + _dot_tn((k * kd).astype(BF16), vb)
            on = _rms(o) * g
            o_ref[pl.ds(r, RET_CHUNK), sl] = (on * jax.nn.silu(ga_ref[pl.ds(r, RET_CHUNK), sl])).astype(BF16)
            return carry

        lax.fori_loop(0, n_chunks, body, 0)

    @pl.when(j == pl.num_programs(1) - 1)
    def _():
        sfin_ref[...] = s_scr[...]


def _retention_prompt(ret_in, bsz, seq_len, dm, qd, kd, cd, g_ret):
    n_chunks = min(8, seq_len // RET_CHUNK)
    rows = n_chunks * RET_CHUNK
    nj = seq_len // rows
    col = lambda c: pl.BlockSpec((rows, W_RET), lambda b, j: (b * nj + j, c))
    return pl.pallas_call(
        functools.partial(_ret_kernel, n_chunks=n_chunks),
        grid=(bsz, nj),
        in_specs=[col(0), col(1), col(2), col(3), _full_spec(dm), _full_spec(qd), _full_spec(kd),
                  _full_spec(cd), _full_spec(g_ret)],
        out_specs=[
            pl.BlockSpec((rows, W_RET), lambda b, j: (b * nj + j, 0)),
            pl.BlockSpec((None, H_RET, DK_RET, DV_RET), lambda b, j: (b, 0, 0, 0)),
        ],
        out_shape=[
            jax.ShapeDtypeStruct((bsz * seq_len, W_RET), BF16),
            jax.ShapeDtypeStruct((bsz, H_RET, DK_RET, DV_RET), F32),
        ],
        scratch_shapes=[pltpu.VMEM((H_RET, DK_RET, DV_RET), F32)],
        compiler_params=_params("parallel", "arbitrary"),
        name="retention_prompt",
    )(ret_in, ret_in, ret_in, ret_in, dm, qd, kd, cd, g_ret)


def _ret_step_kernel(ret_ref, s_ref, qd_ref, cd_ref, g_ref, o_ref, snew_ref):
    g = g_ref[...]
    rows = lax.broadcasted_iota(jnp.int32, (DK_RET, DK_RET), 0)
    cols = lax.broadcasted_iota(jnp.int32, (DK_RET, DK_RET), 1)
    diag = rows == cols
    for hd in range(H_RET):
        lo = hd * DK_RET
        q = ret_ref[:, lo:lo + DK_RET]
        k = ret_ref[:, W_RET + lo:W_RET + lo + DK_RET]
        v = ret_ref[:, 2 * W_RET + lo:2 * W_RET + lo + DK_RET]
        ga = ret_ref[:, 3 * W_RET + lo:3 * W_RET + lo + DK_RET]
        s = s_ref[hd]
        s_hi, s_lo = _hi_lo(s)
        att = jnp.sum(q * k, axis=-1, keepdims=True)
        q8 = jnp.broadcast_to(q * qd_ref[hd], (8, DK_RET))
        o = att * v + _split_dot(q8, s_hi, s_lo)[0:1, :]
        k_diag = jnp.where(diag, jnp.broadcast_to(k, (DK_RET, DK_RET)), 0.0)
        v_rows = jnp.broadcast_to(v, (DK_RET, DV_RET))
        v_hi, v_lo = _hi_lo(v_rows)
        snew_ref[hd] = cd_ref[hd] * s + _split_dot(k_diag, v_hi, v_lo)
        on = _rms(o) * g
        o_ref[:, lo:lo + DK_RET] = on * jax.nn.silu(ga)


def _retention_step(ret_in, state, qd1, cd, g_ret):
    bsz = state.shape[0]
    ret3 = ret_in.reshape(bsz, 1, 4 * W_RET)
    o, s_new = pl.pallas_call(
        _ret_step_kernel,
        grid=(bsz,),
        in_specs=[
            pl.BlockSpec((None, 1, 4 * W_RET), lambda b: (b, 0, 0)),
            pl.BlockSpec((None, H_RET, DK_RET, DV_RET), lambda b: (b, 0, 0, 0)),
            _full_spec(qd1), _full_spec(cd), _full_spec(g_ret),
        ],
        out_specs=[
            pl.BlockSpec((None, 1, W_RET), lambda b: (b, 0, 0)),
            pl.BlockSpec((None, H_RET, DK_RET, DV_RET), lambda b: (b, 0, 0, 0)),
        ],
        out_shape=[
            jax.ShapeDtypeStruct((bsz, 1, W_RET), F32),
            jax.ShapeDtypeStruct((bsz, H_RET, DK_RET, DV_RET), F32),
        ],
        compiler_params=_params("parallel"),
        name="retention_step",
    )(ret3, state, qd1, cd, g_ret)
    return o.reshape(bsz, W_RET), s_new


def _lambda_value(lam_ref, lam_init):
    lv = lam_ref[...]
    s1 = jnp.sum(lv[0:1, :] * lv[1:2, :], axis=-1, keepdims=True)
    s2 = jnp.sum(lv[2:3, :] * lv[3:4, :], axis=-1, keepdims=True)
    return jnp.exp(s1) - jnp.exp(s2) + lam_init


def _diff_kernel(qi_ref, ki_ref, q_ref, k_ref, v_ref, lam_ref, g_ref, o_ref,
                 qs_scr, m_scr, l_scr, acc_scr, *, tq, lam_init):
    p = pl.program_id(2)
    qi = qi_ref[p]
    ki = ki_ref[p]

    @pl.when(ki == 0)
    def _():
        q = q_ref[...]
        lane = lax.broadcasted_iota(jnp.int32, q.shape, 1)
        zero = jnp.zeros_like(q)
        qs_scr[0:tq, :] = jnp.where(lane < DH_DIFF, q, zero)
        qs_scr[tq:2 * tq, :] = jnp.where(lane >= DH_DIFF, q, zero)
        m_scr[...] = jnp.full_like(m_scr, NEG_BIG)
        l_scr[...] = jnp.zeros_like(l_scr)
        acc_scr[...] = jnp.zeros_like(acc_scr)

    k = k_ref[...].astype(BF16)
    v = v_ref[...].astype(BF16)
    s = _dot_nt(qs_scr[...], k) * (DH_DIFF ** -0.5)
    row = lax.broadcasted_iota(jnp.int32, s.shape, 0)
    col = lax.broadcasted_iota(jnp.int32, s.shape, 1)
    row = jnp.where(row >= tq, row - tq, row)
    s = jnp.where(jnp.logical_or(ki < qi, col <= row), s, NEG_BIG)
    m_prev = m_scr[...]
    m_new = jnp.maximum(m_prev, jnp.max(s, axis=-1, keepdims=True))
    alpha = jnp.exp(m_prev - m_new)
    pr = jnp.exp(s - m_new)
    l_scr[...] = alpha * l_scr[...] + jnp.sum(pr, axis=-1, keepdims=True)
    acc_scr[...] = alpha * acc_scr[...] + _bdot(pr.astype(BF16), v)
    m_scr[...] = m_new

    @pl.when(ki == qi)
    def _():
        a = acc_scr[...] / l_scr[...]
        lam = _lambda_value(lam_ref, lam_init)
        o = a[0:tq, :] - lam * a[tq:2 * tq, :]
        o_ref[...] = (_rms(o) * g_ref[...] * (1.0 - lam_init)).astype(BF16)


def _diff_prompt(qb, kb, vb, lam_vec, g_d, bsz, seq_len, lam_init):
    tq = min(512, seq_len)
    nq = seq_len // tq
    pairs = [(i, j) for i in range(nq) for j in range(i + 1)]
    qi = jnp.asarray([a for a, _ in pairs], jnp.int32)
    ki = jnp.asarray([b for _, b in pairs], jnp.int32)
    grid_spec = pltpu.PrefetchScalarGridSpec(
        num_scalar_prefetch=2,
        grid=(bsz, H_DIFF, len(pairs)),
        in_specs=[
            pl.BlockSpec((tq, 2 * DH_DIFF), lambda b, h, p, qi, ki: (b * nq + qi[p], h)),
            pl.BlockSpec((tq, 2 * DH_DIFF), lambda b, h, p, qi, ki: (b * nq + ki[p], h)),
            pl.BlockSpec((tq, DV_DIFF), lambda b, h, p, qi, ki: (b * nq + ki[p], h)),
            pl.BlockSpec(lam_vec.shape, lambda b, h, p, qi, ki: (0, 0)),
            pl.BlockSpec(g_d.shape, lambda b, h, p, qi, ki: (0, 0)),
        ],
        out_specs=pl.BlockSpec((tq, DV_DIFF), lambda b, h, p, qi, ki: (b * nq + qi[p], h)),
        scratch_shapes=[
            pltpu.VMEM((2 * tq, 2 * DH_DIFF), BF16),
            pltpu.VMEM((2 * tq, 1), F32),
            pltpu.VMEM((2 * tq, 1), F32),
            pltpu.VMEM((2 * tq, DV_DIFF), F32),
        ],
    )
    return pl.pallas_call(
        functools.partial(_diff_kernel, tq=tq, lam_init=lam_init),
        grid_spec=grid_spec,
        out_shape=jax.ShapeDtypeStruct((bsz * seq_len, H_DIFF * DV_DIFF), BF16),
        compiler_params=_params("parallel", "parallel", "arbitrary"),
        name="diff_prompt",
    )(qi, ki, qb, kb, vb, lam_vec, g_d)


PAGES_PER_STEP = 8
N_MAPS = 2 * H_DIFF


def _group_sums(prod):
    return jnp.concatenate(
        [jnp.sum(prod[g * DH_DIFF:(g + 1) * DH_DIFF, :], axis=0, keepdims=True) for g in range(N_MAPS)], axis=0)


def _diff_step_kernel(pt_ref, q_ref, kn_ref, vn_ref, lam_ref, g_ref, *rest, lam_init):
    pps = PAGES_PER_STEP
    k_refs = rest[:pps]
    v_refs = rest[pps:2 * pps]
    o_ref = rest[2 * pps]
    qb_scr, m_scr, l_scr, acc_scr = rest[2 * pps + 1:]
    j = pl.program_id(1)
    scale = DH_DIFF ** -0.5

    @pl.when(j == 0)
    def _():
        qb_scr[...] = jnp.broadcast_to(q_ref[...], (W_DIFF, PAGE_SIZE))
        m_scr[...] = jnp.full_like(m_scr, NEG_BIG)
        l_scr[...] = jnp.zeros_like(l_scr)
        acc_scr[...] = jnp.zeros_like(acc_scr)

    scores = [_group_sums(k_refs[pg][...] * qb_scr[...]) * scale for pg in range(pps)]
    s_max = scores[0]
    for pg in range(1, pps):
        s_max = jnp.maximum(s_max, scores[pg])
    m_prev = m_scr[...]
    m_new = jnp.maximum(m_prev, jnp.max(s_max, axis=-1, keepdims=True))
    alpha = jnp.exp(m_prev - m_new)
    row_head = lax.broadcasted_iota(jnp.int32, (N_MAPS, DV_DIFF), 0) // 2
    p_sum = jnp.zeros((N_MAPS, PAGE_SIZE), F32)
    pv = jnp.zeros((N_MAPS, DV_DIFF), F32)
    for pg in range(pps):
        pr = jnp.exp(scores[pg] - m_new)
        p_sum = p_sum + pr
        p_bits = lax.bitcast_convert_type(pr, jnp.uint32) & jnp.uint32(0xFFFF0000)
        p_top = lax.bitcast_convert_type(p_bits, F32)
        p_hi = p_top.astype(BF16)
        p_both = jnp.concatenate([p_top, pr - p_top], axis=0).astype(BF16)
        for hd in range(H_DIFF):
            v_hi, v_lo = _hi_lo(v_refs[pg][pl.ds(hd, PAGE_SIZE, stride=H_DIFF), :])
            r = _bdot(p_both, v_hi)
            r = r[0:N_MAPS, :] + r[N_MAPS:2 * N_MAPS, :] + _bdot(p_hi, v_lo)
            pv = pv + jnp.where(row_head == hd, r, 0.0)
    l_scr[...] = alpha * l_scr[...] + jnp.sum(p_sum, axis=-1, keepdims=True)
    acc_scr[...] = alpha * acc_scr[...] + pv
    m_scr[...] = m_new

    @pl.when(j == pl.num_programs(1) - 1)
    def _():
        s = _group_sums(q_ref[...] * kn_ref[...]) * scale
        m_old = m_scr[...]
        m_fin = jnp.maximum(m_old, s)
        a_fin = jnp.exp(m_old - m_fin)
        pr = jnp.exp(s - m_fin)
        l_fin = a_fin * l_scr[...] + pr
        a = (a_fin * acc_scr[...] + pr * vn_ref[...]) / l_fin
        lam = _lambda_value(lam_ref, lam_init)
        g = g_ref[...]
        for hd in range(H_DIFF):
            d = a[2 * hd:2 * hd + 1, :] - lam * a[2 * hd + 1:2 * hd + 2, :]
            o_ref[hd:hd + 1, :] = _rms(d) * g * (1.0 - lam_init)


def _diff_step(qb, kb_new, vb_new, cache_k, cache_v, layer_idx, page_table, lam_vec, g_d, lam_init):
    bsz, n_pages = page_table.shape
    n_layers, n_pool = cache_k.shape[0:2]
    pps = PAGES_PER_STEP
    assert n_pages % pps == 0
    ck = jnp.transpose(cache_k, (0, 1, 3, 4, 5, 2)).reshape(n_layers * n_pool, W_DIFF, PAGE_SIZE)
    cv = cache_v.reshape(n_layers * n_pool, PAGE_SIZE * H_DIFF, DV_DIFF)
    page_table = page_table + layer_idx * n_pool
    col_spec = pl.BlockSpec((None, W_DIFF, 1), lambda b, j, pt: (b, 0, 0))
    v_new = jnp.repeat(vb_new.reshape(bsz, H_DIFF, DV_DIFF), 2, axis=1)

    def page_spec(pg):
        return pl.BlockSpec((None, W_DIFF, PAGE_SIZE), lambda b, j, pt, pg=pg: (pt[b, j * pps + pg], 0, 0))

    grid_spec = pltpu.PrefetchScalarGridSpec(
        num_scalar_prefetch=1,
        grid=(bsz, n_pages // pps),
        in_specs=[col_spec, col_spec,
                  pl.BlockSpec((None, N_MAPS, DV_DIFF), lambda b, j, pt: (b, 0, 0)),
                  pl.BlockSpec(lam_vec.shape, lambda b, j, pt: (0, 0)),
                  pl.BlockSpec(g_d.shape, lambda b, j, pt: (0, 0))]
                 + [page_spec(pg) for pg in range(pps)] + [page_spec(pg) for pg in range(pps)],
        out_specs=pl.BlockSpec((None, H_DIFF, DV_DIFF), lambda b, j, pt: (b, 0, 0)),
        scratch_shapes=[
            pltpu.VMEM((W_DIFF, PAGE_SIZE), F32),
            pltpu.VMEM((N_MAPS, 1), F32),
            pltpu.VMEM((N_MAPS, 1), F32),
            pltpu.VMEM((N_MAPS, DV_DIFF), F32),
        ],
    )
    o = pl.pallas_call(
        functools.partial(_diff_step_kernel, lam_init=lam_init),
        grid_spec=grid_spec,
        out_shape=jax.ShapeDtypeStruct((bsz, H_DIFF, DV_DIFF), F32),
        compiler_params=_params("parallel", "arbitrary"),
        name="diff_step",
    )(page_table, qb.reshape(bsz, W_DIFF, 1), kb_new.reshape(bsz, W_DIFF, 1), v_new, lam_vec, g_d,
      *([ck] * pps), *([cv] * pps))
    return o.reshape(bsz, W_DIFF)


def _out_even_kernel(a_ref, b_ref, x_ref, gt_ref, *rest):
    w_refs = rest[:-1]
    o_ref = rest[-1]
    m = _mm(a_ref[...], w_refs, slice(0, W_RET)) + _mm(b_ref[...], w_refs, slice(W_RET, 2 * W_RET))
    o_ref[...] = x_ref[...] + gt_ref[...] * m


def _out_even(o_ret, o_diff, ws, x2d, gt, tm, tiles_per_seq):
    m = x2d.shape[0]
    row = lambda n: pl.BlockSpec((tm, n), lambda i: (i, 0))
    return pl.pallas_call(
        _out_even_kernel,
        grid=(m // tm,),
        in_specs=[row(W_RET), row(W_RET), row(D_MODEL), _vec_spec(gt, tm, tiles_per_seq)]
                 + [_full_spec(w) for w in ws],
        out_specs=row(D_MODEL),
        out_shape=jax.ShapeDtypeStruct((m, D_MODEL), F32),
        compiler_params=_params("parallel"),
        name="out_even",
    )(o_ret, o_diff, x2d, gt, *ws)


def _ffn_kernel(x_ref, g_ref, sh_ref, sc_ref, gt_ref, *rest, n_w):
    w1, w3, w2 = rest[0:n_w], rest[n_w:2 * n_w], rest[2 * n_w:3 * n_w]
    o_ref = rest[3 * n_w]
    x = x_ref[...]
    h = _modulate(x, g_ref[...], sh_ref[...], sc_ref[...])
    if n_w == 1:
        h = h.astype(BF16)
    u = jax.nn.silu(_mm(h, w1)) * _mm(h, w3)
    o_ref[...] = x + gt_ref[...] * _mm(u, w2)


def _ffn(x2d, g, sh, sc, gt, w1s, w3s, w2s, tm, tiles_per_seq):
    m = x2d.shape[0]
    row = pl.BlockSpec((tm, D_MODEL), lambda i: (i, 0))
    vs = lambda a: _vec_spec(a, tm, tiles_per_seq)
    ws = (*w1s, *w3s, *w2s)
    return pl.pallas_call(
        functools.partial(_ffn_kernel, n_w=len(w1s)),
        grid=(m // tm,),
        in_specs=[row, _full_spec(g), vs(sh), vs(sc), vs(gt)] + [_full_spec(w) for w in ws],
        out_specs=row,
        out_shape=jax.ShapeDtypeStruct((m, D_MODEL), F32),
        compiler_params=_params("parallel"),
        name="ffn",
    )(x2d, g, sh, sc, gt, *ws)


def _modmm_kernel(x_ref, g_ref, sh_ref, sc_ref, *rest):
    w_refs = rest[:-1]
    o_ref = rest[-1]
    o_ref[...] = _mm(_modulate(x_ref[...], g_ref[...], sh_ref[...], sc_ref[...]), w_refs)


def _modmm(x2d, g, sh, sc, ws, tm, tiles_per_seq):
    m = x2d.shape[0]
    n = ws[0].shape[1]
    vs = lambda a: _vec_spec(a, tm, tiles_per_seq)
    return pl.pallas_call(
        _modmm_kernel,
        grid=(m // tm,),
        in_specs=[pl.BlockSpec((tm, D_MODEL), lambda i: (i, 0)), _full_spec(g), vs(sh), vs(sc)]
                 + [_full_spec(w) for w in ws],
        out_specs=pl.BlockSpec((tm, n), lambda i: (i, 0)),
        out_shape=jax.ShapeDtypeStruct((m, n), F32),
        compiler_params=_params("parallel"),
        name="modmm",
    )(x2d, g, sh, sc, *ws)


def _mm_at(x, w_refs, idx):
    if len(w_refs) == 1:
        return _bdot(x.astype(BF16), w_refs[0][idx])
    return _split_dot(x, w_refs[0][idx], w_refs[1][idx])


def _s5_input_drive(u, bmat_refs, j):
    return _mm_at(u[:, j * LANES:(j + 1) * LANES], bmat_refs, j)


def _s5_output_block(h_re_of, h_im_of, cmat_refs, jb):
    acc = None
    for q in range(4):
        r = 4 * jb + q
        t = (_mm_at(h_re_of(r), cmat_refs, (r, slice(0, LANES), slice(None)))
             + _mm_at(h_im_of(r), cmat_refs, (r, slice(LANES, 2 * LANES), slice(None))))
        acc = t if acc is None else acc + t
    return acc


def _s5_prompt_kernel(u_ref, are_ref, aim_ref, bmat_ref, cmat_ref, d_ref, z_ref, hre_ref, him_ref,
                      sre, sim, st_re, st_im, *, tc):
    j = pl.program_id(1)

    @pl.when(j == 0)
    def _():
        st_re[...] = jnp.zeros_like(st_re)
        st_im[...] = jnp.zeros_like(st_im)

    u = u_ref[...]
    half = 4 * LANES
    for jb in range(8):
        res = _s5_input_drive(u, (bmat_ref,), jb)
        for q in range(4):
            r = 4 * jb + q
            sre[pl.ds(r, tc, stride=N_PAIRS), :] = res[:, q * LANES:(q + 1) * LANES]
            sim[pl.ds(r, tc, stride=N_PAIRS), :] = res[:, half + q * LANES:half + (q + 1) * LANES]

    a_re = are_ref[...]
    a_im = aim_ref[...]

    def step(t, carry):
        h_re, h_im = carry
        base = pl.multiple_of(t * N_PAIRS, N_PAIRS)
        n_re = a_re * h_re - a_im * h_im + sre[pl.ds(base, N_PAIRS), :]
        n_im = a_re * h_im + a_im * h_re + sim[pl.ds(base, N_PAIRS), :]
        sre[pl.ds(base, N_PAIRS), :] = n_re
        sim[pl.ds(base, N_PAIRS), :] = n_im
        return n_re, n_im

    h_re, h_im = lax.fori_loop(0, tc, step, (st_re[...], st_im[...]))
    st_re[...] = h_re
    st_im[...] = h_im

    for jb in range(8):
        y = _s5_output_block(lambda r: sre[pl.ds(r, tc, stride=N_PAIRS), :],
                             lambda r: sim[pl.ds(r, tc, stride=N_PAIRS), :], (cmat_ref,), jb)
        sl = slice(jb * LANES, (jb + 1) * LANES)
        y = y + d_ref[:, sl] * u[:, sl]
        z_ref[:, sl] = jax.nn.gelu(y, approximate=True).astype(BF16)

    @pl.when(j == pl.num_programs(1) - 1)
    def _():
        hre_ref[...] = h_re
        him_ref[...] = h_im


def _s5_prompt(u2d, bsz, seq_len, a_re, a_im, bmat, cmat, d_row):
    tc = min(256, seq_len)
    nj = seq_len // tc
    state_spec = pl.BlockSpec((None, N_PAIRS, LANES), lambda b, j: (b, 0, 0))
    return pl.pallas_call(
        functools.partial(_s5_prompt_kernel, tc=tc),
        grid=(bsz, nj),
        in_specs=[pl.BlockSpec((tc, D_MODEL), lambda b, j: (b * nj + j, 0)),
                  _full_spec(a_re), _full_spec(a_im), _full_spec(bmat), _full_spec(cmat), _full_spec(d_row)],
        out_specs=[pl.BlockSpec((tc, D_MODEL), lambda b, j: (b * nj + j, 0)), state_spec, state_spec],
        out_shape=[
            jax.ShapeDtypeStruct((bsz * seq_len, D_MODEL), BF16),
            jax.ShapeDtypeStruct((bsz, N_PAIRS, LANES), F32),
            jax.ShapeDtypeStruct((bsz, N_PAIRS, LANES), F32),
        ],
        scratch_shapes=[
            pltpu.VMEM((tc * N_PAIRS, LANES), F32),
            pltpu.VMEM((tc * N_PAIRS, LANES), F32),
            pltpu.VMEM((N_PAIRS, LANES), F32),
            pltpu.VMEM((N_PAIRS, LANES), F32),
        ],
        compiler_params=_params("parallel", "arbitrary"),
        name="s5_prompt",
    )(u2d, a_re, a_im, bmat, cmat, d_row)


def _s5_step_kernel(u_ref, h0re_ref, h0im_ref, are_ref, aim_ref, bhi_ref, blo_ref, chi_ref, clo_ref, d_ref,
                    z_ref, hre_ref, him_ref):
    u = u_ref[...]
    half = 4 * LANES
    bmat_refs = (bhi_ref, blo_ref)
    cmat_refs = (chi_ref, clo_ref)
    for jb in range(8):
        res = _s5_input_drive(u, bmat_refs, jb)
        sl = slice(jb * half, (jb + 1) * half)
        a_re = are_ref[:, sl]
        a_im = aim_ref[:, sl]
        h_re = h0re_ref[:, sl]
        h_im = h0im_ref[:, sl]
        hre_ref[:, sl] = a_re * h_re - a_im * h_im + res[:, 0:half]
        him_ref[:, sl] = a_re * h_im + a_im * h_re + res[:, half:2 * half]
    for jb in range(8):
        y = _s5_output_block(lambda r: hre_ref[:, r * LANES:(r + 1) * LANES],
                             lambda r: him_ref[:, r * LANES:(r + 1) * LANES], cmat_refs, jb)
        sl = slice(jb * LANES, (jb + 1) * LANES)
        y = y + d_ref[:, sl] * u[:, sl]
        z_ref[:, sl] = jax.nn.gelu(y, approximate=True)


def _s5_step(u2d, h0_re, h0_im, a_re_row, a_im_row, bmats, cmats, d_row):
    bsz = u2d.shape[0]
    n_state = N_GROUPS * SSM_STATE
    args = (u2d, h0_re, h0_im, a_re_row, a_im_row, *bmats, *cmats, d_row)
    return pl.pallas_call(
        _s5_step_kernel,
        grid=(1,),
        in_specs=[_full_spec(a) for a in args],
        out_specs=[pl.BlockSpec((bsz, D_MODEL), lambda i: (0, 0)),
                   pl.BlockSpec((bsz, n_state), lambda i: (0, 0)),
                   pl.BlockSpec((bsz, n_state), lambda i: (0, 0))],
        out_shape=[
            jax.ShapeDtypeStruct((bsz, D_MODEL), F32),
            jax.ShapeDtypeStruct((bsz, n_state), F32),
            jax.ShapeDtypeStruct((bsz, n_state), F32),
        ],
        compiler_params=_params("arbitrary"),
        name="s5_step",
    )(*args)


def _glu_router_kernel(z_ref, x_ref, gt_ref, g_ref, sh_ref, sc_ref, wrh_ref, wrl_ref, br_ref, *rest):
    wg_refs = rest[:-3]
    x_out, h_out, route_out = rest[-3:]
    vg = _mm(z_ref[...], wg_refs)
    val = vg[:, 0:D_MODEL]
    gate = vg[:, D_MODEL:2 * D_MODEL]
    x = x_ref[...] + gt_ref[...] * (val * jax.nn.sigmoid(gate))
    x_out[...] = x
    h = _modulate(x, g_ref[...], sh_ref[...], sc_ref[...])
    h_out[...] = h.astype(BF16)
    logits = _split_dot(h, wrh_ref[...], wrl_ref[...]) + br_ref[...]
    lane = lax.broadcasted_iota(jnp.int32, logits.shape, 1)
    v1 = jnp.max(logits, axis=-1, keepdims=True)
    i1 = jnp.min(jnp.where(logits == v1, lane, LANES), axis=-1, keepdims=True)
    others = jnp.where(lane == i1, NEG_BIG, logits)
    v2 = jnp.max(others, axis=-1, keepdims=True)
    i2 = jnp.min(jnp.where(others == v2, lane, LANES), axis=-1, keepdims=True)
    e = jnp.exp(v2 - v1)
    g1 = 1.0 / (1.0 + e)
    g2 = e / (1.0 + e)
    route = jnp.where(lane == i1, g1, 0.0) + jnp.where(lane == i2, g2, 0.0)
    flag = jnp.logical_or(lane == i1 + N_EXPERTS, lane == i2 + N_EXPERTS)
    route_out[...] = route + jnp.where(flag, 1.0, 0.0)


def _glu_router(z, wgs, x2d, gt, g, sh, sc, wr_hi, wr_lo, br, tm, tiles_per_seq):
    m = x2d.shape[0]
    row = lambda n: pl.BlockSpec((tm, n), lambda i: (i, 0))
    vs = lambda a: _vec_spec(a, tm, tiles_per_seq)
    return pl.pallas_call(
        _glu_router_kernel,
        grid=(m // tm,),
        in_specs=[row(D_MODEL), row(D_MODEL), vs(gt), _full_spec(g), vs(sh), vs(sc),
                  _full_spec(wr_hi), _full_spec(wr_lo), _full_spec(br)] + [_full_spec(w) for w in wgs],
        out_specs=[row(D_MODEL), row(D_MODEL), row(LANES)],
        out_shape=[
            jax.ShapeDtypeStruct((m, D_MODEL), F32),
            jax.ShapeDtypeStruct((m, D_MODEL), BF16),
            jax.ShapeDtypeStruct((m, LANES), F32),
        ],
        compiler_params=_params("parallel"),
        name="glu_router",
    )(z, x2d, gt, g, sh, sc, wr_hi, wr_lo, br, *wgs)


def _moe_kernel(te_ref, nv_ref, xs_ref, gate_ref, w1_ref, w3_ref, w2_ref, o_ref, acc_scr):
    i = pl.program_id(0)
    f = pl.program_id(1)
    nf = pl.num_programs(1)
    valid = i < nv_ref[0]

    @pl.when(valid)
    def _():
        x = xs_ref[...]
        a = _bdot(x, w1_ref[...].astype(BF16))
        b = _bdot(x, w3_ref[...].astype(BF16))
        u = (jax.nn.silu(a) * b).astype(BF16)
        contrib = _bdot(u, w2_ref[...].astype(BF16))

        @pl.when(f == 0)
        def _():
            acc_scr[...] = contrib

        @pl.when(f > 0)
        def _():
            acc_scr[...] = acc_scr[...] + contrib

        @pl.when(f == nf - 1)
        def _():
            o_ref[...] = acc_scr[...] * gate_ref[...]

    @pl.when(jnp.logical_and(jnp.logical_not(valid), f == nf - 1))
    def _():
        o_ref[...] = jnp.zeros_like(o_ref)


def _moe_experts(xs, row_gate, tile_expert, n_valid, w1, w3, w2, tm):
    rows = xs.shape[0]
    n_tiles = rows // tm
    tf = 512
    nf = D_FF_EXPERT // tf

    def fsel(i, f, nv):
        return jnp.where(i < nv[0], f, nf - 1)

    grid_spec = pltpu.PrefetchScalarGridSpec(
        num_scalar_prefetch=2,
        grid=(n_tiles, nf),
        in_specs=[
            pl.BlockSpec((tm, D_MODEL), lambda i, f, te, nv: (i, 0)),
            pl.BlockSpec((tm, 1), lambda i, f, te, nv: (i, 0)),
            pl.BlockSpec((None, D_MODEL, tf), lambda i, f, te, nv: (te[i], 0, fsel(i, f, nv))),
            pl.BlockSpec((None, D_MODEL, tf), lambda i, f, te, nv: (te[i], 0, fsel(i, f, nv))),
            pl.BlockSpec((None, tf, D_MODEL), lambda i, f, te, nv: (te[i], fsel(i, f, nv), 0)),
        ],
        out_specs=pl.BlockSpec((tm, D_MODEL), lambda i, f, te, nv: (i, 0)),
        scratch_shapes=[pltpu.VMEM((tm, D_MODEL), F32)],
    )
    return pl.pallas_call(
        _moe_kernel,
        grid_spec=grid_spec,
        out_shape=jax.ShapeDtypeStruct((rows, D_MODEL), F32),
        compiler_params=_params("parallel", "arbitrary"),
        name="moe_experts",
    )(tile_expert, n_valid, xs, row_gate, w1, w3, w2)


def _moe_residual_kernel(x_ref, gt_ref, a_ref, b_ref, o_ref):
    o_ref[...] = x_ref[...] + gt_ref[...] * (a_ref[...] + b_ref[...])


def _moe_residual(x2d, gt, ya, yb, tm, tiles_per_seq):
    m = x2d.shape[0]
    row = pl.BlockSpec((tm, D_MODEL), lambda i: (i, 0))
    return pl.pallas_call(
        _moe_residual_kernel,
        grid=(m // tm,),
        in_specs=[row, _vec_spec(gt, tm, tiles_per_seq), row, row],
        out_specs=row,
        out_shape=jax.ShapeDtypeStruct((m, D_MODEL), F32),
        compiler_params=_params("parallel"),
        name="moe_residual",
    )(x2d, gt, ya, yb)


def _moe(h_bf, route, x2d, gt, w1, w3, w2, tm_rows, tm, tiles_per_seq):
    m = h_bf.shape[0]
    comb = route[:, 0:N_EXPERTS]
    sel = route[:, N_EXPERTS:2 * N_EXPERTS] > 0.5
    seli = sel.astype(jnp.int32)
    counts = jnp.sum(seli, axis=0)
    padded = ((counts + tm_rows - 1) // tm_rows) * tm_rows
    ends = jnp.cumsum(padded)
    offsets = ends - padded
    rank = jnp.cumsum(seli, axis=0) - 1
    pos = offsets[None, :] + rank
    n_rows = ((2 * m + N_EXPERTS * (tm_rows - 1)) // tm_rows + 1) * tm_rows
    n_tiles = n_rows // tm_rows
    flat_pos = jnp.where(sel, pos, n_rows).reshape(-1)
    tok = jnp.broadcast_to(jnp.arange(m, dtype=jnp.int32)[:, None], (m, N_EXPERTS)).reshape(-1)
    row_token = jnp.zeros((n_rows,), jnp.int32).at[flat_pos].set(tok, mode="drop")
    row_gate = jnp.zeros((n_rows,), F32).at[flat_pos].set(comb.reshape(-1), mode="drop")
    tile_start = jnp.arange(n_tiles, dtype=jnp.int32) * tm_rows
    n_valid = (ends[-1] // tm_rows).astype(jnp.int32)
    first_row = jnp.minimum(tile_start, ends[-1] - 1)
    tile_expert = jnp.sum((ends[None, :] <= first_row[:, None]).astype(jnp.int32), axis=1)
    xs = jnp.take(h_bf, row_token, axis=0)
    ys = _moe_experts(xs, row_gate.reshape(n_rows, 1), tile_expert, n_valid.reshape(1), w1, w3, w2, tm_rows)
    masked = jnp.where(sel, pos, n_rows)
    slot_a = jnp.min(masked, axis=1)
    slot_b = jnp.min(jnp.where(masked == slot_a[:, None], n_rows, masked), axis=1)
    ya = jnp.take(ys, slot_a, axis=0)
    yb = jnp.take(ys, slot_b, axis=0)
    return _moe_residual(x2d, gt, ya, yb, tm, tiles_per_seq)


def _retention_constants(chunk):
    log_g = jnp.log1p(-jnp.exp2(-5.0 - jnp.arange(H_RET, dtype=F32)))
    idx = jnp.arange(chunk, dtype=F32)
    rel = idx[:, None] - idx[None, :]
    dm = jnp.where(rel >= 0, jnp.exp(log_g[:, None, None] * jnp.maximum(rel, 0.0)), 0.0)
    q_decay = jnp.exp(log_g[:, None] * (idx[None, :] + 1.0))
    k_decay = jnp.exp(log_g[:, None] * (chunk - 1.0 - idx[None, :]))
    chunk_decay = jnp.exp(log_g * chunk)
    qd = jnp.broadcast_to(q_decay[:, :, None], (H_RET, chunk, DK_RET))
    kd = jnp.broadcast_to(k_decay[:, :, None], (H_RET, chunk, DK_RET))
    cd = jnp.broadcast_to(chunk_decay[:, None, None], (H_RET, 1, DV_RET))
    return dm, qd, kd, cd


def _rotary_tables(pos):
    half = DK_RET // 2
    inv = np.float64(ROPE_BASE) ** (-np.arange(half, dtype=np.float64) / half)
    inv_hi = inv.astype(np.float32)
    inv_lo = (inv - inv_hi.astype(np.float64)).astype(np.float32)
    p = jnp.asarray(pos).astype(F32)[:, None]
    ang = p * jnp.asarray(inv_hi)[None, :] + p * jnp.asarray(inv_lo)[None, :]
    cos = jnp.cos(ang)
    sin = jnp.sin(ang)
    return jnp.concatenate([cos, cos], axis=-1), jnp.concatenate([-sin, sin], axis=-1)


def _s5_matrices(lam_re, lam_im, log_step, b_re, b_im, c_re, c_im, d_skip):
    lr = lam_re.astype(F32)
    li = lam_im.astype(F32)
    step = jnp.exp(log_step.astype(F32))[:, None]
    mag = jnp.exp(lr * step)
    ab_re = mag * jnp.cos(li * step)
    ab_im = mag * jnp.sin(li * step)
    nr = ab_re - 1.0
    ni = ab_im
    den = lr * lr + li * li
    kr = (nr * lr + ni * li) / den
    ki = (ni * lr - nr * li) / den
    br = b_re.astype(F32)
    bi = b_im.astype(F32)
    bb_re = kr[..., None] * br - ki[..., None] * bi
    bb_im = kr[..., None] * bi + ki[..., None] * br
    eye8 = jnp.eye(8, dtype=F32)

    def drive(bb):
        t = bb.reshape(8, 8, SSM_STATE, SSM_GROUP).transpose(0, 1, 3, 2)
        t = t[:, :, :, None, :] * eye8[None, :, None, :, None]
        return t.reshape(8, LANES, 8 * SSM_STATE)

    bmat = jnp.concatenate([drive(bb_re), drive(bb_im)], axis=-1)
    r_idx = jnp.arange(N_PAIRS)[:, None, None]
    gl_idx = jnp.arange(2)[None, :, None]
    gb_idx = jnp.arange(8)[None, None, :]
    onehot = (gb_idx == 2 * (r_idx % 4) + gl_idx).astype(F32)

    def readout(c):
        t = c.astype(F32).reshape(N_PAIRS, 2, SSM_GROUP, SSM_STATE).transpose(0, 1, 3, 2)
        t = t[:, :, :, None, :] * onehot[:, :, None, :, None]
        return t.reshape(N_PAIRS, LANES, LANES)

    cmat = jnp.concatenate([readout(c_re), -readout(c_im)], axis=1)
    return (ab_re.reshape(N_PAIRS, LANES), ab_im.reshape(N_PAIRS, LANES), bmat, cmat,
            d_skip.astype(F32).reshape(1, D_MODEL))


def _trunk(x, mods, pos, ret0, ssm0, paged, p):
    bsz, seq_len, _ = x.shape
    m = bsz * seq_len
    prompt = seq_len > 1
    depth = p["w_ada"].shape[0]
    x2d = x.reshape(m, D_MODEL)
    if prompt:
        tm = min(512, seq_len)
        tps = seq_len // tm
        tm_ffn = min(256, seq_len)
        tps_ffn = seq_len // tm_ffn
        vec = lambda v: v.reshape(bsz, 1, D_MODEL)
    else:
        tm = tm_ffn = m
        tps = tps_ffn = 1
        vec = lambda v: v.reshape(1, bsz, D_MODEL)
    row = lambda v: v.reshape(1, -1).astype(F32)
    wsp = lambda w: _wsplit(w, not prompt)
    rets, ks_new, vs_new, srs, sis = [], [], [], [], []
    for layer in range(depth):
        sh1, sc1, gt1, sh2, sc2, gt2 = [vec(v) for v in jnp.split(mods[layer], N_ADA, axis=-1)]
        i = layer // 2
        if layer % 2 == 0:
            lam_init = 0.8 - 0.6 * math.exp(-0.3 * layer)
            cos_t, sin_t = _rotary_tables(pos)
            seg = (jnp.arange(W_DIFF)[:, None] // DH_DIFF == jnp.arange(W_DIFF)[None, :] // DH_DIFF).astype(BF16)
            gq = jnp.tile(row(p["g_qnorm"][i]), (1, W_DIFF // DH_DIFF))
            gk = jnp.tile(row(p["g_knorm"][i]), (1, W_DIFF // DH_DIFF))
            ret_in, qb, kb, vb = _in_even(
                x2d, row(p["g_norm1"][layer]), sh1, sc1, cos_t, sin_t, wsp(p["w_in_mix"][i]),
                gq, gk, seg, tm, tps, BF16 if prompt else F32)
            lam_vec = jnp.stack([p["lam_q1"][i], p["lam_k1"][i], p["lam_q2"][i], p["lam_k2"][i]]).astype(F32)
            g_ret = row(p["g_ret"][i])
            g_d = row(p["g_dnorm"][i])
            if prompt:
                dm, qd, kd, cd = _retention_constants(math.gcd(seq_len, RET_CHUNK))
                o_ret, s_new = _retention_prompt(ret_in, bsz, seq_len, dm, qd, kd, cd, g_ret)
                o_diff = _diff_prompt(qb, kb, vb, lam_vec, g_d, bsz, seq_len, lam_init)
            else:
                _, qd, _, cd = _retention_constants(1)
                o_ret, s_new = _retention_step(ret_in, ret0[i].astype(F32), qd, cd, g_ret)
                o_diff = _diff_step(qb, kb, vb, paged["cache_k"], paged["cache_v"], i,
                                    paged["page_table"], lam_vec, g_d, lam_init)
            rets.append(s_new)
            ks_new.append(kb.reshape(bsz, seq_len, H_DIFF, 2, DH_DIFF))
            vs_new.append(vb.reshape(bsz, seq_len, H_DIFF, DV_DIFF))
            x2d = _out_even(o_ret, o_diff, wsp(p["w_out_mix"][i]), x2d, gt1, tm, tps)
            x2d = _ffn(x2d, row(p["g_norm2"][layer]), sh2, sc2, gt2, wsp(p["w_ffn1"][i]),
                       wsp(p["w_ffn3"][i]), wsp(p["w_ffn2"][i]), tm_ffn, tps_ffn)
        else:
            a_re, a_im, bmat, cmat, d_row = _s5_matrices(
                p["lam_re"][i], p["lam_im"][i], p["log_step"][i], p["b_re"][i], p["b_im"][i],
                p["c_re"][i], p["c_im"][i], p["d_skip"][i])
            u2d = _modmm(x2d, row(p["g_norm1"][layer]), sh1, sc1, wsp(p["w_ssm_in"][i]), tm, tps)
            n_state = N_GROUPS * SSM_STATE
            if prompt:
                z, h_re, h_im = _s5_prompt(u2d, bsz, seq_len, a_re, a_im, bmat.astype(BF16), cmat.astype(BF16),
                                           d_row)
            else:
                z, h_re, h_im = _s5_step(
                    u2d, ssm0[0][i].astype(F32).reshape(bsz, n_state), ssm0[1][i].astype(F32).reshape(bsz, n_state),
                    a_re.reshape(1, n_state), a_im.reshape(1, n_state), wsp(bmat), wsp(cmat), d_row)
            srs.append(h_re.reshape(bsz, N_GROUPS, SSM_STATE))
            sis.append(h_im.reshape(bsz, N_GROUPS, SSM_STATE))
            wr = jnp.zeros((D_MODEL, LANES), F32).at[:, 0:N_EXPERTS].set(p["w_router"][i].astype(F32))
            wr_hi, wr_lo = _wsplit(wr, True)
            br = jnp.full((1, LANES), NEG_BIG, F32).at[0, 0:N_EXPERTS].set(p["b_router"][i].astype(F32))
            x2d, h_bf, route = _glu_router(
                z, wsp(p["w_glu"][i]), x2d, gt1, row(p["g_norm2"][layer]), sh2, sc2,
                wr_hi, wr_lo, br, tm_ffn, tps_ffn)
            x2d = _moe(h_bf, route, x2d, gt2, p["w_moe1"][i], p["w_moe3"][i], p["w_moe2"][i],
                       512 if prompt else 32, tm, tps)
    return (x2d.reshape(bsz, seq_len, D_MODEL), jnp.stack(rets), jnp.stack(ks_new), jnp.stack(vs_new),
            jnp.stack(srs), jnp.stack(sis))


def kernel(x_prompt, x_sample, state_ret, cache_k, cache_v, state_ssm_re, state_ssm_im, page_table, c_prompt, c_sample, w_ada, b_ada, g_norm1, g_norm2, w_in_mix, w_out_mix, g_ret, g_qnorm, g_knorm, g_dnorm, lam_q1, lam_k1, lam_q2, lam_k2, w_ffn1, w_ffn3, w_ffn2, w_ssm_in, lam_re, lam_im, log_step, b_re, b_im, c_re, c_im, d_skip, w_glu, w_router, b_router, w_moe1, w_moe3, w_moe2):
    p = dict(w_ada=w_ada, g_norm1=g_norm1, g_norm2=g_norm2, w_in_mix=w_in_mix, w_out_mix=w_out_mix,
             g_ret=g_ret, g_qnorm=g_qnorm, g_knorm=g_knorm, g_dnorm=g_dnorm, lam_q1=lam_q1, lam_k1=lam_k1,
             lam_q2=lam_q2, lam_k2=lam_k2, w_ffn1=w_ffn1, w_ffn3=w_ffn3, w_ffn2=w_ffn2, w_ssm_in=w_ssm_in,
             lam_re=lam_re, lam_im=lam_im, log_step=log_step, b_re=b_re, b_im=b_im, c_re=c_re, c_im=c_im,
             d_skip=d_skip, w_glu=w_glu, w_router=w_router, b_router=b_router,
             w_moe1=w_moe1, w_moe3=w_moe3, w_moe2=w_moe2)
    n_prompt = c_prompt.shape[0]
    n_sample = c_sample.shape[0]
    n_all = n_prompt + n_sample
    rows = ((n_all + 7) // 8) * 8
    c_all = jnp.zeros((rows, D_MODEL), F32).at[0:n_prompt].set(c_prompt).at[n_prompt:n_all].set(c_sample)
    mods = _ada(c_all, w_ada, b_ada)
    seq_len = x_prompt.shape[1]
    past_len = page_table.shape[1] * PAGE_SIZE
    y_p, ret_p, k_p, v_p, sr_p, si_p = _trunk(
        x_prompt, mods[:, 0:n_prompt], np.arange(seq_len), None, None, None, p)
    paged = dict(cache_k=cache_k, cache_v=cache_v, page_table=page_table)
    y_s, ret_s, k_s, v_s, sr_s, si_s = _trunk(
        x_sample, mods[:, n_prompt:n_all], past_len + np.arange(x_sample.shape[1]),
        state_ret, (state_ssm_re, state_ssm_im), paged, p)
    return (y_p, y_s, ret_p, ret_s, k_p, v_p, k_s, v_s, sr_p, si_p, sr_s, si_s)
```

```python
import functools
import math

import jax
import jax.numpy as jnp
import numpy as np
from jax import lax
from jax.experimental import pallas as pl
from jax.experimental.pallas import tpu as pltpu

F32 = jnp.float32
BF16 = jnp.bfloat16

D_MODEL = 1024
H_RET = 4
DK_RET = 128
DV_RET = 128
RET_CHUNK = 128
ROPE_BASE = 10000.0
H_DIFF = 4
DH_DIFF = 64
DV_DIFF = 128
PAGE_SIZE = 128
W_RET = H_RET * DK_RET
W_DIFF = H_DIFF * 2 * DH_DIFF
IN_EVEN = 4 * W_RET + 3 * W_DIFF
SSM_GROUP = 16
N_GROUPS = D_MODEL // SSM_GROUP
SSM_STATE = 64
N_PAIRS = N_GROUPS // 2
STEP_PITCH = N_PAIRS + 8
D_FF = 2816
N_EXPERTS = 8
D_FF_EXPERT = 3584
N_ADA = 6
NORM_EPS = 1e-6
LANES = 128
NEG_BIG = -1e30

VMEM_LIMIT_BYTES = 56 * 1024 * 1024


def _params(*sem):
    return pltpu.CompilerParams(dimension_semantics=sem, vmem_limit_bytes=VMEM_LIMIT_BYTES)


def _bdot(a, b):
    return jnp.dot(a, b, preferred_element_type=F32)


def _dot_nt(a, b):
    return lax.dot_general(a, b, (((1,), (1,)), ((), ())), preferred_element_type=F32)


def _dot_tn(a, b):
    return lax.dot_general(a, b, (((0,), (0,)), ((), ())), preferred_element_type=F32)


def _rms(x):
    return x * lax.rsqrt(jnp.mean(x * x, axis=-1, keepdims=True) + NORM_EPS)


def _modulate(x, g, shift, scale):
    return (_rms(x) * g) * (1.0 + scale) + shift


def _hi_lo(x):
    bits = lax.bitcast_convert_type(x, jnp.uint32) & jnp.uint32(0xFFFF0000)
    hi = lax.bitcast_convert_type(bits, F32)
    return hi.astype(BF16), (x - hi).astype(BF16)


def _split_dot(x, w_hi, w_lo):
    x_hi, x_lo = _hi_lo(x)
    return _bdot(x_hi, w_hi) + _bdot(x_lo, w_hi) + _bdot(x_hi, w_lo)


def _mm(x, w_refs, rows=slice(None)):
    if len(w_refs) == 1:
        return _bdot(x.astype(BF16), w_refs[0][rows, :])
    return _split_dot(x.astype(F32), w_refs[0][rows, :], w_refs[1][rows, :])


def _wsplit(w, precise):
    w = w.astype(F32)
    if not precise:
        return (w.astype(BF16),)
    return _hi_lo(w)


def _ada_kernel(c_ref, w_ref, b_ref, o_ref):
    w_hi, w_lo = _hi_lo(w_ref[...])
    o_ref[...] = _split_dot(jax.nn.silu(c_ref[...]), w_hi, w_lo) + b_ref[...]


def _ada(c_all, w_ada, b_ada):
    depth, _, n = w_ada.shape
    rows = c_all.shape[0]
    tn = 1536
    return pl.pallas_call(
        _ada_kernel,
        grid=(depth, n // tn),
        in_specs=[
            pl.BlockSpec((rows, D_MODEL), lambda l, j: (0, 0)),
            pl.BlockSpec((None, D_MODEL, tn), lambda l, j: (l, 0, j)),
            pl.BlockSpec((None, 1, tn), lambda l, j: (l, 0, j)),
        ],
        out_specs=pl.BlockSpec((None, rows, tn), lambda l, j: (l, 0, j)),
        out_shape=jax.ShapeDtypeStruct((depth, rows, n), F32),
        compiler_params=_params("parallel", "parallel"),
        name="ada",
    )(c_all, w_ada, b_ada.reshape(depth, 1, n))


def _vec_spec(arr, tm, tiles_per_seq):
    if arr.shape[1] == 1:
        return pl.BlockSpec((None, 1, arr.shape[2]), lambda i, *_: (i // tiles_per_seq, 0, 0))
    return pl.BlockSpec((None, tm, arr.shape[2]), lambda i, *_: (0, 0, 0))


def _pos_spec(arr, tm, tiles_per_seq):
    if arr.shape[0] == 1:
        return pl.BlockSpec((1, arr.shape[1]), lambda i, *_: (0, 0))
    return pl.BlockSpec((tm, arr.shape[1]), lambda i, *_: (i % tiles_per_seq, 0))


def _full_spec(arr):
    nd = arr.ndim
    return pl.BlockSpec(arr.shape, lambda *_: (0,) * nd)


def _in_even_kernel(x_ref, g_ref, sh_ref, sc_ref, cos_ref, sin_ref, gq_ref, gk_ref,
                    seg_ref, *rest, n_w):
    w_refs = rest[:n_w]
    ret_ref, qb_ref, kb_ref, vb_ref = rest[n_w:n_w + 4]
    vt_ref = rest[n_w + 4] if len(rest) > n_w + 4 else None
    y = _mm(_modulate(x_ref[...], g_ref[...], sh_ref[...], sc_ref[...]), w_refs)
    cos = cos_ref[...]
    sin = sin_ref[...]
    for part in range(2):
        for hd in range(H_RET):
            lo = part * W_RET + hd * DK_RET
            blk = y[:, lo:lo + DK_RET]
            rot = blk * cos + pltpu.roll(blk, DK_RET // 2, axis=1) * sin
            if part == 1:
                rot = rot * (DK_RET ** -0.5)
            ret_ref[:, lo:lo + DK_RET] = rot
    ret_ref[:, 2 * W_RET:4 * W_RET] = y[:, 2 * W_RET:4 * W_RET]
    seg = seg_ref[...]
    base = 4 * W_RET

    def qk_norm(blk, g):
        sq_hi, sq_lo = _hi_lo(blk * blk)
        ms = (_bdot(sq_hi, seg) + _bdot(sq_lo, seg)) * (1.0 / DH_DIFF)
        return blk * lax.rsqrt(ms + NORM_EPS) * g

    qb_ref[...] = qk_norm(y[:, base:base + W_DIFF], gq_ref[...]).astype(qb_ref.dtype)
    kb_ref[...] = qk_norm(y[:, base + W_DIFF:base + 2 * W_DIFF], gk_ref[...])
    v = y[:, base + 2 * W_DIFF:base + 3 * W_DIFF]
    vb_ref[...] = v
    if vt_ref is not None:
        vt_ref[...] = v.T.astype(BF16)


def _in_even(x2d, g, sh, sc, cos_t, sin_t, ws, gq, gk, seg, tm, tiles_per_seq, q_dtype, emit_vt):
    m = x2d.shape[0]
    row = lambda n: pl.BlockSpec((tm, n), lambda i: (i, 0))
    out_specs = [row(4 * W_RET), row(W_DIFF), row(W_DIFF), row(W_DIFF)]
    out_shape = [
        jax.ShapeDtypeStruct((m, 4 * W_RET), F32),
        jax.ShapeDtypeStruct((m, W_DIFF), q_dtype),
        jax.ShapeDtypeStruct((m, W_DIFF), F32),
        jax.ShapeDtypeStruct((m, W_DIFF), F32),
    ]
    if emit_vt:
        out_specs.append(pl.BlockSpec((W_DIFF, tm), lambda i: (0, i)))
        out_shape.append(jax.ShapeDtypeStruct((W_DIFF, m), BF16))
    return pl.pallas_call(
        functools.partial(_in_even_kernel, n_w=len(ws)),
        grid=(m // tm,),
        in_specs=[
            row(D_MODEL), _full_spec(g), _vec_spec(sh, tm, tiles_per_seq), _vec_spec(sc, tm, tiles_per_seq),
            _pos_spec(cos_t, tm, tiles_per_seq), _pos_spec(sin_t, tm, tiles_per_seq),
            _full_spec(gq), _full_spec(gk), _full_spec(seg),
        ] + [_full_spec(w) for w in ws],
        out_specs=out_specs,
        out_shape=out_shape,
        compiler_params=_params("parallel"),
        name="in_even",
    )(x2d, g, sh, sc, cos_t, sin_t, gq, gk, seg, *ws)


def _ret_kernel(q_ref, k_ref, v_ref, ga_ref, dm_ref, qd_ref, kd_ref, cd_ref, g_ref,
                o_ref, sfin_ref, s_scr, *, n_chunks):
    j = pl.program_id(1)

    @pl.when(j == 0)
    def _():
        s_scr[...] = jnp.zeros_like(s_scr)

    g = g_ref[...]
    for hd in range(H_RET):
        sl = slice(hd * DK_RET, (hd + 1) * DK_RET)
        dm = dm_ref[hd]
        qd = qd_ref[hd]
        kd = kd_ref[hd]
        cd = cd_ref[hd]

        def body(c, carry, sl=sl, dm=dm, qd=qd, kd=kd, cd=cd, hd=hd):
            r = pl.multiple_of(c * RET_CHUNK, RET_CHUNK)
            q = q_ref[pl.ds(r, RET_CHUNK), sl]
            k = k_ref[pl.ds(r, RET_CHUNK), sl]
            v = v_ref[pl.ds(r, RET_CHUNK), sl]
            vb = v.astype(BF16)
            s = s_scr[hd]
            att = _dot_nt(q.astype(BF16), k.astype(BF16)) * dm
            o = _bdot(att.astype(BF16), vb) + _bdot((q * qd).astype(BF16), s.astype(BF16))
            s_scr[hd] = cd * s + _dot_tn((k * kd).astype(BF16), vb)
            on = _rms(o) * g
            o_ref[pl.ds(r, RET_CHUNK), sl] = (on * jax.nn.silu(ga_ref[pl.ds(r, RET_CHUNK), sl])).astype(BF16)
            return carry

        lax.fori_loop(0, n_chunks, body, 0)

    @pl.when(j == pl.num_programs(1) - 1)
    def _():
        sfin_ref[...] = s_scr[...]


def _retention_prompt(ret_in, bsz, seq_len, dm, qd, kd, cd, g_ret):
    n_chunks = min(8, seq_len // RET_CHUNK)
    rows = n_chunks * RET_CHUNK
    nj = seq_len // rows
    col = lambda c: pl.BlockSpec((rows, W_RET), lambda b, j: (b * nj + j, c))
    return pl.pallas_call(
        functools.partial(_ret_kernel, n_chunks=n_chunks),
        grid=(bsz, nj),
        in_specs=[col(0), col(1), col(2), col(3), _full_spec(dm), _full_spec(qd), _full_spec(kd),
                  _full_spec(cd), _full_spec(g_ret)],
        out_specs=[
            pl.BlockSpec((rows, W_RET), lambda b, j: (b * nj + j, 0)),
            pl.BlockSpec((None, H_RET, DK_RET, DV_RET), lambda b, j: (b, 0, 0, 0)),
        ],
        out_shape=[
            jax.ShapeDtypeStruct((bsz * seq_len, W_RET), BF16),
            jax.ShapeDtypeStruct((bsz, H_RET, DK_RET, DV_RET), F32),
        ],
        scratch_shapes=[pltpu.VMEM((H_RET, DK_RET, DV_RET), F32)],
        compiler_params=_params("parallel", "arbitrary"),
        name="retention_prompt",
    )(ret_in, ret_in, ret_in, ret_in, dm, qd, kd, cd, g_ret)


def _ret_step_kernel(ret_ref, s_ref, qd_ref, cd_ref, g_ref, o_ref, snew_ref):
    g = g_ref[...]
    rows = lax.broadcasted_iota(jnp.int32, (DK_RET, DK_RET), 0)
    cols = lax.broadcasted_iota(jnp.int32, (DK_RET, DK_RET), 1)
    diag = rows == cols
    for hd in range(H_RET):
        lo = hd * DK_RET
        q = ret_ref[:, lo:lo + DK_RET]
        k = ret_ref[:, W_RET + lo:W_RET + lo + DK_RET]
        v = ret_ref[:, 2 * W_RET + lo:2 * W_RET + lo + DK_RET]
        ga = ret_ref[:, 3 * W_RET + lo:3 * W_RET + lo + DK_RET]
        s = s_ref[hd]
        s_hi, s_lo = _hi_lo(s)
        att = jnp.sum(q * k, axis=-1, keepdims=True)
        q8 = jnp.broadcast_to(q * qd_ref[hd], (8, DK_RET))
        o = att * v + _split_dot(q8, s_hi, s_lo)[0:1, :]
        k_diag = jnp.where(diag, jnp.broadcast_to(k, (DK_RET, DK_RET)), 0.0)
        v_rows = jnp.broadcast_to(v, (DK_RET, DV_RET))
        v_hi, v_lo = _hi_lo(v_rows)
        snew_ref[hd] = cd_ref[hd] * s + _split_dot(k_diag, v_hi, v_lo)
        on = _rms(o) * g
        o_ref[:, lo:lo + DK_RET] = on * jax.nn.silu(ga)


def _retention_step(ret_in, state, qd1, cd, g_ret):
    bsz = state.shape[0]
    ret3 = ret_in.reshape(bsz, 1, 4 * W_RET)
    o, s_new = pl.pallas_call(
        _ret_step_kernel,
        grid=(bsz,),
        in_specs=[
            pl.BlockSpec((None, 1, 4 * W_RET), lambda b: (b, 0, 0)),
            pl.BlockSpec((None, H_RET, DK_RET, DV_RET), lambda b: (b, 0, 0, 0)),
            _full_spec(qd1), _full_spec(cd), _full_spec(g_ret),
        ],
        out_specs=[
            pl.BlockSpec((None, 1, W_RET), lambda b: (b, 0, 0)),
            pl.BlockSpec((None, H_RET, DK_RET, DV_RET), lambda b: (b, 0, 0, 0)),
        ],
        out_shape=[
            jax.ShapeDtypeStruct((bsz, 1, W_RET), F32),
            jax.ShapeDtypeStruct((bsz, H_RET, DK_RET, DV_RET), F32),
        ],
        compiler_params=_params("parallel"),
        name="retention_step",
    )(ret3, state, qd1, cd, g_ret)
    return o.reshape(bsz, W_RET), s_new


def _lambda_value(lam_ref, lam_init):
    lv = lam_ref[...]
    s1 = jnp.sum(lv[0:1, :] * lv[1:2, :], axis=-1, keepdims=True)
    s2 = jnp.sum(lv[2:3, :] * lv[3:4, :], axis=-1, keepdims=True)
    return jnp.exp(s1) - jnp.exp(s2) + lam_init


Q_GROUP = 512
LOG2E = 1.4426950408889634


def _diff_kernel(qi_ref, ki_ref, q_ref, k_ref, vt_ref, lam_ref, g_ref, o_ref,
                 qs_scr, k_scr, m_scr, l_scr, acc_scr, *, t, lam_init):
    p = pl.program_id(2)
    qi = qi_ref[p]
    ki = ki_ref[p]
    qg = min(Q_GROUP, t)
    ng = t // qg

    @pl.when(ki == 0)
    def _():
        q = q_ref[...].astype(F32) * (DH_DIFF ** -0.5 * LOG2E)
        lane = lax.broadcasted_iota(jnp.int32, q.shape, 1)
        q_maps = (jnp.where(lane < DH_DIFF, q, 0.0).astype(BF16), jnp.where(lane >= DH_DIFF, q, 0.0).astype(BF16))
        for mp in range(2):
            for gi in range(ng):
                qs_scr[mp * ng + gi] = q_maps[mp][gi * qg:(gi + 1) * qg, :]
        m_scr[...] = jnp.full_like(m_scr, NEG_BIG)
        l_scr[...] = jnp.zeros_like(l_scr)
        acc_scr[...] = jnp.zeros_like(acc_scr)

    k_scr[...] = k_ref[...].astype(BF16)

    def group(g, n_keys, q_off):
        s = _dot_nt(k_scr[0:n_keys, :], qs_scr[g])
        if q_off is not None:
            key = lax.broadcasted_iota(jnp.int32, s.shape, 0)
            qpos = lax.broadcasted_iota(jnp.int32, s.shape, 1) + q_off
            s = jnp.where(key <= qpos, s, NEG_BIG)
        m_prev = m_scr[g]
        m_new = jnp.maximum(m_prev, jnp.max(s, axis=0, keepdims=True))
        alpha = jnp.exp2(m_prev - m_new)
        pr = jnp.exp2(s - m_new)
        l_scr[g] = alpha * l_scr[g] + jnp.sum(pr, axis=0, keepdims=True)
        acc_scr[g] = alpha * acc_scr[g] + _bdot(vt_ref[:, 0:n_keys], pr.astype(BF16))
        m_scr[g] = m_new

    @pl.when(ki < qi)
    def _():
        def body(g, carry):
            group(g, t, None)
            return carry
        lax.fori_loop(0, 2 * ng, body, 0)

    @pl.when(ki == qi)
    def _():
        for g in range(2 * ng):
            q_off = (g % ng) * qg
            group(g, q_off + qg, q_off)
        lam = _lambda_value(lam_ref, lam_init)
        g_col = g_ref[...]
        for c in range(t // LANES):
            g1, lanes = (c * LANES) // qg, slice((c * LANES) % qg, (c * LANES) % qg + LANES)
            a1 = acc_scr[g1][:, lanes] * (1.0 / l_scr[g1][:, lanes])
            a2 = acc_scr[ng + g1][:, lanes] * (1.0 / l_scr[ng + g1][:, lanes])
            o = a1 - lam * a2
            o = o * lax.rsqrt(jnp.mean(o * o, axis=0, keepdims=True) + NORM_EPS) * g_col * (1.0 - lam_init)
            o_ref[c * LANES:(c + 1) * LANES, :] = o.T.astype(BF16)


def _diff_prompt(qb, kb, vt, lam_vec, g_d, bsz, seq_len, lam_init):
    t = min(1024, seq_len)
    nq = seq_len // t
    pairs = [(i, j) for i in range(nq) for j in range(i + 1)]
    qi = jnp.asarray([a for a, _ in pairs], jnp.int32)
    ki = jnp.asarray([b for _, b in pairs], jnp.int32)
    g_col = g_d.reshape(DV_DIFF, 1)
    qg = min(Q_GROUP, t)
    n_grp = 2 * t // qg
    grid_spec = pltpu.PrefetchScalarGridSpec(
        num_scalar_prefetch=2,
        grid=(bsz, H_DIFF, len(pairs)),
        in_specs=[
            pl.BlockSpec((t, 2 * DH_DIFF), lambda b, h, p, qi, ki: (b * nq + qi[p], h)),
            pl.BlockSpec((t, 2 * DH_DIFF), lambda b, h, p, qi, ki: (b * nq + ki[p], h)),
            pl.BlockSpec((DV_DIFF, t), lambda b, h, p, qi, ki: (h, b * nq + ki[p])),
            pl.BlockSpec(lam_vec.shape, lambda b, h, p, qi, ki: (0, 0)),
            pl.BlockSpec(g_col.shape, lambda b, h, p, qi, ki: (0, 0)),
        ],
        out_specs=pl.BlockSpec((t, DV_DIFF), lambda b, h, p, qi, ki: (b * nq + qi[p], h)),
        scratch_shapes=[
            pltpu.VMEM((n_grp, qg, 2 * DH_DIFF), BF16),
            pltpu.VMEM((t, 2 * DH_DIFF), BF16),
            pltpu.VMEM((n_grp, 1, qg), F32),
            pltpu.VMEM((n_grp, 1, qg), F32),
            pltpu.VMEM((n_grp, DV_DIFF, qg), F32),
        ],
    )
    return pl.pallas_call(
        functools.partial(_diff_kernel, t=t, lam_init=lam_init),
        grid_spec=grid_spec,
        out_shape=jax.ShapeDtypeStruct((bsz * seq_len, H_DIFF * DV_DIFF), BF16),
        compiler_params=_params("parallel", "parallel", "arbitrary"),
        name="diff_prompt",
    )(qi, ki, qb, kb, vt, lam_vec, g_col)


PAGES_PER_STEP = 8
N_MAPS = 2 * H_DIFF


def _group_sums(prod):
    return jnp.concatenate(
        [jnp.sum(prod[g * DH_DIFF:(g + 1) * DH_DIFF, :], axis=0, keepdims=True) for g in range(N_MAPS)], axis=0)


def _diff_step_kernel(pt_ref, q_ref, kn_ref, vn_ref, lam_ref, g_ref, *rest, lam_init):
    pps = PAGES_PER_STEP
    k_refs = rest[:pps]
    v_refs = rest[pps:2 * pps]
    o_ref = rest[2 * pps]
    qb_scr, m_scr, l_scr, acc_scr = rest[2 * pps + 1:]
    j = pl.program_id(1)
    scale = DH_DIFF ** -0.5

    @pl.when(j == 0)
    def _():
        qb_scr[...] = jnp.broadcast_to(q_ref[...], (W_DIFF, PAGE_SIZE))
        m_scr[...] = jnp.full_like(m_scr, NEG_BIG)
        l_scr[...] = jnp.zeros_like(l_scr)
        acc_scr[...] = jnp.zeros_like(acc_scr)

    scores = [_group_sums(k_refs[pg][...] * qb_scr[...]) * scale for pg in range(pps)]
    s_max = scores[0]
    for pg in range(1, pps):
        s_max = jnp.maximum(s_max, scores[pg])
    m_prev = m_scr[...]
    m_new = jnp.maximum(m_prev, jnp.max(s_max, axis=-1, keepdims=True))
    alpha = jnp.exp(m_prev - m_new)
    row_head = lax.broadcasted_iota(jnp.int32, (N_MAPS, DV_DIFF), 0) // 2
    p_sum = jnp.zeros((N_MAPS, PAGE_SIZE), F32)
    pv = jnp.zeros((N_MAPS, DV_DIFF), F32)
    for pg in range(pps):
        pr = jnp.exp(scores[pg] - m_new)
        p_sum = p_sum + pr
        p_bits = lax.bitcast_convert_type(pr, jnp.uint32) & jnp.uint32(0xFFFF0000)
        p_top = lax.bitcast_convert_type(p_bits, F32)
        p_hi = p_top.astype(BF16)
        p_both = jnp.concatenate([p_top, pr - p_top], axis=0).astype(BF16)
        for hd in range(H_DIFF):
            v_hi, v_lo = _hi_lo(v_refs[pg][pl.ds(hd, PAGE_SIZE, stride=H_DIFF), :])
            r = _bdot(p_both, v_hi)
            r = r[0:N_MAPS, :] + r[N_MAPS:2 * N_MAPS, :] + _bdot(p_hi, v_lo)
            pv = pv + jnp.where(row_head == hd, r, 0.0)
    l_scr[...] = alpha * l_scr[...] + jnp.sum(p_sum, axis=-1, keepdims=True)
    acc_scr[...] = alpha * acc_scr[...] + pv
    m_scr[...] = m_new

    @pl.when(j == pl.num_programs(1) - 1)
    def _():
        s = _group_sums(q_ref[...] * kn_ref[...]) * scale
        m_old = m_scr[...]
        m_fin = jnp.maximum(m_old, s)
        a_fin = jnp.exp(m_old - m_fin)
        pr = jnp.exp(s - m_fin)
        l_fin = a_fin * l_scr[...] + pr
        a = (a_fin * acc_scr[...] + pr * vn_ref[...]) / l_fin
        lam = _lambda_value(lam_ref, lam_init)
        g = g_ref[...]
        for hd in range(H_DIFF):
            d = a[2 * hd:2 * hd + 1, :] - lam * a[2 * hd + 1:2 * hd + 2, :]
            o_ref[hd:hd + 1, :] = _rms(d) * g * (1.0 - lam_init)


def _diff_step(qb, kb_new, vb_new, cache_k, cache_v, layer_idx, page_table, lam_vec, g_d, lam_init):
    bsz, n_pages = page_table.shape
    n_layers, n_pool = cache_k.shape[0:2]
    pps = PAGES_PER_STEP
    assert n_pages % pps == 0
    ck = jnp.transpose(cache_k, (0, 1, 3, 4, 5, 2)).reshape(n_layers * n_pool, W_DIFF, PAGE_SIZE)
    cv = cache_v.reshape(n_layers * n_pool, PAGE_SIZE * H_DIFF, DV_DIFF)
    page_table = page_table + layer_idx * n_pool
    col_spec = pl.BlockSpec((None, W_DIFF, 1), lambda b, j, pt: (b, 0, 0))
    v_new = jnp.repeat(vb_new.reshape(bsz, H_DIFF, DV_DIFF), 2, axis=1)

    def page_spec(pg):
        return pl.BlockSpec((None, W_DIFF, PAGE_SIZE), lambda b, j, pt, pg=pg: (pt[b, j * pps + pg], 0, 0))

    grid_spec = pltpu.PrefetchScalarGridSpec(
        num_scalar_prefetch=1,
        grid=(bsz, n_pages // pps),
        in_specs=[col_spec, col_spec,
                  pl.BlockSpec((None, N_MAPS, DV_DIFF), lambda b, j, pt: (b, 0, 0)),
                  pl.BlockSpec(lam_vec.shape, lambda b, j, pt: (0, 0)),
                  pl.BlockSpec(g_d.shape, lambda b, j, pt: (0, 0))]
                 + [page_spec(pg) for pg in range(pps)] + [page_spec(pg) for pg in range(pps)],
        out_specs=pl.BlockSpec((None, H_DIFF, DV_DIFF), lambda b, j, pt: (b, 0, 0)),
        scratch_shapes=[
            pltpu.VMEM((W_DIFF, PAGE_SIZE), F32),
            pltpu.VMEM((N_MAPS, 1), F32),
            pltpu.VMEM((N_MAPS, 1), F32),
            pltpu.VMEM((N_MAPS, DV_DIFF), F32),
        ],
    )
    o = pl.pallas_call(
        functools.partial(_diff_step_kernel, lam_init=lam_init),
        grid_spec=grid_spec,
        out_shape=jax.ShapeDtypeStruct((bsz, H_DIFF, DV_DIFF), F32),
        compiler_params=_params("parallel", "arbitrary"),
        name="diff_step",
    )(page_table, qb.reshape(bsz, W_DIFF, 1), kb_new.reshape(bsz, W_DIFF, 1), v_new, lam_vec, g_d,
      *([ck] * pps), *([cv] * pps))
    return o.reshape(bsz, W_DIFF)


def _out_even_kernel(a_ref, b_ref, x_ref, gt_ref, *rest):
    w_refs = rest[:-1]
    o_ref = rest[-1]
    m = _mm(a_ref[...], w_refs, slice(0, W_RET)) + _mm(b_ref[...], w_refs, slice(W_RET, 2 * W_RET))
    o_ref[...] = x_ref[...] + gt_ref[...] * m


def _out_even(o_ret, o_diff, ws, x2d, gt, tm, tiles_per_seq):
    m = x2d.shape[0]
    row = lambda n: pl.BlockSpec((tm, n), lambda i: (i, 0))
    return pl.pallas_call(
        _out_even_kernel,
        grid=(m // tm,),
        in_specs=[row(W_RET), row(W_RET), row(D_MODEL), _vec_spec(gt, tm, tiles_per_seq)]
                 + [_full_spec(w) for w in ws],
        out_specs=row(D_MODEL),
        out_shape=jax.ShapeDtypeStruct((m, D_MODEL), F32),
        compiler_params=_params("parallel"),
        name="out_even",
    )(o_ret, o_diff, x2d, gt, *ws)


def _ffn_kernel(x_ref, g_ref, sh_ref, sc_ref, gt_ref, *rest, n_w):
    w1, w3, w2 = rest[0:n_w], rest[n_w:2 * n_w], rest[2 * n_w:3 * n_w]
    o_ref = rest[3 * n_w]
    x = x_ref[...]
    h = _modulate(x, g_ref[...], sh_ref[...], sc_ref[...])
    if n_w == 1:
        h = h.astype(BF16)
    u = jax.nn.silu(_mm(h, w1)) * _mm(h, w3)
    o_ref[...] = x + gt_ref[...] * _mm(u, w2)


def _ffn(x2d, g, sh, sc, gt, w1s, w3s, w2s, tm, tiles_per_seq):
    m = x2d.shape[0]
    row = pl.BlockSpec((tm, D_MODEL), lambda i: (i, 0))
    vs = lambda a: _vec_spec(a, tm, tiles_per_seq)
    ws = (*w1s, *w3s, *w2s)
    return pl.pallas_call(
        functools.partial(_ffn_kernel, n_w=len(w1s)),
        grid=(m // tm,),
        in_specs=[row, _full_spec(g), vs(sh), vs(sc), vs(gt)] + [_full_spec(w) for w in ws],
        out_specs=row,
        out_shape=jax.ShapeDtypeStruct((m, D_MODEL), F32),
        compiler_params=_params("parallel"),
        name="ffn",
    )(x2d, g, sh, sc, gt, *ws)


def _modmm_kernel(x_ref, g_ref, sh_ref, sc_ref, *rest):
    w_refs = rest[:-1]
    o_ref = rest[-1]
    o_ref[...] = _mm(_modulate(x_ref[...], g_ref[...], sh_ref[...], sc_ref[...]), w_refs)


def _modmm(x2d, g, sh, sc, ws, tm, tiles_per_seq):
    m = x2d.shape[0]
    n = ws[0].shape[1]
    vs = lambda a: _vec_spec(a, tm, tiles_per_seq)
    return pl.pallas_call(
        _modmm_kernel,
        grid=(m // tm,),
        in_specs=[pl.BlockSpec((tm, D_MODEL), lambda i: (i, 0)), _full_spec(g), vs(sh), vs(sc)]
                 + [_full_spec(w) for w in ws],
        out_specs=pl.BlockSpec((tm, n), lambda i: (i, 0)),
        out_shape=jax.ShapeDtypeStruct((m, n), F32),
        compiler_params=_params("parallel"),
        name="modmm",
    )(x2d, g, sh, sc, *ws)


def _mm_at(x, w_refs, idx):
    if len(w_refs) == 1:
        return _bdot(x.astype(BF16), w_refs[0][idx])
    return _split_dot(x, w_refs[0][idx], w_refs[1][idx])


def _s5_input_drive(u, bmat_refs, j):
    return _mm_at(u[:, j * LANES:(j + 1) * LANES], bmat_refs, j)


def _s5_output_block(h_re_of, h_im_of, cmat_refs, jb):
    acc = None
    for q in range(4):
        r = 4 * jb + q
        t = (_mm_at(h_re_of(r), cmat_refs, (r, slice(0, LANES), slice(None)))
             + _mm_at(h_im_of(r), cmat_refs, (r, slice(LANES, 2 * LANES), slice(None))))
        acc = t if acc is None else acc + t
    return acc


def _s5_prompt_kernel(u_ref, are_ref, aim_ref, bmat_ref, cmat_ref, d_ref, z_ref, hre_ref, him_ref,
                      sre, sim, st_re, st_im, *, tc):
    j = pl.program_id(1)

    @pl.when(j == 0)
    def _():
        st_re[...] = jnp.zeros_like(st_re)
        st_im[...] = jnp.zeros_like(st_im)

    u = u_ref[...]
    half = 4 * LANES
    for jb in range(8):
        res = _s5_input_drive(u, (bmat_ref,), jb)
        for q in range(4):
            r = 4 * jb + q
            sre[pl.ds(r, tc, stride=STEP_PITCH), :] = res[:, q * LANES:(q + 1) * LANES]
            sim[pl.ds(r, tc, stride=STEP_PITCH), :] = res[:, half + q * LANES:half + (q + 1) * LANES]

    a_re = are_ref[...]
    a_im = aim_ref[...]

    def step(t, carry):
        h_re, h_im = carry
        base = pl.multiple_of(t * STEP_PITCH, 8)
        n_re = a_re * h_re - a_im * h_im + sre[pl.ds(base, N_PAIRS), :]
        n_im = a_re * h_im + a_im * h_re + sim[pl.ds(base, N_PAIRS), :]
        sre[pl.ds(base, N_PAIRS), :] = n_re
        sim[pl.ds(base, N_PAIRS), :] = n_im
        return n_re, n_im

    h_re, h_im = lax.fori_loop(0, tc, step, (st_re[...], st_im[...]))
    st_re[...] = h_re
    st_im[...] = h_im

    for jb in range(8):
        y = _s5_output_block(lambda r: sre[pl.ds(r, tc, stride=STEP_PITCH), :],
                             lambda r: sim[pl.ds(r, tc, stride=STEP_PITCH), :], (cmat_ref,), jb)
        sl = slice(jb * LANES, (jb + 1) * LANES)
        y = y + d_ref[:, sl] * u[:, sl]
        z_ref[:, sl] = jax.nn.gelu(y, approximate=True).astype(BF16)

    @pl.when(j == pl.num_programs(1) - 1)
    def _():
        hre_ref[...] = h_re
        him_ref[...] = h_im


def _s5_prompt(u2d, bsz, seq_len, a_re, a_im, bmat, cmat, d_row):
    tc = min(256, seq_len)
    nj = seq_len // tc
    state_spec = pl.BlockSpec((None, N_PAIRS, LANES), lambda b, j: (b, 0, 0))
    return pl.pallas_call(
        functools.partial(_s5_prompt_kernel, tc=tc),
        grid=(bsz, nj),
        in_specs=[pl.BlockSpec((tc, D_MODEL), lambda b, j: (b * nj + j, 0)),
                  _full_spec(a_re), _full_spec(a_im), _full_spec(bmat), _full_spec(cmat), _full_spec(d_row)],
        out_specs=[pl.BlockSpec((tc, D_MODEL), lambda b, j: (b * nj + j, 0)), state_spec, state_spec],
        out_shape=[
            jax.ShapeDtypeStruct((bsz * seq_len, D_MODEL), BF16),
            jax.ShapeDtypeStruct((bsz, N_PAIRS, LANES), F32),
            jax.ShapeDtypeStruct((bsz, N_PAIRS, LANES), F32),
        ],
        scratch_shapes=[
            pltpu.VMEM((tc * STEP_PITCH, LANES), F32),
            pltpu.VMEM((tc * STEP_PITCH, LANES), F32),
            pltpu.VMEM((N_PAIRS, LANES), F32),
            pltpu.VMEM((N_PAIRS, LANES), F32),
        ],
        compiler_params=_params("parallel", "arbitrary"),
        name="s5_prompt",
    )(u2d, a_re, a_im, bmat, cmat, d_row)


def _s5_step_kernel(u_ref, h0re_ref, h0im_ref, are_ref, aim_ref, bhi_ref, blo_ref, chi_ref, clo_ref, d_ref,
                    z_ref, hre_ref, him_ref):
    u = u_ref[...]
    half = 4 * LANES
    bmat_refs = (bhi_ref, blo_ref)
    cmat_refs = (chi_ref, clo_ref)
    for jb in range(8):
        res = _s5_input_drive(u, bmat_refs, jb)
        sl = slice(jb * half, (jb + 1) * half)
        a_re = are_ref[:, sl]
        a_im = aim_ref[:, sl]
        h_re = h0re_ref[:, sl]
        h_im = h0im_ref[:, sl]
        hre_ref[:, sl] = a_re * h_re - a_im * h_im + res[:, 0:half]
        him_ref[:, sl] = a_re * h_im + a_im * h_re + res[:, half:2 * half]
    for jb in range(8):
        y = _s5_output_block(lambda r: hre_ref[:, r * LANES:(r + 1) * LANES],
                             lambda r: him_ref[:, r * LANES:(r + 1) * LANES], cmat_refs, jb)
        sl = slice(jb * LANES, (jb + 1) * LANES)
        y = y + d_ref[:, sl] * u[:, sl]
        z_ref[:, sl] = jax.nn.gelu(y, approximate=True)


def _s5_step(u2d, h0_re, h0_im, a_re_row, a_im_row, bmats, cmats, d_row):
    bsz = u2d.shape[0]
    n_state = N_GROUPS * SSM_STATE
    args = (u2d, h0_re, h0_im, a_re_row, a_im_row, *bmats, *cmats, d_row)
    return pl.pallas_call(
        _s5_step_kernel,
        grid=(1,),
        in_specs=[_full_spec(a) for a in args],
        out_specs=[pl.BlockSpec((bsz, D_MODEL), lambda i: (0, 0)),
                   pl.BlockSpec((bsz, n_state), lambda i: (0, 0)),
                   pl.BlockSpec((bsz, n_state), lambda i: (0, 0))],
        out_shape=[
            jax.ShapeDtypeStruct((bsz, D_MODEL), F32),
            jax.ShapeDtypeStruct((bsz, n_state), F32),
            jax.ShapeDtypeStruct((bsz, n_state), F32),
        ],
        compiler_params=_params("arbitrary"),
        name="s5_step",
    )(*args)


def _glu_router_kernel(z_ref, x_ref, gt_ref, g_ref, sh_ref, sc_ref, wrh_ref, wrl_ref, br_ref, *rest):
    wg_refs = rest[:-3]
    x_out, h_out, route_out = rest[-3:]
    vg = _mm(z_ref[...], wg_refs)
    val = vg[:, 0:D_MODEL]
    gate = vg[:, D_MODEL:2 * D_MODEL]
    x = x_ref[...] + gt_ref[...] * (val * jax.nn.sigmoid(gate))
    x_out[...] = x
    h = _modulate(x, g_ref[...], sh_ref[...], sc_ref[...])
    h_out[...] = h.astype(BF16)
    logits = _split_dot(h, wrh_ref[...], wrl_ref[...]) + br_ref[...]
    lane = lax.broadcasted_iota(jnp.int32, logits.shape, 1)
    v1 = jnp.max(logits, axis=-1, keepdims=True)
    i1 = jnp.min(jnp.where(logits == v1, lane, LANES), axis=-1, keepdims=True)
    others = jnp.where(lane == i1, NEG_BIG, logits)
    v2 = jnp.max(others, axis=-1, keepdims=True)
    i2 = jnp.min(jnp.where(others == v2, lane, LANES), axis=-1, keepdims=True)
    e = jnp.exp(v2 - v1)
    g1 = 1.0 / (1.0 + e)
    g2 = e / (1.0 + e)
    route = jnp.where(lane == i1, g1, 0.0) + jnp.where(lane == i2, g2, 0.0)
    flag = jnp.logical_or(lane == i1 + N_EXPERTS, lane == i2 + N_EXPERTS)
    route_out[...] = route + jnp.where(flag, 1.0, 0.0)


def _glu_router(z, wgs, x2d, gt, g, sh, sc, wr_hi, wr_lo, br, tm, tiles_per_seq):
    m = x2d.shape[0]
    row = lambda n: pl.BlockSpec((tm, n), lambda i: (i, 0))
    vs = lambda a: _vec_spec(a, tm, tiles_per_seq)
    return pl.pallas_call(
        _glu_router_kernel,
        grid=(m // tm,),
        in_specs=[row(D_MODEL), row(D_MODEL), vs(gt), _full_spec(g), vs(sh), vs(sc),
                  _full_spec(wr_hi), _full_spec(wr_lo), _full_spec(br)] + [_full_spec(w) for w in wgs],
        out_specs=[row(D_MODEL), row(D_MODEL), row(LANES)],
        out_shape=[
            jax.ShapeDtypeStruct((m, D_MODEL), F32),
            jax.ShapeDtypeStruct((m, D_MODEL), BF16),
            jax.ShapeDtypeStruct((m, LANES), F32),
        ],
        compiler_params=_params("parallel"),
        name="glu_router",
    )(z, x2d, gt, g, sh, sc, wr_hi, wr_lo, br, *wgs)


def _moe_kernel(te_ref, nv_ref, xs_ref, gate_ref, w1_ref, w3_ref, w2_ref, o_ref, acc_scr):
    i = pl.program_id(0)
    f = pl.program_id(1)
    nf = pl.num_programs(1)
    valid = i < nv_ref[0]

    @pl.when(valid)
    def _():
        x = xs_ref[...]
        a = _bdot(x, w1_ref[...].astype(BF16))
        b = _bdot(x, w3_ref[...].astype(BF16))
        u = (jax.nn.silu(a) * b).astype(BF16)
        contrib = _bdot(u, w2_ref[...].astype(BF16))

        @pl.when(f == 0)
        def _():
            acc_scr[...] = contrib

        @pl.when(f > 0)
        def _():
            acc_scr[...] = acc_scr[...] + contrib

        @pl.when(f == nf - 1)
        def _():
            o_ref[...] = acc_scr[...] * gate_ref[...]

    @pl.when(jnp.logical_and(jnp.logical_not(valid), f == nf - 1))
    def _():
        o_ref[...] = jnp.zeros_like(o_ref)


def _moe_experts(xs, row_gate, tile_expert, n_valid, w1, w3, w2, tm):
    rows = xs.shape[0]
    n_tiles = rows // tm
    tf = 512
    nf = D_FF_EXPERT // tf

    def fsel(i, f, nv):
        return jnp.where(i < nv[0], f, nf - 1)

    grid_spec = pltpu.PrefetchScalarGridSpec(
        num_scalar_prefetch=2,
        grid=(n_tiles, nf),
        in_specs=[
            pl.BlockSpec((tm, D_MODEL), lambda i, f, te, nv: (i, 0)),
            pl.BlockSpec((tm, 1), lambda i, f, te, nv: (i, 0)),
            pl.BlockSpec((None, D_MODEL, tf), lambda i, f, te, nv: (te[i], 0, fsel(i, f, nv))),
            pl.BlockSpec((None, D_MODEL, tf), lambda i, f, te, nv: (te[i], 0, fsel(i, f, nv))),
            pl.BlockSpec((None, tf, D_MODEL), lambda i, f, te, nv: (te[i], fsel(i, f, nv), 0)),
        ],
        out_specs=pl.BlockSpec((tm, D_MODEL), lambda i, f, te, nv: (i, 0)),
        scratch_shapes=[pltpu.VMEM((tm, D_MODEL), F32)],
    )
    return pl.pallas_call(
        _moe_kernel,
        grid_spec=grid_spec,
        out_shape=jax.ShapeDtypeStruct((rows, D_MODEL), F32),
        compiler_params=_params("parallel", "arbitrary"),
        name="moe_experts",
    )(tile_expert, n_valid, xs, row_gate, w1, w3, w2)


def _moe_residual_kernel(x_ref, gt_ref, a_ref, b_ref, o_ref):
    o_ref[...] = x_ref[...] + gt_ref[...] * (a_ref[...] + b_ref[...])


def _moe_residual(x2d, gt, ya, yb, tm, tiles_per_seq):
    m = x2d.shape[0]
    row = pl.BlockSpec((tm, D_MODEL), lambda i: (i, 0))
    return pl.pallas_call(
        _moe_residual_kernel,
        grid=(m // tm,),
        in_specs=[row, _vec_spec(gt, tm, tiles_per_seq), row, row],
        out_specs=row,
        out_shape=jax.ShapeDtypeStruct((m, D_MODEL), F32),
        compiler_params=_params("parallel"),
        name="moe_residual",
    )(x2d, gt, ya, yb)


def _moe(h_bf, route, x2d, gt, w1, w3, w2, tm_rows, tm, tiles_per_seq):
    m = h_bf.shape[0]
    comb = route[:, 0:N_EXPERTS]
    sel = route[:, N_EXPERTS:2 * N_EXPERTS] > 0.5
    seli = sel.astype(jnp.int32)
    counts = jnp.sum(seli, axis=0)
    padded = ((counts + tm_rows - 1) // tm_rows) * tm_rows
    ends = jnp.cumsum(padded)
    offsets = ends - padded
    rank = jnp.cumsum(seli, axis=0) - 1
    pos = offsets[None, :] + rank
    n_rows = ((2 * m + N_EXPERTS * (tm_rows - 1)) // tm_rows + 1) * tm_rows
    n_tiles = n_rows // tm_rows
    flat_pos = jnp.where(sel, pos, n_rows).reshape(-1)
    tok = jnp.broadcast_to(jnp.arange(m, dtype=jnp.int32)[:, None], (m, N_EXPERTS)).reshape(-1)
    row_token = jnp.zeros((n_rows,), jnp.int32).at[flat_pos].set(tok, mode="drop")
    row_gate = jnp.zeros((n_rows,), F32).at[flat_pos].set(comb.reshape(-1), mode="drop")
    tile_start = jnp.arange(n_tiles, dtype=jnp.int32) * tm_rows
    n_valid = (ends[-1] // tm_rows).astype(jnp.int32)
    first_row = jnp.minimum(tile_start, ends[-1] - 1)
    tile_expert = jnp.sum((ends[None, :] <= first_row[:, None]).astype(jnp.int32), axis=1)
    xs = jnp.take(h_bf, row_token, axis=0)
    ys = _moe_experts(xs, row_gate.reshape(n_rows, 1), tile_expert, n_valid.reshape(1), w1, w3, w2, tm_rows)
    masked = jnp.where(sel, pos, n_rows)
    slot_a = jnp.min(masked, axis=1)
    slot_b = jnp.min(jnp.where(masked == slot_a[:, None], n_rows, masked), axis=1)
    ya = jnp.take(ys, slot_a, axis=0)
    yb = jnp.take(ys, slot_b, axis=0)
    return _moe_residual(x2d, gt, ya, yb, tm, tiles_per_seq)


def _retention_constants(chunk):
    log_g = jnp.log1p(-jnp.exp2(-5.0 - jnp.arange(H_RET, dtype=F32)))
    idx = jnp.arange(chunk, dtype=F32)
    rel = idx[:, None] - idx[None, :]
    dm = jnp.where(rel >= 0, jnp.exp(log_g[:, None, None] * jnp.maximum(rel, 0.0)), 0.0)
    q_decay = jnp.exp(log_g[:, None] * (idx[None, :] + 1.0))
    k_decay = jnp.exp(log_g[:, None] * (chunk - 1.0 - idx[None, :]))
    chunk_decay = jnp.exp(log_g * chunk)
    qd = jnp.broadcast_to(q_decay[:, :, None], (H_RET, chunk, DK_RET))
    kd = jnp.broadcast_to(k_decay[:, :, None], (H_RET, chunk, DK_RET))
    cd = jnp.broadcast_to(chunk_decay[:, None, None], (H_RET, 1, DV_RET))
    return dm, qd, kd, cd


def _rotary_tables(pos):
    half = DK_RET // 2
    inv = np.float64(ROPE_BASE) ** (-np.arange(half, dtype=np.float64) / half)
    inv_hi = inv.astype(np.float32)
    inv_lo = (inv - inv_hi.astype(np.float64)).astype(np.float32)
    p = jnp.asarray(pos).astype(F32)[:, None]
    ang = p * jnp.asarray(inv_hi)[None, :] + p * jnp.asarray(inv_lo)[None, :]
    cos = jnp.cos(ang)
    sin = jnp.sin(ang)
    return jnp.concatenate([cos, cos], axis=-1), jnp.concatenate([-sin, sin], axis=-1)


def _s5_matrices(lam_re, lam_im, log_step, b_re, b_im, c_re, c_im, d_skip):
    lr = lam_re.astype(F32)
    li = lam_im.astype(F32)
    step = jnp.exp(log_step.astype(F32))[:, None]
    mag = jnp.exp(lr * step)
    ab_re = mag * jnp.cos(li * step)
    ab_im = mag * jnp.sin(li * step)
    nr = ab_re - 1.0
    ni = ab_im
    den = lr * lr + li * li
    kr = (nr * lr + ni * li) / den
    ki = (ni * lr - nr * li) / den
    br = b_re.astype(F32)
    bi = b_im.astype(F32)
    bb_re = kr[..., None] * br - ki[..., None] * bi
    bb_im = kr[..., None] * bi + ki[..., None] * br
    eye8 = jnp.eye(8, dtype=F32)

    def drive(bb):
        t = bb.reshape(8, 8, SSM_STATE, SSM_GROUP).transpose(0, 1, 3, 2)
        t = t[:, :, :, None, :] * eye8[None, :, None, :, None]
        return t.reshape(8, LANES, 8 * SSM_STATE)

    bmat = jnp.concatenate([drive(bb_re), drive(bb_im)], axis=-1)
    r_idx = jnp.arange(N_PAIRS)[:, None, None]
    gl_idx = jnp.arange(2)[None, :, None]
    gb_idx = jnp.arange(8)[None, None, :]
    onehot = (gb_idx == 2 * (r_idx % 4) + gl_idx).astype(F32)

    def readout(c):
        t = c.astype(F32).reshape(N_PAIRS, 2, SSM_GROUP, SSM_STATE).transpose(0, 1, 3, 2)
        t = t[:, :, :, None, :] * onehot[:, :, None, :, None]
        return t.reshape(N_PAIRS, LANES, LANES)

    cmat = jnp.concatenate([readout(c_re), -readout(c_im)], axis=1)
    return (ab_re.reshape(N_PAIRS, LANES), ab_im.reshape(N_PAIRS, LANES), bmat, cmat,
            d_skip.astype(F32).reshape(1, D_MODEL))


def _trunk(x, mods, pos, ret0, ssm0, paged, p):
    bsz, seq_len, _ = x.shape
    m = bsz * seq_len
    prompt = seq_len > 1
    depth = p["w_ada"].shape[0]
    x2d = x.reshape(m, D_MODEL)
    if prompt:
        tm = min(512, seq_len)
        tps = seq_len // tm
        tm_ffn = min(256, seq_len)
        tps_ffn = seq_len // tm_ffn
        vec = lambda v: v.reshape(bsz, 1, D_MODEL)
    else:
        tm = tm_ffn = m
        tps = tps_ffn = 1
        vec = lambda v: v.reshape(1, bsz, D_MODEL)
    row = lambda v: v.reshape(1, -1).astype(F32)
    wsp = lambda w: _wsplit(w, not prompt)
    rets, ks_new, vs_new, srs, sis = [], [], [], [], []
    for layer in range(depth):
        sh1, sc1, gt1, sh2, sc2, gt2 = [vec(v) for v in jnp.split(mods[layer], N_ADA, axis=-1)]
        i = layer // 2
        if layer % 2 == 0:
            lam_init = 0.8 - 0.6 * math.exp(-0.3 * layer)
            cos_t, sin_t = _rotary_tables(pos)
            seg = (jnp.arange(W_DIFF)[:, None] // DH_DIFF == jnp.arange(W_DIFF)[None, :] // DH_DIFF).astype(BF16)
            gq = jnp.tile(row(p["g_qnorm"][i]), (1, W_DIFF // DH_DIFF))
            gk = jnp.tile(row(p["g_knorm"][i]), (1, W_DIFF // DH_DIFF))
            ret_in, qb, kb, vb, *vt = _in_even(
                x2d, row(p["g_norm1"][layer]), sh1, sc1, cos_t, sin_t, wsp(p["w_in_mix"][i]),
                gq, gk, seg, tm, tps, BF16 if prompt else F32, prompt)
            lam_vec = jnp.stack([p["lam_q1"][i], p["lam_k1"][i], p["lam_q2"][i], p["lam_k2"][i]]).astype(F32)
            g_ret = row(p["g_ret"][i])
            g_d = row(p["g_dnorm"][i])
            if prompt:
                dm, qd, kd, cd = _retention_constants(math.gcd(seq_len, RET_CHUNK))
                o_ret, s_new = _retention_prompt(ret_in, bsz, seq_len, dm, qd, kd, cd, g_ret)
                o_diff = _diff_prompt(qb, kb, vt[0], lam_vec, g_d, bsz, seq_len, lam_init)
            else:
                _, qd, _, cd = _retention_constants(1)
                o_ret, s_new = _retention_step(ret_in, ret0[i].astype(F32), qd, cd, g_ret)
                o_diff = _diff_step(qb, kb, vb, paged["cache_k"], paged["cache_v"], i,
                                    paged["page_table"], lam_vec, g_d, lam_init)
            rets.append(s_new)
            ks_new.append(kb.reshape(bsz, seq_len, H_DIFF, 2, DH_DIFF))
            vs_new.append(vb.reshape(bsz, seq_len, H_DIFF, DV_DIFF))
            x2d = _out_even(o_ret, o_diff, wsp(p["w_out_mix"][i]), x2d, gt1, tm, tps)
            x2d = _ffn(x2d, row(p["g_norm2"][layer]), sh2, sc2, gt2, wsp(p["w_ffn1"][i]),
                       wsp(p["w_ffn3"][i]), wsp(p["w_ffn2"][i]), tm_ffn, tps_ffn)
        else:
            a_re, a_im, bmat, cmat, d_row = _s5_matrices(
                p["lam_re"][i], p["lam_im"][i], p["log_step"][i], p["b_re"][i], p["b_im"][i],
                p["c_re"][i], p["c_im"][i], p["d_skip"][i])
            u2d = _modmm(x2d, row(p["g_norm1"][layer]), sh1, sc1, wsp(p["w_ssm_in"][i]), tm, tps)
            n_state = N_GROUPS * SSM_STATE
            if prompt:
                z, h_re, h_im = _s5_prompt(u2d, bsz, seq_len, a_re, a_im, bmat.astype(BF16), cmat.astype(BF16),
                                           d_row)
            else:
                z, h_re, h_im = _s5_step(
                    u2d, ssm0[0][i].astype(F32).reshape(bsz, n_state), ssm0[1][i].astype(F32).reshape(bsz, n_state),
                    a_re.reshape(1, n_state), a_im.reshape(1, n_state), wsp(bmat), wsp(cmat), d_row)
            srs.append(h_re.reshape(bsz, N_GROUPS, SSM_STATE))
            sis.append(h_im.reshape(bsz, N_GROUPS, SSM_STATE))
            wr = jnp.zeros((D_MODEL, LANES), F32).at[:, 0:N_EXPERTS].set(p["w_router"][i].astype(F32))
            wr_hi, wr_lo = _wsplit(wr, True)
            br = jnp.full((1, LANES), NEG_BIG, F32).at[0, 0:N_EXPERTS].set(p["b_router"][i].astype(F32))
            x2d, h_bf, route = _glu_router(
                z, wsp(p["w_glu"][i]), x2d, gt1, row(p["g_norm2"][layer]), sh2, sc2,
                wr_hi, wr_lo, br, tm_ffn, tps_ffn)
            x2d = _moe(h_bf, route, x2d, gt2, p["w_moe1"][i], p["w_moe3"][i], p["w_moe2"][i],
                       512 if prompt else 32, tm, tps)
    return (x2d.reshape(bsz, seq_len, D_MODEL), jnp.stack(rets), jnp.stack(ks_new), jnp.stack(vs_new),
            jnp.stack(srs), jnp.stack(sis))


def kernel(x_prompt, x_sample, state_ret, cache_k, cache_v, state_ssm_re, state_ssm_im, page_table, c_prompt, c_sample, w_ada, b_ada, g_norm1, g_norm2, w_in_mix, w_out_mix, g_ret, g_qnorm, g_knorm, g_dnorm, lam_q1, lam_k1, lam_q2, lam_k2, w_ffn1, w_ffn3, w_ffn2, w_ssm_in, lam_re, lam_im, log_step, b_re, b_im, c_re, c_im, d_skip, w_glu, w_router, b_router, w_moe1, w_moe3, w_moe2):
    p = dict(w_ada=w_ada, g_norm1=g_norm1, g_norm2=g_norm2, w_in_mix=w_in_mix, w_out_mix=w_out_mix,
             g_ret=g_ret, g_qnorm=g_qnorm, g_knorm=g_knorm, g_dnorm=g_dnorm, lam_q1=lam_q1, lam_k1=lam_k1,
             lam_q2=lam_q2, lam_k2=lam_k2, w_ffn1=w_ffn1, w_ffn3=w_ffn3, w_ffn2=w_ffn2, w_ssm_in=w_ssm_in,
             lam_re=lam_re, lam_im=lam_im, log_step=log_step, b_re=b_re, b_im=b_im, c_re=c_re, c_im=c_im,
             d_skip=d_skip, w_glu=w_glu, w_router=w_router, b_router=b_router,
             w_moe1=w_moe1, w_moe3=w_moe3, w_moe2=w_moe2)
    n_prompt = c_prompt.shape[0]
    n_sample = c_sample.shape[0]
    n_all = n_prompt + n_sample
    rows = ((n_all + 7) // 8) * 8
    c_all = jnp.zeros((rows, D_MODEL), F32).at[0:n_prompt].set(c_prompt).at[n_prompt:n_all].set(c_sample)
    mods = _ada(c_all, w_ada, b_ada)
    seq_len = x_prompt.shape[1]
    past_len = page_table.shape[1] * PAGE_SIZE
    y_p, ret_p, k_p, v_p, sr_p, si_p = _trunk(
        x_prompt, mods[:, 0:n_prompt], np.arange(seq_len), None, None, None, p)
    paged = dict(cache_k=cache_k, cache_v=cache_v, page_table=page_table)
    y_s, ret_s, k_s, v_s, sr_s, si_s = _trunk(
        x_sample, mods[:, n_prompt:n_all], past_len + np.arange(x_sample.shape[1]),
        state_ret, (state_ssm_re, state_ssm_im), paged, p)
    return (y_p, y_s, ret_p, ret_s, k_p, v_p, k_s, v_s, sr_p, si_p, sr_s, si_s)
```

```python
import functools
import math

import jax
import jax.numpy as jnp
import numpy as np
from jax import lax
from jax.experimental import pallas as pl
from jax.experimental.pallas import tpu as pltpu

F32 = jnp.float32
BF16 = jnp.bfloat16

D_MODEL = 1024
H_RET = 4
DK_RET = 128
DV_RET = 128
RET_CHUNK = 128
ROPE_BASE = 10000.0
H_DIFF = 4
DH_DIFF = 64
DV_DIFF = 128
PAGE_SIZE = 128
W_RET = H_RET * DK_RET
W_DIFF = H_DIFF * 2 * DH_DIFF
IN_EVEN = 4 * W_RET + 3 * W_DIFF
SSM_GROUP = 16
N_GROUPS = D_MODEL // SSM_GROUP
SSM_STATE = 64
N_PAIRS = N_GROUPS // 2
STEP_PITCH = N_PAIRS + 8
D_FF = 2816
N_EXPERTS = 8
D_FF_EXPERT = 3584
N_ADA = 6
NORM_EPS = 1e-6
LANES = 128
NEG_BIG = -1e30

VMEM_LIMIT_BYTES = 56 * 1024 * 1024


def _params(*sem):
    return pltpu.CompilerParams(dimension_semantics=sem, vmem_limit_bytes=VMEM_LIMIT_BYTES)


def _bdot(a, b):
    return jnp.dot(a, b, preferred_element_type=F32)


def _dot_nt(a, b):
    return lax.dot_general(a, b, (((1,), (1,)), ((), ())), preferred_element_type=F32)


def _dot_tn(a, b):
    return lax.dot_general(a, b, (((0,), (0,)), ((), ())), preferred_element_type=F32)


def _rms(x):
    return x * lax.rsqrt(jnp.mean(x * x, axis=-1, keepdims=True) + NORM_EPS)


def _modulate(x, g, shift, scale):
    return (_rms(x) * g) * (1.0 + scale) + shift


def _hi_lo(x):
    bits = lax.bitcast_convert_type(x, jnp.uint32) & jnp.uint32(0xFFFF0000)
    hi = lax.bitcast_convert_type(bits, F32)
    return hi.astype(BF16), (x - hi).astype(BF16)


def _split_dot(x, w_hi, w_lo):
    x_hi, x_lo = _hi_lo(x)
    return _bdot(x_hi, w_hi) + _bdot(x_lo, w_hi) + _bdot(x_hi, w_lo)


def _mm(x, w_refs, rows=slice(None)):
    if len(w_refs) == 1:
        return _bdot(x.astype(BF16), w_refs[0][rows, :])
    return _split_dot(x.astype(F32), w_refs[0][rows, :], w_refs[1][rows, :])


def _wsplit(w, precise):
    w = w.astype(F32)
    if not precise:
        return (w.astype(BF16),)
    return _hi_lo(w)


def _ada_kernel(c_ref, w_ref, b_ref, o_ref):
    w_hi, w_lo = _hi_lo(w_ref[...])
    o_ref[...] = _split_dot(jax.nn.silu(c_ref[...]), w_hi, w_lo) + b_ref[...]


def _ada(c_all, w_ada, b_ada):
    depth, _, n = w_ada.shape
    rows = c_all.shape[0]
    tn = 1536
    return pl.pallas_call(
        _ada_kernel,
        grid=(depth, n // tn),
        in_specs=[
            pl.BlockSpec((rows, D_MODEL), lambda l, j: (0, 0)),
            pl.BlockSpec((None, D_MODEL, tn), lambda l, j: (l, 0, j)),
            pl.BlockSpec((None, 1, tn), lambda l, j: (l, 0, j)),
        ],
        out_specs=pl.BlockSpec((None, rows, tn), lambda l, j: (l, 0, j)),
        out_shape=jax.ShapeDtypeStruct((depth, rows, n), F32),
        compiler_params=_params("parallel", "parallel"),
        name="ada",
    )(c_all, w_ada, b_ada.reshape(depth, 1, n))


def _vec_spec(arr, tm, tiles_per_seq):
    if arr.shape[1] == 1:
        return pl.BlockSpec((None, 1, arr.shape[2]), lambda i, *_: (i // tiles_per_seq, 0, 0))
    return pl.BlockSpec((None, tm, arr.shape[2]), lambda i, *_: (0, 0, 0))


def _pos_spec(arr, tm, tiles_per_seq):
    if arr.shape[0] == 1:
        return pl.BlockSpec((1, arr.shape[1]), lambda i, *_: (0, 0))
    return pl.BlockSpec((tm, arr.shape[1]), lambda i, *_: (i % tiles_per_seq, 0))


def _full_spec(arr):
    nd = arr.ndim
    return pl.BlockSpec(arr.shape, lambda *_: (0,) * nd)


def _in_even_kernel(x_ref, g_ref, sh_ref, sc_ref, cos_ref, sin_ref, gq_ref, gk_ref,
                    seg_ref, *rest, n_w):
    w_refs = rest[:n_w]
    ret_ref, qb_ref, kb_ref, vb_ref = rest[n_w:n_w + 4]
    vt_ref = rest[n_w + 4] if len(rest) > n_w + 4 else None
    y = _mm(_modulate(x_ref[...], g_ref[...], sh_ref[...], sc_ref[...]), w_refs)
    cos = cos_ref[...]
    sin = sin_ref[...]
    for part in range(2):
        for hd in range(H_RET):
            lo = part * W_RET + hd * DK_RET
            blk = y[:, lo:lo + DK_RET]
            rot = blk * cos + pltpu.roll(blk, DK_RET // 2, axis=1) * sin
            if part == 1:
                rot = rot * (DK_RET ** -0.5)
            ret_ref[:, lo:lo + DK_RET] = rot
    ret_ref[:, 2 * W_RET:4 * W_RET] = y[:, 2 * W_RET:4 * W_RET]
    seg = seg_ref[...]
    base = 4 * W_RET

    def qk_norm(blk, g):
        sq_hi, sq_lo = _hi_lo(blk * blk)
        ms = (_bdot(sq_hi, seg) + _bdot(sq_lo, seg)) * (1.0 / DH_DIFF)
        return blk * lax.rsqrt(ms + NORM_EPS) * g

    qb_ref[...] = qk_norm(y[:, base:base + W_DIFF], gq_ref[...]).astype(qb_ref.dtype)
    kb_ref[...] = qk_norm(y[:, base + W_DIFF:base + 2 * W_DIFF], gk_ref[...])
    v = y[:, base + 2 * W_DIFF:base + 3 * W_DIFF]
    vb_ref[...] = v
    if vt_ref is not None:
        vt_ref[...] = v.T.astype(BF16)


def _in_even(x2d, g, sh, sc, cos_t, sin_t, ws, gq, gk, seg, tm, tiles_per_seq, q_dtype, emit_vt):
    m = x2d.shape[0]
    row = lambda n: pl.BlockSpec((tm, n), lambda i: (i, 0))
    out_specs = [row(4 * W_RET), row(W_DIFF), row(W_DIFF), row(W_DIFF)]
    out_shape = [
        jax.ShapeDtypeStruct((m, 4 * W_RET), F32),
        jax.ShapeDtypeStruct((m, W_DIFF), q_dtype),
        jax.ShapeDtypeStruct((m, W_DIFF), F32),
        jax.ShapeDtypeStruct((m, W_DIFF), F32),
    ]
    if emit_vt:
        out_specs.append(pl.BlockSpec((W_DIFF, tm), lambda i: (0, i)))
        out_shape.append(jax.ShapeDtypeStruct((W_DIFF, m), BF16))
    return pl.pallas_call(
        functools.partial(_in_even_kernel, n_w=len(ws)),
        grid=(m // tm,),
        in_specs=[
            row(D_MODEL), _full_spec(g), _vec_spec(sh, tm, tiles_per_seq), _vec_spec(sc, tm, tiles_per_seq),
            _pos_spec(cos_t, tm, tiles_per_seq), _pos_spec(sin_t, tm, tiles_per_seq),
            _full_spec(gq), _full_spec(gk), _full_spec(seg),
        ] + [_full_spec(w) for w in ws],
        out_specs=out_specs,
        out_shape=out_shape,
        compiler_params=_params("parallel"),
        name="in_even",
    )(x2d, g, sh, sc, cos_t, sin_t, gq, gk, seg, *ws)


def _ret_kernel(q_ref, k_ref, v_ref, ga_ref, dm_ref, qd_ref, kd_ref, cd_ref, g_ref,
                o_ref, sfin_ref, s_scr, *, n_chunks):
    j = pl.program_id(1)

    @pl.when(j == 0)
    def _():
        s_scr[...] = jnp.zeros_like(s_scr)

    g = g_ref[...]
    for hd in range(H_RET):
        sl = slice(hd * DK_RET, (hd + 1) * DK_RET)
        dm = dm_ref[hd]
        qd = qd_ref[hd]
        kd = kd_ref[hd]
        cd = cd_ref[hd]

        def body(c, carry, sl=sl, dm=dm, qd=qd, kd=kd, cd=cd, hd=hd):
            r = pl.multiple_of(c * RET_CHUNK, RET_CHUNK)
            q = q_ref[pl.ds(r, RET_CHUNK), sl]
            k = k_ref[pl.ds(r, RET_CHUNK), sl]
            v = v_ref[pl.ds(r, RET_CHUNK), sl]
            vb = v.astype(BF16)
            s = s_scr[hd]
            att = _dot_nt(q.astype(BF16), k.astype(BF16)) * dm
            o = _bdot(att.astype(BF16), vb) + _bdot((q * qd).astype(BF16), s.astype(BF16))
            s_scr[hd] = cd * s + _dot_tn((k * kd).astype(BF16), vb)
            on = _rms(o) * g
            o_ref[pl.ds(r, RET_CHUNK), sl] = (on * jax.nn.silu(ga_ref[pl.ds(r, RET_CHUNK), sl])).astype(BF16)
            return carry

        lax.fori_loop(0, n_chunks, body, 0)

    @pl.when(j == pl.num_programs(1) - 1)
    def _():
        sfin_ref[...] = s_scr[...]


def _retention_prompt(ret_in, bsz, seq_len, dm, qd, kd, cd, g_ret):
    n_chunks = min(8, seq_len // RET_CHUNK)
    rows = n_chunks * RET_CHUNK
    nj = seq_len // rows
    col = lambda c: pl.BlockSpec((rows, W_RET), lambda b, j: (b * nj + j, c))
    return pl.pallas_call(
        functools.partial(_ret_kernel, n_chunks=n_chunks),
        grid=(bsz, nj),
        in_specs=[col(0), col(1), col(2), col(3), _full_spec(dm), _full_spec(qd), _full_spec(kd),
                  _full_spec(cd), _full_spec(g_ret)],
        out_specs=[
            pl.BlockSpec((rows, W_RET), lambda b, j: (b * nj + j, 0)),
            pl.BlockSpec((None, H_RET, DK_RET, DV_RET), lambda b, j: (b, 0, 0, 0)),
        ],
        out_shape=[
            jax.ShapeDtypeStruct((bsz * seq_len, W_RET), BF16),
            jax.ShapeDtypeStruct((bsz, H_RET, DK_RET, DV_RET), F32),
        ],
        scratch_shapes=[pltpu.VMEM((H_RET, DK_RET, DV_RET), F32)],
        compiler_params=_params("parallel", "arbitrary"),
        name="retention_prompt",
    )(ret_in, ret_in, ret_in, ret_in, dm, qd, kd, cd, g_ret)


def _ret_step_kernel(ret_ref, s_ref, qd_ref, cd_ref, g_ref, o_ref, snew_ref):
    g = g_ref[...]
    rows = lax.broadcasted_iota(jnp.int32, (DK_RET, DK_RET), 0)
    cols = lax.broadcasted_iota(jnp.int32, (DK_RET, DK_RET), 1)
    diag = rows == cols
    for hd in range(H_RET):
        lo = hd * DK_RET
        q = ret_ref[:, lo:lo + DK_RET]
        k = ret_ref[:, W_RET + lo:W_RET + lo + DK_RET]
        v = ret_ref[:, 2 * W_RET + lo:2 * W_RET + lo + DK_RET]
        ga = ret_ref[:, 3 * W_RET + lo:3 * W_RET + lo + DK_RET]
        s = s_ref[hd]
        s_hi, s_lo = _hi_lo(s)
        att = jnp.sum(q * k, axis=-1, keepdims=True)
        q8 = jnp.broadcast_to(q * qd_ref[hd], (8, DK_RET))
        o = att * v + _split_dot(q8, s_hi, s_lo)[0:1, :]
        k_diag = jnp.where(diag, jnp.broadcast_to(k, (DK_RET, DK_RET)), 0.0)
        v_rows = jnp.broadcast_to(v, (DK_RET, DV_RET))
        v_hi, v_lo = _hi_lo(v_rows)
        snew_ref[hd] = cd_ref[hd] * s + _split_dot(k_diag, v_hi, v_lo)
        on = _rms(o) * g
        o_ref[:, lo:lo + DK_RET] = on * jax.nn.silu(ga)


def _retention_step(ret_in, state, qd1, cd, g_ret):
    bsz = state.shape[0]
    ret3 = ret_in.reshape(bsz, 1, 4 * W_RET)
    o, s_new = pl.pallas_call(
        _ret_step_kernel,
        grid=(bsz,),
        in_specs=[
            pl.BlockSpec((None, 1, 4 * W_RET), lambda b: (b, 0, 0)),
            pl.BlockSpec((None, H_RET, DK_RET, DV_RET), lambda b: (b, 0, 0, 0)),
            _full_spec(qd1), _full_spec(cd), _full_spec(g_ret),
        ],
        out_specs=[
            pl.BlockSpec((None, 1, W_RET), lambda b: (b, 0, 0)),
            pl.BlockSpec((None, H_RET, DK_RET, DV_RET), lambda b: (b, 0, 0, 0)),
        ],
        out_shape=[
            jax.ShapeDtypeStruct((bsz, 1, W_RET), F32),
            jax.ShapeDtypeStruct((bsz, H_RET, DK_RET, DV_RET), F32),
        ],
        compiler_params=_params("parallel"),
        name="retention_step",
    )(ret3, state, qd1, cd, g_ret)
    return o.reshape(bsz, W_RET), s_new


def _lambda_value(lam_ref, lam_init):
    lv = lam_ref[...]
    s1 = jnp.sum(lv[0:1, :] * lv[1:2, :], axis=-1, keepdims=True)
    s2 = jnp.sum(lv[2:3, :] * lv[3:4, :], axis=-1, keepdims=True)
    return jnp.exp(s1) - jnp.exp(s2) + lam_init


Q_GROUP = 512
LOG2E = 1.4426950408889634


def _diff_kernel(qi_ref, ki_ref, q_ref, k_ref, vt_ref, lam_ref, g_ref, o_ref,
                 qs_scr, k_scr, m_scr, l_scr, acc_scr, *, t, lam_init):
    p = pl.program_id(2)
    qi = qi_ref[p]
    ki = ki_ref[p]
    qg = min(Q_GROUP, t)
    ng = t // qg

    @pl.when(ki == 0)
    def _():
        q = q_ref[...].astype(F32) * (DH_DIFF ** -0.5 * LOG2E)
        lane = lax.broadcasted_iota(jnp.int32, q.shape, 1)
        q_maps = (jnp.where(lane < DH_DIFF, q, 0.0).astype(BF16), jnp.where(lane >= DH_DIFF, q, 0.0).astype(BF16))
        for mp in range(2):
            for gi in range(ng):
                qs_scr[mp * ng + gi] = q_maps[mp][gi * qg:(gi + 1) * qg, :]
        m_scr[...] = jnp.full_like(m_scr, NEG_BIG)
        l_scr[...] = jnp.zeros_like(l_scr)
        acc_scr[...] = jnp.zeros_like(acc_scr)

    k_scr[...] = k_ref[...].astype(BF16)

    def group(g, n_keys, q_off):
        s = _dot_nt(k_scr[0:n_keys, :], qs_scr[g])
        if q_off is not None:
            key = lax.broadcasted_iota(jnp.int32, s.shape, 0)
            qpos = lax.broadcasted_iota(jnp.int32, s.shape, 1) + q_off
            s = jnp.where(key <= qpos, s, NEG_BIG)
        m_prev = m_scr[g]
        m_new = jnp.maximum(m_prev, jnp.max(s, axis=0, keepdims=True))
        alpha = jnp.exp2(m_prev - m_new)
        pr = jnp.exp2(s - m_new)
        l_scr[g] = alpha * l_scr[g] + jnp.sum(pr, axis=0, keepdims=True)
        acc_scr[g] = alpha * acc_scr[g] + _bdot(vt_ref[:, 0:n_keys], pr.astype(BF16))
        m_scr[g] = m_new

    @pl.when(ki < qi)
    def _():
        def body(g, carry):
            group(g, t, None)
            return carry
        lax.fori_loop(0, 2 * ng, body, 0)

    @pl.when(ki == qi)
    def _():
        for g in range(2 * ng):
            q_off = (g % ng) * qg
            group(g, q_off + qg, q_off)
        lam = _lambda_value(lam_ref, lam_init)
        g_col = g_ref[...]
        for c in range(t // LANES):
            g1, lanes = (c * LANES) // qg, slice((c * LANES) % qg, (c * LANES) % qg + LANES)
            a1 = acc_scr[g1][:, lanes] * (1.0 / l_scr[g1][:, lanes])
            a2 = acc_scr[ng + g1][:, lanes] * (1.0 / l_scr[ng + g1][:, lanes])
            o = a1 - lam * a2
            o = o * lax.rsqrt(jnp.mean(o * o, axis=0, keepdims=True) + NORM_EPS) * g_col * (1.0 - lam_init)
            o_ref[c * LANES:(c + 1) * LANES, :] = o.T.astype(BF16)


def _diff_prompt(qb, kb, vt, lam_vec, g_d, bsz, seq_len, lam_init):
    t = min(1024, seq_len)
    nq = seq_len // t
    pairs = [(i, j) for i in range(nq) for j in range(i + 1)]
    qi = jnp.asarray([a for a, _ in pairs], jnp.int32)
    ki = jnp.asarray([b for _, b in pairs], jnp.int32)
    g_col = g_d.reshape(DV_DIFF, 1)
    qg = min(Q_GROUP, t)
    n_grp = 2 * t // qg
    grid_spec = pltpu.PrefetchScalarGridSpec(
        num_scalar_prefetch=2,
        grid=(bsz, H_DIFF, len(pairs)),
        in_specs=[
            pl.BlockSpec((t, 2 * DH_DIFF), lambda b, h, p, qi, ki: (b * nq + qi[p], h)),
            pl.BlockSpec((t, 2 * DH_DIFF), lambda b, h, p, qi, ki: (b * nq + ki[p], h)),
            pl.BlockSpec((DV_DIFF, t), lambda b, h, p, qi, ki: (h, b * nq + ki[p])),
            pl.BlockSpec(lam_vec.shape, lambda b, h, p, qi, ki: (0, 0)),
            pl.BlockSpec(g_col.shape, lambda b, h, p, qi, ki: (0, 0)),
        ],
        out_specs=pl.BlockSpec((t, DV_DIFF), lambda b, h, p, qi, ki: (b * nq + qi[p], h)),
        scratch_shapes=[
            pltpu.VMEM((n_grp, qg, 2 * DH_DIFF), BF16),
            pltpu.VMEM((t, 2 * DH_DIFF), BF16),
            pltpu.VMEM((n_grp, 1, qg), F32),
            pltpu.VMEM((n_grp, 1, qg), F32),
            pltpu.VMEM((n_grp, DV_DIFF, qg), F32),
        ],
    )
    return pl.pallas_call(
        functools.partial(_diff_kernel, t=t, lam_init=lam_init),
        grid_spec=grid_spec,
        out_shape=jax.ShapeDtypeStruct((bsz * seq_len, H_DIFF * DV_DIFF), BF16),
        compiler_params=_params("parallel", "parallel", "arbitrary"),
        name="diff_prompt",
    )(qi, ki, qb, kb, vt, lam_vec, g_col)


PAGES_PER_STEP = 8
N_MAPS = 2 * H_DIFF


def _group_sums(prod):
    return jnp.concatenate(
        [jnp.sum(prod[g * DH_DIFF:(g + 1) * DH_DIFF, :], axis=0, keepdims=True) for g in range(N_MAPS)], axis=0)


def _diff_step_kernel(pt_ref, q_ref, kn_ref, vn_ref, lam_ref, g_ref, *rest, lam_init):
    pps = PAGES_PER_STEP
    k_refs = rest[:pps]
    v_refs = rest[pps:2 * pps]
    o_ref = rest[2 * pps]
    qb_scr, m_scr, l_scr, acc_scr = rest[2 * pps + 1:]
    j = pl.program_id(1)
    scale = DH_DIFF ** -0.5

    @pl.when(j == 0)
    def _():
        qb_scr[...] = jnp.broadcast_to(q_ref[...], (W_DIFF, PAGE_SIZE))
        m_scr[...] = jnp.full_like(m_scr, NEG_BIG)
        l_scr[...] = jnp.zeros_like(l_scr)
        acc_scr[...] = jnp.zeros_like(acc_scr)

    scores = [_group_sums(k_refs[pg][...] * qb_scr[...]) * scale for pg in range(pps)]
    s_max = scores[0]
    for pg in range(1, pps):
        s_max = jnp.maximum(s_max, scores[pg])
    m_prev = m_scr[...]
    m_new = jnp.maximum(m_prev, jnp.max(s_max, axis=-1, keepdims=True))
    alpha = jnp.exp(m_prev - m_new)
    row_head = lax.broadcasted_iota(jnp.int32, (N_MAPS, DV_DIFF), 0) // 2
    p_sum = jnp.zeros((N_MAPS, PAGE_SIZE), F32)
    pv = jnp.zeros((N_MAPS, DV_DIFF), F32)
    for pg in range(pps):
        pr = jnp.exp(scores[pg] - m_new)
        p_sum = p_sum + pr
        p_bits = lax.bitcast_convert_type(pr, jnp.uint32) & jnp.uint32(0xFFFF0000)
        p_top = lax.bitcast_convert_type(p_bits, F32)
        p_hi = p_top.astype(BF16)
        p_both = jnp.concatenate([p_top, pr - p_top], axis=0).astype(BF16)
        for hd in range(H_DIFF):
            v_hi, v_lo = _hi_lo(v_refs[pg][pl.ds(hd, PAGE_SIZE, stride=H_DIFF), :])
            r = _bdot(p_both, v_hi)
            r = r[0:N_MAPS, :] + r[N_MAPS:2 * N_MAPS, :] + _bdot(p_hi, v_lo)
            pv = pv + jnp.where(row_head == hd, r, 0.0)
    l_scr[...] = alpha * l_scr[...] + jnp.sum(p_sum, axis=-1, keepdims=True)
    acc_scr[...] = alpha * acc_scr[...] + pv
    m_scr[...] = m_new

    @pl.when(j == pl.num_programs(1) - 1)
    def _():
        s = _group_sums(q_ref[...] * kn_ref[...]) * scale
        m_old = m_scr[...]
        m_fin = jnp.maximum(m_old, s)
        a_fin = jnp.exp(m_old - m_fin)
        pr = jnp.exp(s - m_fin)
        l_fin = a_fin * l_scr[...] + pr
        a = (a_fin * acc_scr[...] + pr * vn_ref[...]) / l_fin
        lam = _lambda_value(lam_ref, lam_init)
        g = g_ref[...]
        for hd in range(H_DIFF):
            d = a[2 * hd:2 * hd + 1, :] - lam * a[2 * hd + 1:2 * hd + 2, :]
            o_ref[hd:hd + 1, :] = _rms(d) * g * (1.0 - lam_init)


def _diff_step(qb, kb_new, vb_new, cache_k, cache_v, layer_idx, page_table, lam_vec, g_d, lam_init):
    bsz, n_pages = page_table.shape
    n_layers, n_pool = cache_k.shape[0:2]
    pps = PAGES_PER_STEP
    assert n_pages % pps == 0
    ck = jnp.transpose(cache_k, (0, 1, 3, 4, 5, 2)).reshape(n_layers * n_pool, W_DIFF, PAGE_SIZE)
    cv = cache_v.reshape(n_layers * n_pool, PAGE_SIZE * H_DIFF, DV_DIFF)
    page_table = page_table + layer_idx * n_pool
    col_spec = pl.BlockSpec((None, W_DIFF, 1), lambda b, j, pt: (b, 0, 0))
    v_new = jnp.repeat(vb_new.reshape(bsz, H_DIFF, DV_DIFF), 2, axis=1)

    def page_spec(pg):
        return pl.BlockSpec((None, W_DIFF, PAGE_SIZE), lambda b, j, pt, pg=pg: (pt[b, j * pps + pg], 0, 0))

    grid_spec = pltpu.PrefetchScalarGridSpec(
        num_scalar_prefetch=1,
        grid=(bsz, n_pages // pps),
        in_specs=[col_spec, col_spec,
                  pl.BlockSpec((None, N_MAPS, DV_DIFF), lambda b, j, pt: (b, 0, 0)),
                  pl.BlockSpec(lam_vec.shape, lambda b, j, pt: (0, 0)),
                  pl.BlockSpec(g_d.shape, lambda b, j, pt: (0, 0))]
                 + [page_spec(pg) for pg in range(pps)] + [page_spec(pg) for pg in range(pps)],
        out_specs=pl.BlockSpec((None, H_DIFF, DV_DIFF), lambda b, j, pt: (b, 0, 0)),
        scratch_shapes=[
            pltpu.VMEM((W_DIFF, PAGE_SIZE), F32),
            pltpu.VMEM((N_MAPS, 1), F32),
            pltpu.VMEM((N_MAPS, 1), F32),
            pltpu.VMEM((N_MAPS, DV_DIFF), F32),
        ],
    )
    o = pl.pallas_call(
        functools.partial(_diff_step_kernel, lam_init=lam_init),
        grid_spec=grid_spec,
        out_shape=jax.ShapeDtypeStruct((bsz, H_DIFF, DV_DIFF), F32),
        compiler_params=_params("parallel", "arbitrary"),
        name="diff_step",
    )(page_table, qb.reshape(bsz, W_DIFF, 1), kb_new.reshape(bsz, W_DIFF, 1), v_new, lam_vec, g_d,
      *([ck] * pps), *([cv] * pps))
    return o.reshape(bsz, W_DIFF)


def _out_even_kernel(a_ref, b_ref, x_ref, gt_ref, *rest):
    w_refs = rest[:-1]
    o_ref = rest[-1]
    m = _mm(a_ref[...], w_refs, slice(0, W_RET)) + _mm(b_ref[...], w_refs, slice(W_RET, 2 * W_RET))
    o_ref[...] = x_ref[...] + gt_ref[...] * m


def _out_even(o_ret, o_diff, ws, x2d, gt, tm, tiles_per_seq):
    m = x2d.shape[0]
    row = lambda n: pl.BlockSpec((tm, n), lambda i: (i, 0))
    return pl.pallas_call(
        _out_even_kernel,
        grid=(m // tm,),
        in_specs=[row(W_RET), row(W_RET), row(D_MODEL), _vec_spec(gt, tm, tiles_per_seq)]
                 + [_full_spec(w) for w in ws],
        out_specs=row(D_MODEL),
        out_shape=jax.ShapeDtypeStruct((m, D_MODEL), F32),
        compiler_params=_params("parallel"),
        name="out_even",
    )(o_ret, o_diff, x2d, gt, *ws)


def _ffn_kernel(x_ref, g_ref, sh_ref, sc_ref, gt_ref, *rest, n_w):
    w1, w3, w2 = rest[0:n_w], rest[n_w:2 * n_w], rest[2 * n_w:3 * n_w]
    o_ref = rest[3 * n_w]
    x = x_ref[...]
    h = _modulate(x, g_ref[...], sh_ref[...], sc_ref[...])
    if n_w == 1:
        h = h.astype(BF16)
    u = jax.nn.silu(_mm(h, w1)) * _mm(h, w3)
    o_ref[...] = x + gt_ref[...] * _mm(u, w2)


def _ffn(x2d, g, sh, sc, gt, w1s, w3s, w2s, tm, tiles_per_seq):
    m = x2d.shape[0]
    row = pl.BlockSpec((tm, D_MODEL), lambda i: (i, 0))
    vs = lambda a: _vec_spec(a, tm, tiles_per_seq)
    ws = (*w1s, *w3s, *w2s)
    return pl.pallas_call(
        functools.partial(_ffn_kernel, n_w=len(w1s)),
        grid=(m // tm,),
        in_specs=[row, _full_spec(g), vs(sh), vs(sc), vs(gt)] + [_full_spec(w) for w in ws],
        out_specs=row,
        out_shape=jax.ShapeDtypeStruct((m, D_MODEL), F32),
        compiler_params=_params("parallel"),
        name="ffn",
    )(x2d, g, sh, sc, gt, *ws)


def _modmm_kernel(x_ref, g_ref, sh_ref, sc_ref, *rest):
    w_refs = rest[:-1]
    o_ref = rest[-1]
    o_ref[...] = _mm(_modulate(x_ref[...], g_ref[...], sh_ref[...], sc_ref[...]), w_refs)


def _modmm(x2d, g, sh, sc, ws, tm, tiles_per_seq):
    m = x2d.shape[0]
    n = ws[0].shape[1]
    vs = lambda a: _vec_spec(a, tm, tiles_per_seq)
    return pl.pallas_call(
        _modmm_kernel,
        grid=(m // tm,),
        in_specs=[pl.BlockSpec((tm, D_MODEL), lambda i: (i, 0)), _full_spec(g), vs(sh), vs(sc)]
                 + [_full_spec(w) for w in ws],
        out_specs=pl.BlockSpec((tm, n), lambda i: (i, 0)),
        out_shape=jax.ShapeDtypeStruct((m, n), F32),
        compiler_params=_params("parallel"),
        name="modmm",
    )(x2d, g, sh, sc, *ws)


def _mm_at(x, w_refs, idx):
    if len(w_refs) == 1:
        return _bdot(x.astype(BF16), w_refs[0][idx])
    return _split_dot(x, w_refs[0][idx], w_refs[1][idx])


def _s5_input_drive(u, bmat_refs, j):
    return _mm_at(u[:, j * LANES:(j + 1) * LANES], bmat_refs, j)


def _s5_output_block(h_re_of, h_im_of, cmat_refs, jb):
    acc = None
    for q in range(4):
        r = 4 * jb + q
        t = (_mm_at(h_re_of(r), cmat_refs, (r, slice(0, LANES), slice(None)))
             + _mm_at(h_im_of(r), cmat_refs, (r, slice(LANES, 2 * LANES), slice(None))))
        acc = t if acc is None else acc + t
    return acc


def _s5_prompt_kernel(u_ref, are_ref, aim_ref, bmat_ref, cmat_ref, d_ref, z_ref, hre_ref, him_ref,
                      sre, sim, st_re, st_im, *, tc):
    j = pl.program_id(1)

    @pl.when(j == 0)
    def _():
        st_re[...] = jnp.zeros_like(st_re)
        st_im[...] = jnp.zeros_like(st_im)

    u = u_ref[...]
    half = 4 * LANES
    for jb in range(8):
        res = _s5_input_drive(u, (bmat_ref,), jb)
        for q in range(4):
            r = 4 * jb + q
            sre[pl.ds(r, tc, stride=STEP_PITCH), :] = res[:, q * LANES:(q + 1) * LANES]
            sim[pl.ds(r, tc, stride=STEP_PITCH), :] = res[:, half + q * LANES:half + (q + 1) * LANES]

    a_re = are_ref[...]
    a_im = aim_ref[...]

    def step(t, carry):
        h_re, h_im = carry
        base = pl.multiple_of(t * STEP_PITCH, 8)
        n_re = a_re * h_re - a_im * h_im + sre[pl.ds(base, N_PAIRS), :]
        n_im = a_re * h_im + a_im * h_re + sim[pl.ds(base, N_PAIRS), :]
        sre[pl.ds(base, N_PAIRS), :] = n_re
        sim[pl.ds(base, N_PAIRS), :] = n_im
        return n_re, n_im

    h_re, h_im = lax.fori_loop(0, tc, step, (st_re[...], st_im[...]))
    st_re[...] = h_re
    st_im[...] = h_im

    for jb in range(8):
        y = _s5_output_block(lambda r: sre[pl.ds(r, tc, stride=STEP_PITCH), :],
                             lambda r: sim[pl.ds(r, tc, stride=STEP_PITCH), :], (cmat_ref,), jb)
        sl = slice(jb * LANES, (jb + 1) * LANES)
        y = y + d_ref[:, sl] * u[:, sl]
        z_ref[:, sl] = jax.nn.gelu(y, approximate=True).astype(BF16)

    @pl.when(j == pl.num_programs(1) - 1)
    def _():
        hre_ref[...] = h_re
        him_ref[...] = h_im


def _s5_prompt(u2d, bsz, seq_len, a_re, a_im, bmat, cmat, d_row):
    tc = min(256, seq_len)
    nj = seq_len // tc
    state_spec = pl.BlockSpec((None, N_PAIRS, LANES), lambda b, j: (b, 0, 0))
    return pl.pallas_call(
        functools.partial(_s5_prompt_kernel, tc=tc),
        grid=(bsz, nj),
        in_specs=[pl.BlockSpec((tc, D_MODEL), lambda b, j: (b * nj + j, 0)),
                  _full_spec(a_re), _full_spec(a_im), _full_spec(bmat), _full_spec(cmat), _full_spec(d_row)],
        out_specs=[pl.BlockSpec((tc, D_MODEL), lambda b, j: (b * nj + j, 0)), state_spec, state_spec],
        out_shape=[
            jax.ShapeDtypeStruct((bsz * seq_len, D_MODEL), BF16),
            jax.ShapeDtypeStruct((bsz, N_PAIRS, LANES), F32),
            jax.ShapeDtypeStruct((bsz, N_PAIRS, LANES), F32),
        ],
        scratch_shapes=[
            pltpu.VMEM((tc * STEP_PITCH, LANES), F32),
            pltpu.VMEM((tc * STEP_PITCH, LANES), F32),
            pltpu.VMEM((N_PAIRS, LANES), F32),
            pltpu.VMEM((N_PAIRS, LANES), F32),
        ],
        compiler_params=_params("parallel", "arbitrary"),
        name="s5_prompt",
    )(u2d, a_re, a_im, bmat, cmat, d_row)


def _s5_step_kernel(u_ref, h0re_ref, h0im_ref, are_ref, aim_ref, bhi_ref, blo_ref, chi_ref, clo_ref, d_ref,
                    z_ref, hre_ref, him_ref):
    u = u_ref[...]
    half = 4 * LANES
    bmat_refs = (bhi_ref, blo_ref)
    cmat_refs = (chi_ref, clo_ref)
    for jb in range(8):
        res = _s5_input_drive(u, bmat_refs, jb)
        sl = slice(jb * half, (jb + 1) * half)
        a_re = are_ref[:, sl]
        a_im = aim_ref[:, sl]
        h_re = h0re_ref[:, sl]
        h_im = h0im_ref[:, sl]
        hre_ref[:, sl] = a_re * h_re - a_im * h_im + res[:, 0:half]
        him_ref[:, sl] = a_re * h_im + a_im * h_re + res[:, half:2 * half]
    for jb in range(8):
        y = _s5_output_block(lambda r: hre_ref[:, r * LANES:(r + 1) * LANES],
                             lambda r: him_ref[:, r * LANES:(r + 1) * LANES], cmat_refs, jb)
        sl = slice(jb * LANES, (jb + 1) * LANES)
        y = y + d_ref[:, sl] * u[:, sl]
        z_ref[:, sl] = jax.nn.gelu(y, approximate=True)


def _s5_step(u2d, h0_re, h0_im, a_re_row, a_im_row, bmats, cmats, d_row):
    bsz = u2d.shape[0]
    n_state = N_GROUPS * SSM_STATE
    args = (u2d, h0_re, h0_im, a_re_row, a_im_row, *bmats, *cmats, d_row)
    return pl.pallas_call(
        _s5_step_kernel,
        grid=(1,),
        in_specs=[_full_spec(a) for a in args],
        out_specs=[pl.BlockSpec((bsz, D_MODEL), lambda i: (0, 0)),
                   pl.BlockSpec((bsz, n_state), lambda i: (0, 0)),
                   pl.BlockSpec((bsz, n_state), lambda i: (0, 0))],
        out_shape=[
            jax.ShapeDtypeStruct((bsz, D_MODEL), F32),
            jax.ShapeDtypeStruct((bsz, n_state), F32),
            jax.ShapeDtypeStruct((bsz, n_state), F32),
        ],
        compiler_params=_params("arbitrary"),
        name="s5_step",
    )(*args)


def _glu_router_kernel(z_ref, x_ref, gt_ref, g_ref, sh_ref, sc_ref, wrh_ref, wrl_ref, br_ref, *rest):
    wg_refs = rest[:-3]
    x_out, h_out, route_out = rest[-3:]
    vg = _mm(z_ref[...], wg_refs)
    val = vg[:, 0:D_MODEL]
    gate = vg[:, D_MODEL:2 * D_MODEL]
    x = x_ref[...] + gt_ref[...] * (val * jax.nn.sigmoid(gate))
    x_out[...] = x
    h = _modulate(x, g_ref[...], sh_ref[...], sc_ref[...])
    h_out[...] = h
    logits = _split_dot(h, wrh_ref[...], wrl_ref[...]) + br_ref[...]
    lane = lax.broadcasted_iota(jnp.int32, logits.shape, 1)
    v1 = jnp.max(logits, axis=-1, keepdims=True)
    i1 = jnp.min(jnp.where(logits == v1, lane, LANES), axis=-1, keepdims=True)
    others = jnp.where(lane == i1, NEG_BIG, logits)
    v2 = jnp.max(others, axis=-1, keepdims=True)
    i2 = jnp.min(jnp.where(others == v2, lane, LANES), axis=-1, keepdims=True)
    e = jnp.exp(v2 - v1)
    g1 = 1.0 / (1.0 + e)
    g2 = e / (1.0 + e)
    route = jnp.where(lane == i1, g1, 0.0) + jnp.where(lane == i2, g2, 0.0)
    flag = jnp.logical_or(lane == i1 + N_EXPERTS, lane == i2 + N_EXPERTS)
    route_out[...] = route + jnp.where(flag, 1.0, 0.0)


def _glu_router(z, wgs, x2d, gt, g, sh, sc, wr_hi, wr_lo, br, tm, tiles_per_seq):
    m = x2d.shape[0]
    row = lambda n: pl.BlockSpec((tm, n), lambda i: (i, 0))
    vs = lambda a: _vec_spec(a, tm, tiles_per_seq)
    return pl.pallas_call(
        _glu_router_kernel,
        grid=(m // tm,),
        in_specs=[row(D_MODEL), row(D_MODEL), vs(gt), _full_spec(g), vs(sh), vs(sc),
                  _full_spec(wr_hi), _full_spec(wr_lo), _full_spec(br)] + [_full_spec(w) for w in wgs],
        out_specs=[row(D_MODEL), row(D_MODEL), row(LANES)],
        out_shape=[
            jax.ShapeDtypeStruct((m, D_MODEL), F32),
            jax.ShapeDtypeStruct((m, D_MODEL), F32),
            jax.ShapeDtypeStruct((m, LANES), F32),
        ],
        compiler_params=_params("parallel"),
        name="glu_router",
    )(z, x2d, gt, g, sh, sc, wr_hi, wr_lo, br, *wgs)


def _moe_kernel(te_ref, nv_ref, xs_ref, w1_ref, w3_ref, w2_ref, o_ref, x_scr, acc_scr):
    i = pl.program_id(0)
    f = pl.program_id(1)
    nf = pl.num_programs(1)
    valid = i < nv_ref[0]

    @pl.when(jnp.logical_and(valid, f == 0))
    def _():
        x_scr[...] = xs_ref[...].astype(BF16)

    @pl.when(valid)
    def _():
        x = x_scr[...]
        a = _bdot(x, w1_ref[...].astype(BF16))
        b = _bdot(x, w3_ref[...].astype(BF16))
        u = (jax.nn.silu(a) * b).astype(BF16)
        contrib = _bdot(u, w2_ref[...].astype(BF16))

        @pl.when(f == 0)
        def _():
            acc_scr[...] = contrib

        @pl.when(f > 0)
        def _():
            acc_scr[...] = acc_scr[...] + contrib

        @pl.when(f == nf - 1)
        def _():
            o_ref[...] = acc_scr[...]

    @pl.when(jnp.logical_and(jnp.logical_not(valid), f == nf - 1))
    def _():
        o_ref[...] = jnp.zeros_like(o_ref)


def _moe_experts(xs, tile_expert, n_valid, w1, w3, w2, tm):
    rows = xs.shape[0]
    n_tiles = rows // tm
    tf = 512
    nf = D_FF_EXPERT // tf

    def fsel(i, f, nv):
        return jnp.where(i < nv[0], f, nf - 1)

    grid_spec = pltpu.PrefetchScalarGridSpec(
        num_scalar_prefetch=2,
        grid=(n_tiles, nf),
        in_specs=[
            pl.BlockSpec((tm, D_MODEL), lambda i, f, te, nv: (i, 0)),
            pl.BlockSpec((None, D_MODEL, tf), lambda i, f, te, nv: (te[i], 0, fsel(i, f, nv))),
            pl.BlockSpec((None, D_MODEL, tf), lambda i, f, te, nv: (te[i], 0, fsel(i, f, nv))),
            pl.BlockSpec((None, tf, D_MODEL), lambda i, f, te, nv: (te[i], fsel(i, f, nv), 0)),
        ],
        out_specs=pl.BlockSpec((tm, D_MODEL), lambda i, f, te, nv: (i, 0)),
        scratch_shapes=[pltpu.VMEM((tm, D_MODEL), BF16), pltpu.VMEM((tm, D_MODEL), F32)],
    )
    return pl.pallas_call(
        _moe_kernel,
        grid_spec=grid_spec,
        out_shape=jax.ShapeDtypeStruct((rows, D_MODEL), F32),
        compiler_params=_params("parallel", "arbitrary"),
        name="moe_experts",
    )(tile_expert, n_valid, xs, w1, w3, w2)


def _row_params(*sem):
    return pltpu.CompilerParams(dimension_semantics=sem, vmem_limit_bytes=VMEM_LIMIT_BYTES,
                                disable_bounds_checks=True)


def _dispatch_kernel(sa_ref, sb_ref, h_ref, xs_in_ref, xs_ref, sem, *, tm):
    del xs_in_ref
    base = pl.program_id(0) * tm

    def copies(t):
        tok = base + t
        src = h_ref.at[pl.ds(tok, 1)]
        return (pltpu.make_async_copy(src, xs_ref.at[pl.ds(sa_ref[tok], 1)], sem.at[0]),
                pltpu.make_async_copy(src, xs_ref.at[pl.ds(sb_ref[tok], 1)], sem.at[1]))

    def issue(t, carry):
        for cp in copies(t):
            cp.start()
        return carry

    def drain(t, carry):
        for cp in copies(t):
            cp.wait()
        return carry

    lax.fori_loop(0, tm, issue, 0)
    lax.fori_loop(0, tm, drain, 0)


def _moe_dispatch(h, slot_a, slot_b, n_rows, tm):
    m = h.shape[0]
    grid_spec = pltpu.PrefetchScalarGridSpec(
        num_scalar_prefetch=2,
        grid=(m // tm,),
        in_specs=[pl.BlockSpec(memory_space=pl.ANY), pl.BlockSpec(memory_space=pl.ANY)],
        out_specs=pl.BlockSpec(memory_space=pl.ANY),
        scratch_shapes=[pltpu.SemaphoreType.DMA((2,))],
    )
    return pl.pallas_call(
        functools.partial(_dispatch_kernel, tm=tm),
        grid_spec=grid_spec,
        out_shape=jax.ShapeDtypeStruct((n_rows, D_MODEL), F32),
        input_output_aliases={3: 0},
        compiler_params=_row_params("arbitrary"),
        name="moe_dispatch",
    )(slot_a, slot_b, h, jnp.zeros((n_rows, D_MODEL), F32))


def _combine_kernel(sa_ref, sb_ref, x_ref, gt_ref, ga_ref, gb_ref, ys_ref, o_ref, buf_a, buf_b, sem, *, tm):
    base = pl.program_id(0) * tm

    def copies(t):
        tok = base + t
        return (pltpu.make_async_copy(ys_ref.at[pl.ds(sa_ref[tok], 1)], buf_a.at[pl.ds(t, 1)], sem.at[0]),
                pltpu.make_async_copy(ys_ref.at[pl.ds(sb_ref[tok], 1)], buf_b.at[pl.ds(t, 1)], sem.at[1]))

    def issue(t, carry):
        for cp in copies(t):
            cp.start()
        return carry

    def drain(t, carry):
        for cp in copies(t):
            cp.wait()
        return carry

    lax.fori_loop(0, tm, issue, 0)
    lax.fori_loop(0, tm, drain, 0)
    o_ref[...] = x_ref[...] + gt_ref[...] * (ga_ref[...] * buf_a[...] + gb_ref[...] * buf_b[...])


def _moe_combine(x2d, gt, gate_a, gate_b, ys, slot_a, slot_b, tm, tiles_per_seq):
    m = x2d.shape[0]
    row = pl.BlockSpec((tm, D_MODEL), lambda i, sa, sb: (i, 0))
    col = pl.BlockSpec((tm, 1), lambda i, sa, sb: (i, 0))
    grid_spec = pltpu.PrefetchScalarGridSpec(
        num_scalar_prefetch=2,
        grid=(m // tm,),
        in_specs=[row, _vec_spec(gt, tm, tiles_per_seq), col, col, pl.BlockSpec(memory_space=pl.ANY)],
        out_specs=row,
        scratch_shapes=[pltpu.VMEM((tm, D_MODEL), F32), pltpu.VMEM((tm, D_MODEL), F32),
                        pltpu.SemaphoreType.DMA((2,))],
    )
    return pl.pallas_call(
        functools.partial(_combine_kernel, tm=tm),
        grid_spec=grid_spec,
        out_shape=jax.ShapeDtypeStruct((m, D_MODEL), F32),
        compiler_params=_row_params("arbitrary"),
        name="moe_combine",
    )(slot_a, slot_b, x2d, gt, gate_a, gate_b, ys)


def _moe(h, route, x2d, gt, w1, w3, w2, tm_rows, tm, tiles_per_seq):
    m = h.shape[0]
    comb = route[:, 0:N_EXPERTS]
    sel = route[:, N_EXPERTS:2 * N_EXPERTS] > 0.5
    seli = sel.astype(jnp.int32)
    counts = jnp.sum(seli, axis=0)
    padded = ((counts + tm_rows - 1) // tm_rows) * tm_rows
    ends = jnp.cumsum(padded)
    offsets = ends - padded
    rank = jnp.cumsum(seli, axis=0) - 1
    pos = offsets[None, :] + rank
    n_rows = ((2 * m + N_EXPERTS * (tm_rows - 1)) // tm_rows + 1) * tm_rows
    n_tiles = n_rows // tm_rows
    tile_start = jnp.arange(n_tiles, dtype=jnp.int32) * tm_rows
    n_valid = (ends[-1] // tm_rows).astype(jnp.int32)
    first_row = jnp.minimum(tile_start, ends[-1] - 1)
    tile_expert = jnp.sum((ends[None, :] <= first_row[:, None]).astype(jnp.int32), axis=1)
    slot_a = jnp.min(jnp.where(sel, pos, n_rows), axis=1).astype(jnp.int32)
    slot_b = jnp.max(jnp.where(sel, pos, -1), axis=1).astype(jnp.int32)
    expert_id = jnp.arange(N_EXPERTS, dtype=jnp.int32)[None, :]
    e_lo = jnp.min(jnp.where(sel, expert_id, N_EXPERTS), axis=1, keepdims=True)
    e_hi = jnp.max(jnp.where(sel, expert_id, -1), axis=1, keepdims=True)
    gate_a = jnp.sum(jnp.where(expert_id == e_lo, comb, 0.0), axis=1, keepdims=True)
    gate_b = jnp.sum(jnp.where(expert_id == e_hi, comb, 0.0), axis=1, keepdims=True)
    tm_dispatch = min(1024, m)
    xs = _moe_dispatch(h, slot_a, slot_b, n_rows, tm_dispatch)
    ys = _moe_experts(xs, tile_expert, n_valid.reshape(1), w1, w3, w2, tm_rows)
    return _moe_combine(x2d, gt, gate_a, gate_b, ys, slot_a, slot_b, tm, tiles_per_seq)


def _retention_constants(chunk):
    log_g = jnp.log1p(-jnp.exp2(-5.0 - jnp.arange(H_RET, dtype=F32)))
    idx = jnp.arange(chunk, dtype=F32)
    rel = idx[:, None] - idx[None, :]
    dm = jnp.where(rel >= 0, jnp.exp(log_g[:, None, None] * jnp.maximum(rel, 0.0)), 0.0)
    q_decay = jnp.exp(log_g[:, None] * (idx[None, :] + 1.0))
    k_decay = jnp.exp(log_g[:, None] * (chunk - 1.0 - idx[None, :]))
    chunk_decay = jnp.exp(log_g * chunk)
    qd = jnp.broadcast_to(q_decay[:, :, None], (H_RET, chunk, DK_RET))
    kd = jnp.broadcast_to(k_decay[:, :, None], (H_RET, chunk, DK_RET))
    cd = jnp.broadcast_to(chunk_decay[:, None, None], (H_RET, 1, DV_RET))
    return dm, qd, kd, cd


def _rotary_tables(pos):
    half = DK_RET // 2
    inv = np.float64(ROPE_BASE) ** (-np.arange(half, dtype=np.float64) / half)
    inv_hi = inv.astype(np.float32)
    inv_lo = (inv - inv_hi.astype(np.float64)).astype(np.float32)
    p = jnp.asarray(pos).astype(F32)[:, None]
    ang = p * jnp.asarray(inv_hi)[None, :] + p * jnp.asarray(inv_lo)[None, :]
    cos = jnp.cos(ang)
    sin = jnp.sin(ang)
    return jnp.concatenate([cos, cos], axis=-1), jnp.concatenate([-sin, sin], axis=-1)


def _s5_matrices(lam_re, lam_im, log_step, b_re, b_im, c_re, c_im, d_skip):
    lr = lam_re.astype(F32)
    li = lam_im.astype(F32)
    step = jnp.exp(log_step.astype(F32))[:, None]
    mag = jnp.exp(lr * step)
    ab_re = mag * jnp.cos(li * step)
    ab_im = mag * jnp.sin(li * step)
    nr = ab_re - 1.0
    ni = ab_im
    den = lr * lr + li * li
    kr = (nr * lr + ni * li) / den
    ki = (ni * lr - nr * li) / den
    br = b_re.astype(F32)
    bi = b_im.astype(F32)
    bb_re = kr[..., None] * br - ki[..., None] * bi
    bb_im = kr[..., None] * bi + ki[..., None] * br
    eye8 = jnp.eye(8, dtype=F32)

    def drive(bb):
        t = bb.reshape(8, 8, SSM_STATE, SSM_GROUP).transpose(0, 1, 3, 2)
        t = t[:, :, :, None, :] * eye8[None, :, None, :, None]
        return t.reshape(8, LANES, 8 * SSM_STATE)

    bmat = jnp.concatenate([drive(bb_re), drive(bb_im)], axis=-1)
    r_idx = jnp.arange(N_PAIRS)[:, None, None]
    gl_idx = jnp.arange(2)[None, :, None]
    gb_idx = jnp.arange(8)[None, None, :]
    onehot = (gb_idx == 2 * (r_idx % 4) + gl_idx).astype(F32)

    def readout(c):
        t = c.astype(F32).reshape(N_PAIRS, 2, SSM_GROUP, SSM_STATE).transpose(0, 1, 3, 2)
        t = t[:, :, :, None, :] * onehot[:, :, None, :, None]
        return t.reshape(N_PAIRS, LANES, LANES)

    cmat = jnp.concatenate([readout(c_re), -readout(c_im)], axis=1)
    return (ab_re.reshape(N_PAIRS, LANES), ab_im.reshape(N_PAIRS, LANES), bmat, cmat,
            d_skip.astype(F32).reshape(1, D_MODEL))


def _trunk(x, mods, pos, ret0, ssm0, paged, p):
    bsz, seq_len, _ = x.shape
    m = bsz * seq_len
    prompt = seq_len > 1
    depth = p["w_ada"].shape[0]
    x2d = x.reshape(m, D_MODEL)
    if prompt:
        tm = min(512, seq_len)
        tps = seq_len // tm
        tm_ffn = min(256, seq_len)
        tps_ffn = seq_len // tm_ffn
        vec = lambda v: v.reshape(bsz, 1, D_MODEL)
    else:
        tm = tm_ffn = m
        tps = tps_ffn = 1
        vec = lambda v: v.reshape(1, bsz, D_MODEL)
    row = lambda v: v.reshape(1, -1).astype(F32)
    wsp = lambda w: _wsplit(w, not prompt)
    rets, ks_new, vs_new, srs, sis = [], [], [], [], []
    for layer in range(depth):
        sh1, sc1, gt1, sh2, sc2, gt2 = [vec(v) for v in jnp.split(mods[layer], N_ADA, axis=-1)]
        i = layer // 2
        if layer % 2 == 0:
            lam_init = 0.8 - 0.6 * math.exp(-0.3 * layer)
            cos_t, sin_t = _rotary_tables(pos)
            seg = (jnp.arange(W_DIFF)[:, None] // DH_DIFF == jnp.arange(W_DIFF)[None, :] // DH_DIFF).astype(BF16)
            gq = jnp.tile(row(p["g_qnorm"][i]), (1, W_DIFF // DH_DIFF))
            gk = jnp.tile(row(p["g_knorm"][i]), (1, W_DIFF // DH_DIFF))
            ret_in, qb, kb, vb, *vt = _in_even(
                x2d, row(p["g_norm1"][layer]), sh1, sc1, cos_t, sin_t, wsp(p["w_in_mix"][i]),
                gq, gk, seg, tm, tps, BF16 if prompt else F32, prompt)
            lam_vec = jnp.stack([p["lam_q1"][i], p["lam_k1"][i], p["lam_q2"][i], p["lam_k2"][i]]).astype(F32)
            g_ret = row(p["g_ret"][i])
            g_d = row(p["g_dnorm"][i])
            if prompt:
                dm, qd, kd, cd = _retention_constants(math.gcd(seq_len, RET_CHUNK))
                o_ret, s_new = _retention_prompt(ret_in, bsz, seq_len, dm, qd, kd, cd, g_ret)
                o_diff = _diff_prompt(qb, kb, vt[0], lam_vec, g_d, bsz, seq_len, lam_init)
            else:
                _, qd, _, cd = _retention_constants(1)
                o_ret, s_new = _retention_step(ret_in, ret0[i].astype(F32), qd, cd, g_ret)
                o_diff = _diff_step(qb, kb, vb, paged["cache_k"], paged["cache_v"], i,
                                    paged["page_table"], lam_vec, g_d, lam_init)
            rets.append(s_new)
            ks_new.append(kb.reshape(bsz, seq_len, H_DIFF, 2, DH_DIFF))
            vs_new.append(vb.reshape(bsz, seq_len, H_DIFF, DV_DIFF))
            x2d = _out_even(o_ret, o_diff, wsp(p["w_out_mix"][i]), x2d, gt1, tm, tps)
            x2d = _ffn(x2d, row(p["g_norm2"][layer]), sh2, sc2, gt2, wsp(p["w_ffn1"][i]),
                       wsp(p["w_ffn3"][i]), wsp(p["w_ffn2"][i]), tm_ffn, tps_ffn)
        else:
            a_re, a_im, bmat, cmat, d_row = _s5_matrices(
                p["lam_re"][i], p["lam_im"][i], p["log_step"][i], p["b_re"][i], p["b_im"][i],
                p["c_re"][i], p["c_im"][i], p["d_skip"][i])
            u2d = _modmm(x2d, row(p["g_norm1"][layer]), sh1, sc1, wsp(p["w_ssm_in"][i]), tm, tps)
            n_state = N_GROUPS * SSM_STATE
            if prompt:
                z, h_re, h_im = _s5_prompt(u2d, bsz, seq_len, a_re, a_im, bmat.astype(BF16), cmat.astype(BF16),
                                           d_row)
            else:
                z, h_re, h_im = _s5_step(
                    u2d, ssm0[0][i].astype(F32).reshape(bsz, n_state), ssm0[1][i].astype(F32).reshape(bsz, n_state),
                    a_re.reshape(1, n_state), a_im.reshape(1, n_state), wsp(bmat), wsp(cmat), d_row)
            srs.append(h_re.reshape(bsz, N_GROUPS, SSM_STATE))
            sis.append(h_im.reshape(bsz, N_GROUPS, SSM_STATE))
            wr = jnp.zeros((D_MODEL, LANES), F32).at[:, 0:N_EXPERTS].set(p["w_router"][i].astype(F32))
            wr_hi, wr_lo = _wsplit(wr, True)
            br = jnp.full((1, LANES), NEG_BIG, F32).at[0, 0:N_EXPERTS].set(p["b_router"][i].astype(F32))
            x2d, h_mod, route = _glu_router(
                z, wsp(p["w_glu"][i]), x2d, gt1, row(p["g_norm2"][layer]), sh2, sc2,
                wr_hi, wr_lo, br, tm_ffn, tps_ffn)
            x2d = _moe(h_mod, route, x2d, gt2, p["w_moe1"][i], p["w_moe3"][i], p["w_moe2"][i],
                       512 if prompt else 32, tm, tps)
    return (x2d.reshape(bsz, seq_len, D_MODEL), jnp.stack(rets), jnp.stack(ks_new), jnp.stack(vs_new),
            jnp.stack(srs), jnp.stack(sis))


def kernel(x_prompt, x_sample, state_ret, cache_k, cache_v, state_ssm_re, state_ssm_im, page_table, c_prompt, c_sample, w_ada, b_ada, g_norm1, g_norm2, w_in_mix, w_out_mix, g_ret, g_qnorm, g_knorm, g_dnorm, lam_q1, lam_k1, lam_q2, lam_k2, w_ffn1, w_ffn3, w_ffn2, w_ssm_in, lam_re, lam_im, log_step, b_re, b_im, c_re, c_im, d_skip, w_glu, w_router, b_router, w_moe1, w_moe3, w_moe2):
    p = dict(w_ada=w_ada, g_norm1=g_norm1, g_norm2=g_norm2, w_in_mix=w_in_mix, w_out_mix=w_out_mix,
             g_ret=g_ret, g_qnorm=g_qnorm, g_knorm=g_knorm, g_dnorm=g_dnorm, lam_q1=lam_q1, lam_k1=lam_k1,
             lam_q2=lam_q2, lam_k2=lam_k2, w_ffn1=w_ffn1, w_ffn3=w_ffn3, w_ffn2=w_ffn2, w_ssm_in=w_ssm_in,
             lam_re=lam_re, lam_im=lam_im, log_step=log_step, b_re=b_re, b_im=b_im, c_re=c_re, c_im=c_im,
             d_skip=d_skip, w_glu=w_glu, w_router=w_router, b_router=b_router,
             w_moe1=w_moe1, w_moe3=w_moe3, w_moe2=w_moe2)
    n_prompt = c_prompt.shape[0]
    n_sample = c_sample.shape[0]
    n_all = n_prompt + n_sample
    rows = ((n_all + 7) // 8) * 8
    c_all = jnp.zeros((rows, D_MODEL), F32).at[0:n_prompt].set(c_prompt).at[n_prompt:n_all].set(c_sample)
    mods = _ada(c_all, w_ada, b_ada)
    seq_len = x_prompt.shape[1]
    past_len = page_table.shape[1] * PAGE_SIZE
    y_p, ret_p, k_p, v_p, sr_p, si_p = _trunk(
        x_prompt, mods[:, 0:n_prompt], np.arange(seq_len), None, None, None, p)
    paged = dict(cache_k=cache_k, cache_v=cache_v, page_table=page_table)
    y_s, ret_s, k_s, v_s, sr_s, si_s = _trunk(
        x_sample, mods[:, n_prompt:n_all], past_len + np.arange(x_sample.shape[1]),
        state_ret, (state_ssm_re, state_ssm_im), paged, p)
    return (y_p, y_s, ret_p, ret_s, k_p, v_p, k_s, v_s, sr_p, si_p, sr_s, si_s)
```

```python
import functools
import math

import jax
import jax.numpy as jnp
import numpy as np
from jax import lax
from jax.experimental import pallas as pl
from jax.experimental.pallas import tpu as pltpu

F32 = jnp.float32
BF16 = jnp.bfloat16

D_MODEL = 1024
H_RET = 4
DK_RET = 128
DV_RET = 128
RET_CHUNK = 128
ROPE_BASE = 10000.0
H_DIFF = 4
DH_DIFF = 64
DV_DIFF = 128
PAGE_SIZE = 128
W_RET = H_RET * DK_RET
W_DIFF = H_DIFF * 2 * DH_DIFF
IN_EVEN = 4 * W_RET + 3 * W_DIFF
SSM_GROUP = 16
N_GROUPS = D_MODEL // SSM_GROUP
SSM_STATE = 64
N_PAIRS = N_GROUPS // 2
STEP_PITCH = N_PAIRS + 8
D_FF = 2816
N_EXPERTS = 8
D_FF_EXPERT = 3584
N_ADA = 6
NORM_EPS = 1e-6
LANES = 128
NEG_BIG = -1e30

VMEM_LIMIT_BYTES = 56 * 1024 * 1024


def _params(*sem):
    return pltpu.CompilerParams(dimension_semantics=sem, vmem_limit_bytes=VMEM_LIMIT_BYTES)


def _bdot(a, b):
    return jnp.dot(a, b, preferred_element_type=F32)


def _dot_nt(a, b):
    return lax.dot_general(a, b, (((1,), (1,)), ((), ())), preferred_element_type=F32)


def _dot_tn(a, b):
    return lax.dot_general(a, b, (((0,), (0,)), ((), ())), preferred_element_type=F32)


def _rms(x):
    return x * lax.rsqrt(jnp.mean(x * x, axis=-1, keepdims=True) + NORM_EPS)


def _modulate(x, g, shift, scale):
    return (_rms(x) * g) * (1.0 + scale) + shift


def _hi_lo(x):
    bits = lax.bitcast_convert_type(x, jnp.uint32) & jnp.uint32(0xFFFF0000)
    hi = lax.bitcast_convert_type(bits, F32)
    return hi.astype(BF16), (x - hi).astype(BF16)


def _split_dot(x, w_hi, w_lo):
    x_hi, x_lo = _hi_lo(x)
    return _bdot(x_hi, w_hi) + _bdot(x_lo, w_hi) + _bdot(x_hi, w_lo)


def _mm(x, w_refs, rows=slice(None)):
    if len(w_refs) == 1:
        return _bdot(x.astype(BF16), w_refs[0][rows, :])
    return _split_dot(x.astype(F32), w_refs[0][rows, :], w_refs[1][rows, :])


def _wsplit(w, precise):
    w = w.astype(F32)
    if not precise:
        return (w.astype(BF16),)
    return _hi_lo(w)


def _ada_kernel(c_ref, w_ref, b_ref, o_ref):
    w_hi, w_lo = _hi_lo(w_ref[...])
    o_ref[...] = _split_dot(jax.nn.silu(c_ref[...]), w_hi, w_lo) + b_ref[...]


def _ada(c_all, w_ada, b_ada):
    depth, _, n = w_ada.shape
    rows = c_all.shape[0]
    tn = 1536
    return pl.pallas_call(
        _ada_kernel,
        grid=(depth, n // tn),
        in_specs=[
            pl.BlockSpec((rows, D_MODEL), lambda l, j: (0, 0)),
            pl.BlockSpec((None, D_MODEL, tn), lambda l, j: (l, 0, j)),
            pl.BlockSpec((None, 1, tn), lambda l, j: (l, 0, j)),
        ],
        out_specs=pl.BlockSpec((None, rows, tn), lambda l, j: (l, 0, j)),
        out_shape=jax.ShapeDtypeStruct((depth, rows, n), F32),
        compiler_params=_params("parallel", "parallel"),
        name="ada",
    )(c_all, w_ada, b_ada.reshape(depth, 1, n))


def _vec_spec(arr, tm, tiles_per_seq):
    if arr.shape[1] == 1:
        return pl.BlockSpec((None, 1, arr.shape[2]), lambda i, *_: (i // tiles_per_seq, 0, 0))
    return pl.BlockSpec((None, tm, arr.shape[2]), lambda i, *_: (0, 0, 0))


def _pos_spec(arr, tm, tiles_per_seq):
    if arr.shape[0] == 1:
        return pl.BlockSpec((1, arr.shape[1]), lambda i, *_: (0, 0))
    return pl.BlockSpec((tm, arr.shape[1]), lambda i, *_: (i % tiles_per_seq, 0))


def _full_spec(arr):
    nd = arr.ndim
    return pl.BlockSpec(arr.shape, lambda *_: (0,) * nd)


def _in_even_kernel(x_ref, g_ref, sh_ref, sc_ref, cos_ref, sin_ref, gq_ref, gk_ref,
                    seg_ref, *rest, n_w):
    w_refs = rest[:n_w]
    ret_ref, qb_ref, kb_ref, vb_ref = rest[n_w:n_w + 4]
    vt_ref = rest[n_w + 4] if len(rest) > n_w + 4 else None
    y = _mm(_modulate(x_ref[...], g_ref[...], sh_ref[...], sc_ref[...]), w_refs)
    cos = cos_ref[...]
    sin = sin_ref[...]
    for part in range(2):
        for hd in range(H_RET):
            lo = part * W_RET + hd * DK_RET
            blk = y[:, lo:lo + DK_RET]
            rot = blk * cos + pltpu.roll(blk, DK_RET // 2, axis=1) * sin
            if part == 1:
                rot = rot * (DK_RET ** -0.5)
            ret_ref[:, lo:lo + DK_RET] = rot
    ret_ref[:, 2 * W_RET:4 * W_RET] = y[:, 2 * W_RET:4 * W_RET]
    seg = seg_ref[...]
    base = 4 * W_RET

    def qk_norm(blk, g):
        sq_hi, sq_lo = _hi_lo(blk * blk)
        ms = (_bdot(sq_hi, seg) + _bdot(sq_lo, seg)) * (1.0 / DH_DIFF)
        return blk * lax.rsqrt(ms + NORM_EPS) * g

    qb_ref[...] = qk_norm(y[:, base:base + W_DIFF], gq_ref[...]).astype(qb_ref.dtype)
    kb_ref[...] = qk_norm(y[:, base + W_DIFF:base + 2 * W_DIFF], gk_ref[...])
    v = y[:, base + 2 * W_DIFF:base + 3 * W_DIFF]
    vb_ref[...] = v
    if vt_ref is not None:
        vt_ref[...] = v.T.astype(BF16)


def _in_even(x2d, g, sh, sc, cos_t, sin_t, ws, gq, gk, seg, tm, tiles_per_seq, q_dtype, emit_vt):
    m = x2d.shape[0]
    row = lambda n: pl.BlockSpec((tm, n), lambda i: (i, 0))
    out_specs = [row(4 * W_RET), row(W_DIFF), row(W_DIFF), row(W_DIFF)]
    out_shape = [
        jax.ShapeDtypeStruct((m, 4 * W_RET), F32),
        jax.ShapeDtypeStruct((m, W_DIFF), q_dtype),
        jax.ShapeDtypeStruct((m, W_DIFF), F32),
        jax.ShapeDtypeStruct((m, W_DIFF), F32),
    ]
    if emit_vt:
        out_specs.append(pl.BlockSpec((W_DIFF, tm), lambda i: (0, i)))
        out_shape.append(jax.ShapeDtypeStruct((W_DIFF, m), BF16))
    return pl.pallas_call(
        functools.partial(_in_even_kernel, n_w=len(ws)),
        grid=(m // tm,),
        in_specs=[
            row(D_MODEL), _full_spec(g), _vec_spec(sh, tm, tiles_per_seq), _vec_spec(sc, tm, tiles_per_seq),
            _pos_spec(cos_t, tm, tiles_per_seq), _pos_spec(sin_t, tm, tiles_per_seq),
            _full_spec(gq), _full_spec(gk), _full_spec(seg),
        ] + [_full_spec(w) for w in ws],
        out_specs=out_specs,
        out_shape=out_shape,
        compiler_params=_params("parallel"),
        name="in_even",
    )(x2d, g, sh, sc, cos_t, sin_t, gq, gk, seg, *ws)


def _ret_kernel(q_ref, k_ref, v_ref, ga_ref, dm_ref, qd_ref, kd_ref, cd_ref, g_ref,
                o_ref, sfin_ref, s_scr, *, n_chunks):
    j = pl.program_id(1)

    @pl.when(j == 0)
    def _():
        s_scr[...] = jnp.zeros_like(s_scr)

    g = g_ref[...]
    for hd in range(H_RET):
        sl = slice(hd * DK_RET, (hd + 1) * DK_RET)
        dm = dm_ref[hd]
        qd = qd_ref[hd]
        kd = kd_ref[hd]
        cd = cd_ref[hd]

        def body(c, carry, sl=sl, dm=dm, qd=qd, kd=kd, cd=cd, hd=hd):
            r = pl.multiple_of(c * RET_CHUNK, RET_CHUNK)
            q = q_ref[pl.ds(r, RET_CHUNK), sl]
            k = k_ref[pl.ds(r, RET_CHUNK), sl]
            v = v_ref[pl.ds(r, RET_CHUNK), sl]
            vb = v.astype(BF16)
            s = s_scr[hd]
            att = _dot_nt(q.astype(BF16), k.astype(BF16)) * dm
            o = _bdot(att.astype(BF16), vb) + _bdot((q * qd).astype(BF16), s.astype(BF16))
            s_scr[hd] = cd * s + _dot_tn((k * kd).astype(BF16), vb)
            on = _rms(o) * g
            o_ref[pl.ds(r, RET_CHUNK), sl] = (on * jax.nn.silu(ga_ref[pl.ds(r, RET_CHUNK), sl])).astype(BF16)
            return carry

        lax.fori_loop(0, n_chunks, body, 0)

    @pl.when(j == pl.num_programs(1) - 1)
    def _():
        sfin_ref[...] = s_scr[...]


def _retention_prompt(ret_in, bsz, seq_len, dm, qd, kd, cd, g_ret):
    n_chunks = min(8, seq_len // RET_CHUNK)
    rows = n_chunks * RET_CHUNK
    nj = seq_len // rows
    col = lambda c: pl.BlockSpec((rows, W_RET), lambda b, j: (b * nj + j, c))
    return pl.pallas_call(
        functools.partial(_ret_kernel, n_chunks=n_chunks),
        grid=(bsz, nj),
        in_specs=[col(0), col(1), col(2), col(3), _full_spec(dm), _full_spec(qd), _full_spec(kd),
                  _full_spec(cd), _full_spec(g_ret)],
        out_specs=[
            pl.BlockSpec((rows, W_RET), lambda b, j: (b * nj + j, 0)),
            pl.BlockSpec((None, H_RET, DK_RET, DV_RET), lambda b, j: (b, 0, 0, 0)),
        ],
        out_shape=[
            jax.ShapeDtypeStruct((bsz * seq_len, W_RET), BF16),
            jax.ShapeDtypeStruct((bsz, H_RET, DK_RET, DV_RET), F32),
        ],
        scratch_shapes=[pltpu.VMEM((H_RET, DK_RET, DV_RET), F32)],
        compiler_params=_params("parallel", "arbitrary"),
        name="retention_prompt",
    )(ret_in, ret_in, ret_in, ret_in, dm, qd, kd, cd, g_ret)


def _ret_step_kernel(ret_ref, s_ref, qd_ref, cd_ref, g_ref, o_ref, snew_ref):
    g = g_ref[...]
    rows = lax.broadcasted_iota(jnp.int32, (DK_RET, DK_RET), 0)
    cols = lax.broadcasted_iota(jnp.int32, (DK_RET, DK_RET), 1)
    diag = rows == cols
    for hd in range(H_RET):
        lo = hd * DK_RET
        q = ret_ref[:, lo:lo + DK_RET]
        k = ret_ref[:, W_RET + lo:W_RET + lo + DK_RET]
        v = ret_ref[:, 2 * W_RET + lo:2 * W_RET + lo + DK_RET]
        ga = ret_ref[:, 3 * W_RET + lo:3 * W_RET + lo + DK_RET]
        s = s_ref[hd]
        s_hi, s_lo = _hi_lo(s)
        att = jnp.sum(q * k, axis=-1, keepdims=True)
        q8 = jnp.broadcast_to(q * qd_ref[hd], (8, DK_RET))
        o = att * v + _split_dot(q8, s_hi, s_lo)[0:1, :]
        k_diag = jnp.where(diag, jnp.broadcast_to(k, (DK_RET, DK_RET)), 0.0)
        v_rows = jnp.broadcast_to(v, (DK_RET, DV_RET))
        v_hi, v_lo = _hi_lo(v_rows)
        snew_ref[hd] = cd_ref[hd] * s + _split_dot(k_diag, v_hi, v_lo)
        on = _rms(o) * g
        o_ref[:, lo:lo + DK_RET] = on * jax.nn.silu(ga)


def _retention_step(ret_in, state, qd1, cd, g_ret):
    bsz = state.shape[0]
    ret3 = ret_in.reshape(bsz, 1, 4 * W_RET)
    o, s_new = pl.pallas_call(
        _ret_step_kernel,
        grid=(bsz,),
        in_specs=[
            pl.BlockSpec((None, 1, 4 * W_RET), lambda b: (b, 0, 0)),
            pl.BlockSpec((None, H_RET, DK_RET, DV_RET), lambda b: (b, 0, 0, 0)),
            _full_spec(qd1), _full_spec(cd), _full_spec(g_ret),
        ],
        out_specs=[
            pl.BlockSpec((None, 1, W_RET), lambda b: (b, 0, 0)),
            pl.BlockSpec((None, H_RET, DK_RET, DV_RET), lambda b: (b, 0, 0, 0)),
        ],
        out_shape=[
            jax.ShapeDtypeStruct((bsz, 1, W_RET), F32),
            jax.ShapeDtypeStruct((bsz, H_RET, DK_RET, DV_RET), F32),
        ],
        compiler_params=_params("parallel"),
        name="retention_step",
    )(ret3, state, qd1, cd, g_ret)
    return o.reshape(bsz, W_RET), s_new


def _lambda_value(lam_ref, lam_init):
    lv = lam_ref[...]
    s1 = jnp.sum(lv[0:1, :] * lv[1:2, :], axis=-1, keepdims=True)
    s2 = jnp.sum(lv[2:3, :] * lv[3:4, :], axis=-1, keepdims=True)
    return jnp.exp(s1) - jnp.exp(s2) + lam_init


Q_GROUP = 512
LOG2E = 1.4426950408889634


def _diff_kernel(qi_ref, ki_ref, q_ref, k_ref, vt_ref, lam_ref, g_ref, o_ref,
                 qs_scr, k_scr, m_scr, l_scr, acc_scr, *, t, lam_init):
    p = pl.program_id(2)
    qi = qi_ref[p]
    ki = ki_ref[p]
    qg = min(Q_GROUP, t)
    ng = t // qg

    @pl.when(ki == 0)
    def _():
        q = q_ref[...].astype(F32) * (DH_DIFF ** -0.5 * LOG2E)
        lane = lax.broadcasted_iota(jnp.int32, q.shape, 1)
        q_maps = (jnp.where(lane < DH_DIFF, q, 0.0).astype(BF16), jnp.where(lane >= DH_DIFF, q, 0.0).astype(BF16))
        for mp in range(2):
            for gi in range(ng):
                qs_scr[mp * ng + gi] = q_maps[mp][gi * qg:(gi + 1) * qg, :]
        m_scr[...] = jnp.full_like(m_scr, NEG_BIG)
        l_scr[...] = jnp.zeros_like(l_scr)
        acc_scr[...] = jnp.zeros_like(acc_scr)

    k_scr[...] = k_ref[...].astype(BF16)

    def group(g, n_keys, q_off):
        s = _dot_nt(k_scr[0:n_keys, :], qs_scr[g])
        if q_off is not None:
            key = lax.broadcasted_iota(jnp.int32, s.shape, 0)
            qpos = lax.broadcasted_iota(jnp.int32, s.shape, 1) + q_off
            s = jnp.where(key <= qpos, s, NEG_BIG)
        m_prev = m_scr[g]
        m_new = jnp.maximum(m_prev, jnp.max(s, axis=0, keepdims=True))
        alpha = jnp.exp2(m_prev - m_new)
        pr = jnp.exp2(s - m_new)
        l_scr[g] = alpha * l_scr[g] + jnp.sum(pr, axis=0, keepdims=True)
        acc_scr[g] = alpha * acc_scr[g] + _bdot(vt_ref[:, 0:n_keys], pr.astype(BF16))
        m_scr[g] = m_new

    @pl.when(ki < qi)
    def _():
        def body(g, carry):
            group(g, t, None)
            return carry
        lax.fori_loop(0, 2 * ng, body, 0)

    @pl.when(ki == qi)
    def _():
        for g in range(2 * ng):
            q_off = (g % ng) * qg
            group(g, q_off + qg, q_off)
        lam = _lambda_value(lam_ref, lam_init)
        g_col = g_ref[...]
        for c in range(t // LANES):
            g1, lanes = (c * LANES) // qg, slice((c * LANES) % qg, (c * LANES) % qg + LANES)
            a1 = acc_scr[g1][:, lanes] * (1.0 / l_scr[g1][:, lanes])
            a2 = acc_scr[ng + g1][:, lanes] * (1.0 / l_scr[ng + g1][:, lanes])
            o = a1 - lam * a2
            o = o * lax.rsqrt(jnp.mean(o * o, axis=0, keepdims=True) + NORM_EPS) * g_col * (1.0 - lam_init)
            o_ref[c * LANES:(c + 1) * LANES, :] = o.T.astype(BF16)


def _diff_prompt(qb, kb, vt, lam_vec, g_d, bsz, seq_len, lam_init):
    t = min(1024, seq_len)
    nq = seq_len // t
    pairs = [(i, j) for i in range(nq) for j in range(i + 1)]
    qi = jnp.asarray([a for a, _ in pairs], jnp.int32)
    ki = jnp.asarray([b for _, b in pairs], jnp.int32)
    g_col = g_d.reshape(DV_DIFF, 1)
    qg = min(Q_GROUP, t)
    n_grp = 2 * t // qg
    grid_spec = pltpu.PrefetchScalarGridSpec(
        num_scalar_prefetch=2,
        grid=(bsz, H_DIFF, len(pairs)),
        in_specs=[
            pl.BlockSpec((t, 2 * DH_DIFF), lambda b, h, p, qi, ki: (b * nq + qi[p], h)),
            pl.BlockSpec((t, 2 * DH_DIFF), lambda b, h, p, qi, ki: (b * nq + ki[p], h)),
            pl.BlockSpec((DV_DIFF, t), lambda b, h, p, qi, ki: (h, b * nq + ki[p])),
            pl.BlockSpec(lam_vec.shape, lambda b, h, p, qi, ki: (0, 0)),
            pl.BlockSpec(g_col.shape, lambda b, h, p, qi, ki: (0, 0)),
        ],
        out_specs=pl.BlockSpec((t, DV_DIFF), lambda b, h, p, qi, ki: (b * nq + qi[p], h)),
        scratch_shapes=[
            pltpu.VMEM((n_grp, qg, 2 * DH_DIFF), BF16),
            pltpu.VMEM((t, 2 * DH_DIFF), BF16),
            pltpu.VMEM((n_grp, 1, qg), F32),
            pltpu.VMEM((n_grp, 1, qg), F32),
            pltpu.VMEM((n_grp, DV_DIFF, qg), F32),
        ],
    )
    return pl.pallas_call(
        functools.partial(_diff_kernel, t=t, lam_init=lam_init),
        grid_spec=grid_spec,
        out_shape=jax.ShapeDtypeStruct((bsz * seq_len, H_DIFF * DV_DIFF), BF16),
        compiler_params=_params("parallel", "parallel", "arbitrary"),
        name="diff_prompt",
    )(qi, ki, qb, kb, vt, lam_vec, g_col)


PAGES_PER_STEP = 8
N_MAPS = 2 * H_DIFF


def _group_sums(prod):
    return jnp.concatenate(
        [jnp.sum(prod[g * DH_DIFF:(g + 1) * DH_DIFF, :], axis=0, keepdims=True) for g in range(N_MAPS)], axis=0)


def _diff_step_kernel(pt_ref, q_ref, kn_ref, vn_ref, lam_ref, g_ref, *rest, lam_init):
    pps = PAGES_PER_STEP
    k_refs = rest[:pps]
    v_refs = rest[pps:2 * pps]
    o_ref = rest[2 * pps]
    qb_scr, m_scr, l_scr, acc_scr = rest[2 * pps + 1:]
    j = pl.program_id(1)
    scale = DH_DIFF ** -0.5

    @pl.when(j == 0)
    def _():
        qb_scr[...] = jnp.broadcast_to(q_ref[...], (W_DIFF, PAGE_SIZE))
        m_scr[...] = jnp.full_like(m_scr, NEG_BIG)
        l_scr[...] = jnp.zeros_like(l_scr)
        acc_scr[...] = jnp.zeros_like(acc_scr)

    scores = [_group_sums(k_refs[pg][...] * qb_scr[...]) * scale for pg in range(pps)]
    s_max = scores[0]
    for pg in range(1, pps):
        s_max = jnp.maximum(s_max, scores[pg])
    m_prev = m_scr[...]
    m_new = jnp.maximum(m_prev, jnp.max(s_max, axis=-1, keepdims=True))
    alpha = jnp.exp(m_prev - m_new)
    row_head = lax.broadcasted_iota(jnp.int32, (N_MAPS, DV_DIFF), 0) // 2
    p_sum = jnp.zeros((N_MAPS, PAGE_SIZE), F32)
    pv = jnp.zeros((N_MAPS, DV_DIFF), F32)
    for pg in range(pps):
        pr = jnp.exp(scores[pg] - m_new)
        p_sum = p_sum + pr
        p_bits = lax.bitcast_convert_type(pr, jnp.uint32) & jnp.uint32(0xFFFF0000)
        p_top = lax.bitcast_convert_type(p_bits, F32)
        p_hi = p_top.astype(BF16)
        p_both = jnp.concatenate([p_top, pr - p_top], axis=0).astype(BF16)
        for hd in range(H_DIFF):
            v_hi, v_lo = _hi_lo(v_refs[pg][pl.ds(hd, PAGE_SIZE, stride=H_DIFF), :])
            r = _bdot(p_both, v_hi)
            r = r[0:N_MAPS, :] + r[N_MAPS:2 * N_MAPS, :] + _bdot(p_hi, v_lo)
            pv = pv + jnp.where(row_head == hd, r, 0.0)
    l_scr[...] = alpha * l_scr[...] + jnp.sum(p_sum, axis=-1, keepdims=True)
    acc_scr[...] = alpha * acc_scr[...] + pv
    m_scr[...] = m_new

    @pl.when(j == pl.num_programs(1) - 1)
    def _():
        s = _group_sums(q_ref[...] * kn_ref[...]) * scale
        m_old = m_scr[...]
        m_fin = jnp.maximum(m_old, s)
        a_fin = jnp.exp(m_old - m_fin)
        pr = jnp.exp(s - m_fin)
        l_fin = a_fin * l_scr[...] + pr
        a = (a_fin * acc_scr[...] + pr * vn_ref[...]) / l_fin
        lam = _lambda_value(lam_ref, lam_init)
        g = g_ref[...]
        for hd in range(H_DIFF):
            d = a[2 * hd:2 * hd + 1, :] - lam * a[2 * hd + 1:2 * hd + 2, :]
            o_ref[hd:hd + 1, :] = _rms(d) * g * (1.0 - lam_init)


def _diff_step(qb, kb_new, vb_new, cache_k, cache_v, layer_idx, page_table, lam_vec, g_d, lam_init):
    bsz, n_pages = page_table.shape
    n_layers, n_pool = cache_k.shape[0:2]
    pps = PAGES_PER_STEP
    assert n_pages % pps == 0
    ck = jnp.transpose(cache_k, (0, 1, 3, 4, 5, 2)).reshape(n_layers * n_pool, W_DIFF, PAGE_SIZE)
    cv = cache_v.reshape(n_layers * n_pool, PAGE_SIZE * H_DIFF, DV_DIFF)
    page_table = page_table + layer_idx * n_pool
    col_spec = pl.BlockSpec((None, W_DIFF, 1), lambda b, j, pt: (b, 0, 0))
    v_new = jnp.repeat(vb_new.reshape(bsz, H_DIFF, DV_DIFF), 2, axis=1)

    def page_spec(pg):
        return pl.BlockSpec((None, W_DIFF, PAGE_SIZE), lambda b, j, pt, pg=pg: (pt[b, j * pps + pg], 0, 0))

    grid_spec = pltpu.PrefetchScalarGridSpec(
        num_scalar_prefetch=1,
        grid=(bsz, n_pages // pps),
        in_specs=[col_spec, col_spec,
                  pl.BlockSpec((None, N_MAPS, DV_DIFF), lambda b, j, pt: (b, 0, 0)),
                  pl.BlockSpec(lam_vec.shape, lambda b, j, pt: (0, 0)),
                  pl.BlockSpec(g_d.shape, lambda b, j, pt: (0, 0))]
                 + [page_spec(pg) for pg in range(pps)] + [page_spec(pg) for pg in range(pps)],
        out_specs=pl.BlockSpec((None, H_DIFF, DV_DIFF), lambda b, j, pt: (b, 0, 0)),
        scratch_shapes=[
            pltpu.VMEM((W_DIFF, PAGE_SIZE), F32),
            pltpu.VMEM((N_MAPS, 1), F32),
            pltpu.VMEM((N_MAPS, 1), F32),
            pltpu.VMEM((N_MAPS, DV_DIFF), F32),
        ],
    )
    o = pl.pallas_call(
        functools.partial(_diff_step_kernel, lam_init=lam_init),
        grid_spec=grid_spec,
        out_shape=jax.ShapeDtypeStruct((bsz, H_DIFF, DV_DIFF), F32),
        compiler_params=_params("parallel", "arbitrary"),
        name="diff_step",
    )(page_table, qb.reshape(bsz, W_DIFF, 1), kb_new.reshape(bsz, W_DIFF, 1), v_new, lam_vec, g_d,
      *([ck] * pps), *([cv] * pps))
    return o.reshape(bsz, W_DIFF)


def _out_even_kernel(a_ref, b_ref, x_ref, gt_ref, *rest):
    w_refs = rest[:-1]
    o_ref = rest[-1]
    m = _mm(a_ref[...], w_refs, slice(0, W_RET)) + _mm(b_ref[...], w_refs, slice(W_RET, 2 * W_RET))
    o_ref[...] = x_ref[...] + gt_ref[...] * m


def _out_even(o_ret, o_diff, ws, x2d, gt, tm, tiles_per_seq):
    m = x2d.shape[0]
    row = lambda n: pl.BlockSpec((tm, n), lambda i: (i, 0))
    return pl.pallas_call(
        _out_even_kernel,
        grid=(m // tm,),
        in_specs=[row(W_RET), row(W_RET), row(D_MODEL), _vec_spec(gt, tm, tiles_per_seq)]
                 + [_full_spec(w) for w in ws],
        out_specs=row(D_MODEL),
        out_shape=jax.ShapeDtypeStruct((m, D_MODEL), F32),
        compiler_params=_params("parallel"),
        name="out_even",
    )(o_ret, o_diff, x2d, gt, *ws)


def _ffn_kernel(x_ref, g_ref, sh_ref, sc_ref, gt_ref, *rest, n_w):
    w1, w3, w2 = rest[0:n_w], rest[n_w:2 * n_w], rest[2 * n_w:3 * n_w]
    o_ref = rest[3 * n_w]
    x = x_ref[...]
    h = _modulate(x, g_ref[...], sh_ref[...], sc_ref[...])
    if n_w == 1:
        h = h.astype(BF16)
    u = jax.nn.silu(_mm(h, w1)) * _mm(h, w3)
    o_ref[...] = x + gt_ref[...] * _mm(u, w2)


def _ffn(x2d, g, sh, sc, gt, w1s, w3s, w2s, tm, tiles_per_seq):
    m = x2d.shape[0]
    row = pl.BlockSpec((tm, D_MODEL), lambda i: (i, 0))
    vs = lambda a: _vec_spec(a, tm, tiles_per_seq)
    ws = (*w1s, *w3s, *w2s)
    return pl.pallas_call(
        functools.partial(_ffn_kernel, n_w=len(w1s)),
        grid=(m // tm,),
        in_specs=[row, _full_spec(g), vs(sh), vs(sc), vs(gt)] + [_full_spec(w) for w in ws],
        out_specs=row,
        out_shape=jax.ShapeDtypeStruct((m, D_MODEL), F32),
        compiler_params=_params("parallel"),
        name="ffn",
    )(x2d, g, sh, sc, gt, *ws)


def _modmm_kernel(x_ref, g_ref, sh_ref, sc_ref, *rest):
    w_refs = rest[:-1]
    o_ref = rest[-1]
    o_ref[...] = _mm(_modulate(x_ref[...], g_ref[...], sh_ref[...], sc_ref[...]), w_refs)


def _modmm(x2d, g, sh, sc, ws, tm, tiles_per_seq):
    m = x2d.shape[0]
    n = ws[0].shape[1]
    vs = lambda a: _vec_spec(a, tm, tiles_per_seq)
    return pl.pallas_call(
        _modmm_kernel,
        grid=(m // tm,),
        in_specs=[pl.BlockSpec((tm, D_MODEL), lambda i: (i, 0)), _full_spec(g), vs(sh), vs(sc)]
                 + [_full_spec(w) for w in ws],
        out_specs=pl.BlockSpec((tm, n), lambda i: (i, 0)),
        out_shape=jax.ShapeDtypeStruct((m, n), F32),
        compiler_params=_params("parallel"),
        name="modmm",
    )(x2d, g, sh, sc, *ws)


def _mm_at(x, w_refs, idx):
    if len(w_refs) == 1:
        return _bdot(x.astype(BF16), w_refs[0][idx])
    return _split_dot(x, w_refs[0][idx], w_refs[1][idx])


def _s5_input_drive(u, bmat_refs, j):
    return _mm_at(u[:, j * LANES:(j + 1) * LANES], bmat_refs, j)


def _s5_output_block(h_re_of, h_im_of, cmat_refs, jb):
    acc = None
    for q in range(4):
        r = 4 * jb + q
        t = (_mm_at(h_re_of(r), cmat_refs, (r, slice(0, LANES), slice(None)))
             + _mm_at(h_im_of(r), cmat_refs, (r, slice(LANES, 2 * LANES), slice(None))))
        acc = t if acc is None else acc + t
    return acc


def _s5_prompt_kernel(u_ref, are_ref, aim_ref, bmat_ref, cmat_ref, d_ref, z_ref, hre_ref, him_ref,
                      sre, sim, st_re, st_im, *, tc):
    j = pl.program_id(1)

    @pl.when(j == 0)
    def _():
        st_re[...] = jnp.zeros_like(st_re)
        st_im[...] = jnp.zeros_like(st_im)

    u = u_ref[...]
    half = 4 * LANES
    for jb in range(8):
        res = _s5_input_drive(u, (bmat_ref,), jb)
        for q in range(4):
            r = 4 * jb + q
            sre[pl.ds(r, tc, stride=STEP_PITCH), :] = res[:, q * LANES:(q + 1) * LANES]
            sim[pl.ds(r, tc, stride=STEP_PITCH), :] = res[:, half + q * LANES:half + (q + 1) * LANES]

    a_re = are_ref[...]
    a_im = aim_ref[...]

    def step(t, carry):
        h_re, h_im = carry
        base = pl.multiple_of(t * STEP_PITCH, 8)
        n_re = a_re * h_re - a_im * h_im + sre[pl.ds(base, N_PAIRS), :]
        n_im = a_re * h_im + a_im * h_re + sim[pl.ds(base, N_PAIRS), :]
        sre[pl.ds(base, N_PAIRS), :] = n_re
        sim[pl.ds(base, N_PAIRS), :] = n_im
        return n_re, n_im

    h_re, h_im = lax.fori_loop(0, tc, step, (st_re[...], st_im[...]))
    st_re[...] = h_re
    st_im[...] = h_im

    for jb in range(8):
        y = _s5_output_block(lambda r: sre[pl.ds(r, tc, stride=STEP_PITCH), :],
                             lambda r: sim[pl.ds(r, tc, stride=STEP_PITCH), :], (cmat_ref,), jb)
        sl = slice(jb * LANES, (jb + 1) * LANES)
        y = y + d_ref[:, sl] * u[:, sl]
        z_ref[:, sl] = jax.nn.gelu(y, approximate=True).astype(BF16)

    @pl.when(j == pl.num_programs(1) - 1)
    def _():
        hre_ref[...] = h_re
        him_ref[...] = h_im


def _s5_prompt(u2d, bsz, seq_len, a_re, a_im, bmat, cmat, d_row):
    tc = min(256, seq_len)
    nj = seq_len // tc
    state_spec = pl.BlockSpec((None, N_PAIRS, LANES), lambda b, j: (b, 0, 0))
    return pl.pallas_call(
        functools.partial(_s5_prompt_kernel, tc=tc),
        grid=(bsz, nj),
        in_specs=[pl.BlockSpec((tc, D_MODEL), lambda b, j: (b * nj + j, 0)),
                  _full_spec(a_re), _full_spec(a_im), _full_spec(bmat), _full_spec(cmat), _full_spec(d_row)],
        out_specs=[pl.BlockSpec((tc, D_MODEL), lambda b, j: (b * nj + j, 0)), state_spec, state_spec],
        out_shape=[
            jax.ShapeDtypeStruct((bsz * seq_len, D_MODEL), BF16),
            jax.ShapeDtypeStruct((bsz, N_PAIRS, LANES), F32),
            jax.ShapeDtypeStruct((bsz, N_PAIRS, LANES), F32),
        ],
        scratch_shapes=[
            pltpu.VMEM((tc * STEP_PITCH, LANES), F32),
            pltpu.VMEM((tc * STEP_PITCH, LANES), F32),
            pltpu.VMEM((N_PAIRS, LANES), F32),
            pltpu.VMEM((N_PAIRS, LANES), F32),
        ],
        compiler_params=_params("parallel", "arbitrary"),
        name="s5_prompt",
    )(u2d, a_re, a_im, bmat, cmat, d_row)


def _s5_step_kernel(u_ref, h0re_ref, h0im_ref, are_ref, aim_ref, bhi_ref, blo_ref, chi_ref, clo_ref, d_ref,
                    z_ref, hre_ref, him_ref):
    u = u_ref[...]
    half = 4 * LANES
    bmat_refs = (bhi_ref, blo_ref)
    cmat_refs = (chi_ref, clo_ref)
    for jb in range(8):
        res = _s5_input_drive(u, bmat_refs, jb)
        sl = slice(jb * half, (jb + 1) * half)
        a_re = are_ref[:, sl]
        a_im = aim_ref[:, sl]
        h_re = h0re_ref[:, sl]
        h_im = h0im_ref[:, sl]
        hre_ref[:, sl] = a_re * h_re - a_im * h_im + res[:, 0:half]
        him_ref[:, sl] = a_re * h_im + a_im * h_re + res[:, half:2 * half]
    for jb in range(8):
        y = _s5_output_block(lambda r: hre_ref[:, r * LANES:(r + 1) * LANES],
                             lambda r: him_ref[:, r * LANES:(r + 1) * LANES], cmat_refs, jb)
        sl = slice(jb * LANES, (jb + 1) * LANES)
        y = y + d_ref[:, sl] * u[:, sl]
        z_ref[:, sl] = jax.nn.gelu(y, approximate=True)


def _s5_step(u2d, h0_re, h0_im, a_re_row, a_im_row, bmats, cmats, d_row):
    bsz = u2d.shape[0]
    n_state = N_GROUPS * SSM_STATE
    args = (u2d, h0_re, h0_im, a_re_row, a_im_row, *bmats, *cmats, d_row)
    return pl.pallas_call(
        _s5_step_kernel,
        grid=(1,),
        in_specs=[_full_spec(a) for a in args],
        out_specs=[pl.BlockSpec((bsz, D_MODEL), lambda i: (0, 0)),
                   pl.BlockSpec((bsz, n_state), lambda i: (0, 0)),
                   pl.BlockSpec((bsz, n_state), lambda i: (0, 0))],
        out_shape=[
            jax.ShapeDtypeStruct((bsz, D_MODEL), F32),
            jax.ShapeDtypeStruct((bsz, n_state), F32),
            jax.ShapeDtypeStruct((bsz, n_state), F32),
        ],
        compiler_params=_params("arbitrary"),
        name="s5_step",
    )(*args)


def _glu_router_kernel(z_ref, x_ref, gt_ref, g_ref, sh_ref, sc_ref, wrh_ref, wrl_ref, br_ref, *rest):
    wg_refs = rest[:-3]
    x_out, h_out, route_out = rest[-3:]
    vg = _mm(z_ref[...], wg_refs)
    val = vg[:, 0:D_MODEL]
    gate = vg[:, D_MODEL:2 * D_MODEL]
    x = x_ref[...] + gt_ref[...] * (val * jax.nn.sigmoid(gate))
    x_out[...] = x
    h = _modulate(x, g_ref[...], sh_ref[...], sc_ref[...])
    h_out[...] = h
    logits = _split_dot(h, wrh_ref[...], wrl_ref[...]) + br_ref[...]
    lane = lax.broadcasted_iota(jnp.int32, logits.shape, 1)
    v1 = jnp.max(logits, axis=-1, keepdims=True)
    i1 = jnp.min(jnp.where(logits == v1, lane, LANES), axis=-1, keepdims=True)
    others = jnp.where(lane == i1, NEG_BIG, logits)
    v2 = jnp.max(others, axis=-1, keepdims=True)
    i2 = jnp.min(jnp.where(others == v2, lane, LANES), axis=-1, keepdims=True)
    e = jnp.exp(v2 - v1)
    g1 = 1.0 / (1.0 + e)
    g2 = e / (1.0 + e)
    route = jnp.where(lane == i1, g1, 0.0) + jnp.where(lane == i2, g2, 0.0)
    flag = jnp.logical_or(lane == i1 + N_EXPERTS, lane == i2 + N_EXPERTS)
    route_out[...] = route + jnp.where(flag, 1.0, 0.0)


def _glu_router(z, wgs, x2d, gt, g, sh, sc, wr_hi, wr_lo, br, tm, tiles_per_seq):
    m = x2d.shape[0]
    row = lambda n: pl.BlockSpec((tm, n), lambda i: (i, 0))
    vs = lambda a: _vec_spec(a, tm, tiles_per_seq)
    return pl.pallas_call(
        _glu_router_kernel,
        grid=(m // tm,),
        in_specs=[row(D_MODEL), row(D_MODEL), vs(gt), _full_spec(g), vs(sh), vs(sc),
                  _full_spec(wr_hi), _full_spec(wr_lo), _full_spec(br)] + [_full_spec(w) for w in wgs],
        out_specs=[row(D_MODEL), row(D_MODEL), row(LANES)],
        out_shape=[
            jax.ShapeDtypeStruct((m, D_MODEL), F32),
            jax.ShapeDtypeStruct((m, D_MODEL), F32),
            jax.ShapeDtypeStruct((m, LANES), F32),
        ],
        compiler_params=_params("parallel"),
        name="glu_router",
    )(z, x2d, gt, g, sh, sc, wr_hi, wr_lo, br, *wgs)


def _moe_kernel(te_ref, nv_ref, xs_ref, w1_ref, w3_ref, w2_ref, o_ref, x_scr, acc_scr):
    i = pl.program_id(0)
    f = pl.program_id(1)
    nf = pl.num_programs(1)
    valid = i < nv_ref[0]

    @pl.when(jnp.logical_and(valid, f == 0))
    def _():
        x_scr[...] = xs_ref[...].astype(BF16)

    @pl.when(valid)
    def _():
        x = x_scr[...]
        a = _bdot(x, w1_ref[...].astype(BF16))
        b = _bdot(x, w3_ref[...].astype(BF16))
        u = (jax.nn.silu(a) * b).astype(BF16)
        contrib = _bdot(u, w2_ref[...].astype(BF16))

        @pl.when(f == 0)
        def _():
            acc_scr[...] = contrib

        @pl.when(f > 0)
        def _():
            acc_scr[...] = acc_scr[...] + contrib

        @pl.when(f == nf - 1)
        def _():
            o_ref[...] = acc_scr[...]

    @pl.when(jnp.logical_and(jnp.logical_not(valid), f == nf - 1))
    def _():
        o_ref[...] = jnp.zeros_like(o_ref)


def _moe_experts(xs, tile_expert, n_valid, w1, w3, w2, tm):
    rows = xs.shape[0]
    n_tiles = rows // tm
    tf = 512
    nf = D_FF_EXPERT // tf

    def fsel(i, f, nv):
        return jnp.where(i < nv[0], f, nf - 1)

    grid_spec = pltpu.PrefetchScalarGridSpec(
        num_scalar_prefetch=2,
        grid=(n_tiles, nf),
        in_specs=[
            pl.BlockSpec((tm, D_MODEL), lambda i, f, te, nv: (i, 0)),
            pl.BlockSpec((None, D_MODEL, tf), lambda i, f, te, nv: (te[i], 0, fsel(i, f, nv))),
            pl.BlockSpec((None, D_MODEL, tf), lambda i, f, te, nv: (te[i], 0, fsel(i, f, nv))),
            pl.BlockSpec((None, tf, D_MODEL), lambda i, f, te, nv: (te[i], fsel(i, f, nv), 0)),
        ],
        out_specs=pl.BlockSpec((tm, D_MODEL), lambda i, f, te, nv: (i, 0)),
        scratch_shapes=[pltpu.VMEM((tm, D_MODEL), BF16), pltpu.VMEM((tm, D_MODEL), F32)],
    )
    return pl.pallas_call(
        _moe_kernel,
        grid_spec=grid_spec,
        out_shape=jax.ShapeDtypeStruct((rows, D_MODEL), F32),
        compiler_params=_params("parallel", "arbitrary"),
        name="moe_experts",
    )(tile_expert, n_valid, xs, w1, w3, w2)


def _row_params(*sem):
    return pltpu.CompilerParams(dimension_semantics=sem, vmem_limit_bytes=VMEM_LIMIT_BYTES,
                                disable_bounds_checks=True)


def _dispatch_kernel(sa_ref, sb_ref, h_ref, xs_in_ref, xs_ref, sem, *, tm):
    del xs_in_ref
    base = pl.program_id(0) * tm

    def copies(t):
        tok = base + t
        src = h_ref.at[pl.ds(t, 1)]
        return (pltpu.make_async_copy(src, xs_ref.at[pl.ds(sa_ref[tok], 1)], sem.at[0]),
                pltpu.make_async_copy(src, xs_ref.at[pl.ds(sb_ref[tok], 1)], sem.at[1]))

    def issue(t, carry):
        for cp in copies(t):
            cp.start()
        return carry

    def drain(t, carry):
        for cp in copies(t):
            cp.wait()
        return carry

    lax.fori_loop(0, tm, issue, 0)
    lax.fori_loop(0, tm, drain, 0)


def _moe_dispatch(h, slot_a, slot_b, n_rows, tm):
    m = h.shape[0]
    grid_spec = pltpu.PrefetchScalarGridSpec(
        num_scalar_prefetch=2,
        grid=(m // tm,),
        in_specs=[pl.BlockSpec((tm, D_MODEL), lambda i, sa, sb: (i, 0)), pl.BlockSpec(memory_space=pl.ANY)],
        out_specs=pl.BlockSpec(memory_space=pl.ANY),
        scratch_shapes=[pltpu.SemaphoreType.DMA((2,))],
    )
    return pl.pallas_call(
        functools.partial(_dispatch_kernel, tm=tm),
        grid_spec=grid_spec,
        out_shape=jax.ShapeDtypeStruct((n_rows, D_MODEL), F32),
        input_output_aliases={3: 0},
        compiler_params=_row_params("arbitrary"),
        name="moe_dispatch",
    )(slot_a, slot_b, h, jnp.zeros((n_rows, D_MODEL), F32))


def _combine_kernel(sa_ref, sb_ref, x_ref, gt_ref, ga_ref, gb_ref, ys_ref, o_ref, buf_a, buf_b, sem, *, tm):
    base = pl.program_id(0) * tm

    def copies(t):
        tok = base + t
        return (pltpu.make_async_copy(ys_ref.at[pl.ds(sa_ref[tok], 1)], buf_a.at[pl.ds(t, 1)], sem.at[0]),
                pltpu.make_async_copy(ys_ref.at[pl.ds(sb_ref[tok], 1)], buf_b.at[pl.ds(t, 1)], sem.at[1]))

    def issue(t, carry):
        for cp in copies(t):
            cp.start()
        return carry

    def drain(t, carry):
        for cp in copies(t):
            cp.wait()
        return carry

    lax.fori_loop(0, tm, issue, 0)
    lax.fori_loop(0, tm, drain, 0)
    o_ref[...] = x_ref[...] + gt_ref[...] * (ga_ref[...] * buf_a[...] + gb_ref[...] * buf_b[...])


def _moe_combine(x2d, gt, gate_a, gate_b, ys, slot_a, slot_b, tm, tiles_per_seq):
    m = x2d.shape[0]
    row = pl.BlockSpec((tm, D_MODEL), lambda i, sa, sb: (i, 0))
    col = pl.BlockSpec((tm, 1), lambda i, sa, sb: (i, 0))
    grid_spec = pltpu.PrefetchScalarGridSpec(
        num_scalar_prefetch=2,
        grid=(m // tm,),
        in_specs=[row, _vec_spec(gt, tm, tiles_per_seq), col, col, pl.BlockSpec(memory_space=pl.ANY)],
        out_specs=row,
        scratch_shapes=[pltpu.VMEM((tm, D_MODEL), F32), pltpu.VMEM((tm, D_MODEL), F32),
                        pltpu.SemaphoreType.DMA((2,))],
    )
    return pl.pallas_call(
        functools.partial(_combine_kernel, tm=tm),
        grid_spec=grid_spec,
        out_shape=jax.ShapeDtypeStruct((m, D_MODEL), F32),
        compiler_params=_row_params("arbitrary"),
        name="moe_combine",
    )(slot_a, slot_b, x2d, gt, gate_a, gate_b, ys)


def _moe(h, route, x2d, gt, w1, w3, w2, tm_rows, tm, tiles_per_seq):
    m = h.shape[0]
    comb = route[:, 0:N_EXPERTS]
    sel = route[:, N_EXPERTS:2 * N_EXPERTS] > 0.5
    seli = sel.astype(jnp.int32)
    counts = jnp.sum(seli, axis=0)
    padded = ((counts + tm_rows - 1) // tm_rows) * tm_rows
    ends = jnp.cumsum(padded)
    offsets = ends - padded
    rank = jnp.cumsum(seli, axis=0) - 1
    pos = offsets[None, :] + rank
    n_rows = ((2 * m + N_EXPERTS * (tm_rows - 1)) // tm_rows + 1) * tm_rows
    n_tiles = n_rows // tm_rows
    tile_start = jnp.arange(n_tiles, dtype=jnp.int32) * tm_rows
    n_valid = (ends[-1] // tm_rows).astype(jnp.int32)
    first_row = jnp.minimum(tile_start, ends[-1] - 1)
    tile_expert = jnp.sum((ends[None, :] <= first_row[:, None]).astype(jnp.int32), axis=1)
    slot_a = jnp.min(jnp.where(sel, pos, n_rows), axis=1).astype(jnp.int32)
    slot_b = jnp.max(jnp.where(sel, pos, -1), axis=1).astype(jnp.int32)
    expert_id = jnp.arange(N_EXPERTS, dtype=jnp.int32)[None, :]
    e_lo = jnp.min(jnp.where(sel, expert_id, N_EXPERTS), axis=1, keepdims=True)
    e_hi = jnp.max(jnp.where(sel, expert_id, -1), axis=1, keepdims=True)
    gate_a = jnp.sum(jnp.where(expert_id == e_lo, comb, 0.0), axis=1, keepdims=True)
    gate_b = jnp.sum(jnp.where(expert_id == e_hi, comb, 0.0), axis=1, keepdims=True)
    tm_dispatch = min(1024, m)
    xs = _moe_dispatch(h, slot_a, slot_b, n_rows, tm_dispatch)
    ys = _moe_experts(xs, tile_expert, n_valid.reshape(1), w1, w3, w2, tm_rows)
    return _moe_combine(x2d, gt, gate_a, gate_b, ys, slot_a, slot_b, tm, tiles_per_seq)


def _retention_constants(chunk):
    log_g = jnp.log1p(-jnp.exp2(-5.0 - jnp.arange(H_RET, dtype=F32)))
    idx = jnp.arange(chunk, dtype=F32)
    rel = idx[:, None] - idx[None, :]
    dm = jnp.where(rel >= 0, jnp.exp(log_g[:, None, None] * jnp.maximum(rel, 0.0)), 0.0)
    q_decay = jnp.exp(log_g[:, None] * (idx[None, :] + 1.0))
    k_decay = jnp.exp(log_g[:, None] * (chunk - 1.0 - idx[None, :]))
    chunk_decay = jnp.exp(log_g * chunk)
    qd = jnp.broadcast_to(q_decay[:, :, None], (H_RET, chunk, DK_RET))
    kd = jnp.broadcast_to(k_decay[:, :, None], (H_RET, chunk, DK_RET))
    cd = jnp.broadcast_to(chunk_decay[:, None, None], (H_RET, 1, DV_RET))
    return dm, qd, kd, cd


def _rotary_tables(pos):
    half = DK_RET // 2
    inv = np.float64(ROPE_BASE) ** (-np.arange(half, dtype=np.float64) / half)
    inv_hi = inv.astype(np.float32)
    inv_lo = (inv - inv_hi.astype(np.float64)).astype(np.float32)
    p = jnp.asarray(pos).astype(F32)[:, None]
    ang = p * jnp.asarray(inv_hi)[None, :] + p * jnp.asarray(inv_lo)[None, :]
    cos = jnp.cos(ang)
    sin = jnp.sin(ang)
    return jnp.concatenate([cos, cos], axis=-1), jnp.concatenate([-sin, sin], axis=-1)


def _s5_matrices(lam_re, lam_im, log_step, b_re, b_im, c_re, c_im, d_skip):
    lr = lam_re.astype(F32)
    li = lam_im.astype(F32)
    step = jnp.exp(log_step.astype(F32))[:, None]
    mag = jnp.exp(lr * step)
    ab_re = mag * jnp.cos(li * step)
    ab_im = mag * jnp.sin(li * step)
    nr = ab_re - 1.0
    ni = ab_im
    den = lr * lr + li * li
    kr = (nr * lr + ni * li) / den
    ki = (ni * lr - nr * li) / den
    br = b_re.astype(F32)
    bi = b_im.astype(F32)
    bb_re = kr[..., None] * br - ki[..., None] * bi
    bb_im = kr[..., None] * bi + ki[..., None] * br
    eye8 = jnp.eye(8, dtype=F32)

    def drive(bb):
        t = bb.reshape(8, 8, SSM_STATE, SSM_GROUP).transpose(0, 1, 3, 2)
        t = t[:, :, :, None, :] * eye8[None, :, None, :, None]
        return t.reshape(8, LANES, 8 * SSM_STATE)

    bmat = jnp.concatenate([drive(bb_re), drive(bb_im)], axis=-1)
    r_idx = jnp.arange(N_PAIRS)[:, None, None]
    gl_idx = jnp.arange(2)[None, :, None]
    gb_idx = jnp.arange(8)[None, None, :]
    onehot = (gb_idx == 2 * (r_idx % 4) + gl_idx).astype(F32)

    def readout(c):
        t = c.astype(F32).reshape(N_PAIRS, 2, SSM_GROUP, SSM_STATE).transpose(0, 1, 3, 2)
        t = t[:, :, :, None, :] * onehot[:, :, None, :, None]
        return t.reshape(N_PAIRS, LANES, LANES)

    cmat = jnp.concatenate([readout(c_re), -readout(c_im)], axis=1)
    return (ab_re.reshape(N_PAIRS, LANES), ab_im.reshape(N_PAIRS, LANES), bmat, cmat,
            d_skip.astype(F32).reshape(1, D_MODEL))


def _trunk(x, mods, pos, ret0, ssm0, paged, p):
    bsz, seq_len, _ = x.shape
    m = bsz * seq_len
    prompt = seq_len > 1
    depth = p["w_ada"].shape[0]
    x2d = x.reshape(m, D_MODEL)
    if prompt:
        tm = min(512, seq_len)
        tps = seq_len // tm
        tm_ffn = min(256, seq_len)
        tps_ffn = seq_len // tm_ffn
        vec = lambda v: v.reshape(bsz, 1, D_MODEL)
    else:
        tm = tm_ffn = m
        tps = tps_ffn = 1
        vec = lambda v: v.reshape(1, bsz, D_MODEL)
    row = lambda v: v.reshape(1, -1).astype(F32)
    wsp = lambda w: _wsplit(w, not prompt)
    rets, ks_new, vs_new, srs, sis = [], [], [], [], []
    for layer in range(depth):
        sh1, sc1, gt1, sh2, sc2, gt2 = [vec(v) for v in jnp.split(mods[layer], N_ADA, axis=-1)]
        i = layer // 2
        if layer % 2 == 0:
            lam_init = 0.8 - 0.6 * math.exp(-0.3 * layer)
            cos_t, sin_t = _rotary_tables(pos)
            seg = (jnp.arange(W_DIFF)[:, None] // DH_DIFF == jnp.arange(W_DIFF)[None, :] // DH_DIFF).astype(BF16)
            gq = jnp.tile(row(p["g_qnorm"][i]), (1, W_DIFF // DH_DIFF))
            gk = jnp.tile(row(p["g_knorm"][i]), (1, W_DIFF // DH_DIFF))
            ret_in, qb, kb, vb, *vt = _in_even(
                x2d, row(p["g_norm1"][layer]), sh1, sc1, cos_t, sin_t, wsp(p["w_in_mix"][i]),
                gq, gk, seg, tm, tps, BF16 if prompt else F32, prompt)
            lam_vec = jnp.stack([p["lam_q1"][i], p["lam_k1"][i], p["lam_q2"][i], p["lam_k2"][i]]).astype(F32)
            g_ret = row(p["g_ret"][i])
            g_d = row(p["g_dnorm"][i])
            if prompt:
                dm, qd, kd, cd = _retention_constants(math.gcd(seq_len, RET_CHUNK))
                o_ret, s_new = _retention_prompt(ret_in, bsz, seq_len, dm, qd, kd, cd, g_ret)
                o_diff = _diff_prompt(qb, kb, vt[0], lam_vec, g_d, bsz, seq_len, lam_init)
            else:
                _, qd, _, cd = _retention_constants(1)
                o_ret, s_new = _retention_step(ret_in, ret0[i].astype(F32), qd, cd, g_ret)
                o_diff = _diff_step(qb, kb, vb, paged["cache_k"], paged["cache_v"], i,
                                    paged["page_table"], lam_vec, g_d, lam_init)
            rets.append(s_new)
            ks_new.append(kb.reshape(bsz, seq_len, H_DIFF, 2, DH_DIFF))
            vs_new.append(vb.reshape(bsz, seq_len, H_DIFF, DV_DIFF))
            x2d = _out_even(o_ret, o_diff, wsp(p["w_out_mix"][i]), x2d, gt1, tm, tps)
            x2d = _ffn(x2d, row(p["g_norm2"][layer]), sh2, sc2, gt2, wsp(p["w_ffn1"][i]),
                       wsp(p["w_ffn3"][i]), wsp(p["w_ffn2"][i]), tm_ffn, tps_ffn)
        else:
            a_re, a_im, bmat, cmat, d_row = _s5_matrices(
                p["lam_re"][i], p["lam_im"][i], p["log_step"][i], p["b_re"][i], p["b_im"][i],
                p["c_re"][i], p["c_im"][i], p["d_skip"][i])
            u2d = _modmm(x2d, row(p["g_norm1"][layer]), sh1, sc1, wsp(p["w_ssm_in"][i]), tm, tps)
            n_state = N_GROUPS * SSM_STATE
            if prompt:
                z, h_re, h_im = _s5_prompt(u2d, bsz, seq_len, a_re, a_im, bmat.astype(BF16), cmat.astype(BF16),
                                           d_row)
            else:
                z, h_re, h_im = _s5_step(
                    u2d, ssm0[0][i].astype(F32).reshape(bsz, n_state), ssm0[1][i].astype(F32).reshape(bsz, n_state),
                    a_re.reshape(1, n_state), a_im.reshape(1, n_state), wsp(bmat), wsp(cmat), d_row)
            srs.append(h_re.reshape(bsz, N_GROUPS, SSM_STATE))
            sis.append(h_im.reshape(bsz, N_GROUPS, SSM_STATE))
            wr = jnp.zeros((D_MODEL, LANES), F32).at[:, 0:N_EXPERTS].set(p["w_router"][i].astype(F32))
            wr_hi, wr_lo = _wsplit(wr, True)
            br = jnp.full((1, LANES), NEG_BIG, F32).at[0, 0:N_EXPERTS].set(p["b_router"][i].astype(F32))
            x2d, h_mod, route = _glu_router(
                z, wsp(p["w_glu"][i]), x2d, gt1, row(p["g_norm2"][layer]), sh2, sc2,
                wr_hi, wr_lo, br, tm_ffn, tps_ffn)
            x2d = _moe(h_mod, route, x2d, gt2, p["w_moe1"][i], p["w_moe3"][i], p["w_moe2"][i],
                       512 if prompt else 32, tm, tps)
    return (x2d.reshape(bsz, seq_len, D_MODEL), jnp.stack(rets), jnp.stack(ks_new), jnp.stack(vs_new),
            jnp.stack(srs), jnp.stack(sis))


def kernel(x_prompt, x_sample, state_ret, cache_k, cache_v, state_ssm_re, state_ssm_im, page_table, c_prompt, c_sample, w_ada, b_ada, g_norm1, g_norm2, w_in_mix, w_out_mix, g_ret, g_qnorm, g_knorm, g_dnorm, lam_q1, lam_k1, lam_q2, lam_k2, w_ffn1, w_ffn3, w_ffn2, w_ssm_in, lam_re, lam_im, log_step, b_re, b_im, c_re, c_im, d_skip, w_glu, w_router, b_router, w_moe1, w_moe3, w_moe2):
    p = dict(w_ada=w_ada, g_norm1=g_norm1, g_norm2=g_norm2, w_in_mix=w_in_mix, w_out_mix=w_out_mix,
             g_ret=g_ret, g_qnorm=g_qnorm, g_knorm=g_knorm, g_dnorm=g_dnorm, lam_q1=lam_q1, lam_k1=lam_k1,
             lam_q2=lam_q2, lam_k2=lam_k2, w_ffn1=w_ffn1, w_ffn3=w_ffn3, w_ffn2=w_ffn2, w_ssm_in=w_ssm_in,
             lam_re=lam_re, lam_im=lam_im, log_step=log_step, b_re=b_re, b_im=b_im, c_re=c_re, c_im=c_im,
             d_skip=d_skip, w_glu=w_glu, w_router=w_router, b_router=b_router,
             w_moe1=w_moe1, w_moe3=w_moe3, w_moe2=w_moe2)
    n_prompt = c_prompt.shape[0]
    n_sample = c_sample.shape[0]
    n_all = n_prompt + n_sample
    rows = ((n_all + 7) // 8) * 8
    c_all = jnp.zeros((rows, D_MODEL), F32).at[0:n_prompt].set(c_prompt).at[n_prompt:n_all].set(c_sample)
    mods = _ada(c_all, w_ada, b_ada)
    seq_len = x_prompt.shape[1]
    past_len = page_table.shape[1] * PAGE_SIZE
    y_p, ret_p, k_p, v_p, sr_p, si_p = _trunk(
        x_prompt, mods[:, 0:n_prompt], np.arange(seq_len), None, None, None, p)
    paged = dict(cache_k=cache_k, cache_v=cache_v, page_table=page_table)
    y_s, ret_s, k_s, v_s, sr_s, si_s = _trunk(
        x_sample, mods[:, n_prompt:n_all], past_len + np.arange(x_sample.shape[1]),
        state_ret, (state_ssm_re, state_ssm_im), paged, p)
    return (y_p, y_s, ret_p, ret_s, k_p, v_p, k_s, v_s, sr_p, si_p, sr_s, si_s)
```

```python
import functools
import math

import jax
import jax.numpy as jnp
import numpy as np
from jax import lax
from jax.experimental import pallas as pl
from jax.experimental.pallas import tpu as pltpu

F32 = jnp.float32
BF16 = jnp.bfloat16

D_MODEL = 1024
H_RET = 4
DK_RET = 128
DV_RET = 128
RET_CHUNK = 128
ROPE_BASE = 10000.0
H_DIFF = 4
DH_DIFF = 64
DV_DIFF = 128
PAGE_SIZE = 128
W_RET = H_RET * DK_RET
W_DIFF = H_DIFF * 2 * DH_DIFF
IN_EVEN = 4 * W_RET + 3 * W_DIFF
SSM_GROUP = 16
N_GROUPS = D_MODEL // SSM_GROUP
SSM_STATE = 64
N_PAIRS = N_GROUPS // 2
STEP_PITCH = N_PAIRS + 8
D_FF = 2816
N_EXPERTS = 8
D_FF_EXPERT = 3584
N_ADA = 6
NORM_EPS = 1e-6
LANES = 128
NEG_BIG = -1e30

VMEM_LIMIT_BYTES = 56 * 1024 * 1024


def _params(*sem):
    return pltpu.CompilerParams(dimension_semantics=sem, vmem_limit_bytes=VMEM_LIMIT_BYTES)


def _bdot(a, b):
    return jnp.dot(a, b, preferred_element_type=F32)


def _dot_nt(a, b):
    return lax.dot_general(a, b, (((1,), (1,)), ((), ())), preferred_element_type=F32)


def _dot_tn(a, b):
    return lax.dot_general(a, b, (((0,), (0,)), ((), ())), preferred_element_type=F32)


def _rms(x):
    return x * lax.rsqrt(jnp.mean(x * x, axis=-1, keepdims=True) + NORM_EPS)


def _modulate(x, g, shift, scale):
    return (_rms(x) * g) * (1.0 + scale) + shift


def _hi_lo(x):
    bits = lax.bitcast_convert_type(x, jnp.uint32) & jnp.uint32(0xFFFF0000)
    hi = lax.bitcast_convert_type(bits, F32)
    return hi.astype(BF16), (x - hi).astype(BF16)


def _split_dot(x, w_hi, w_lo):
    x_hi, x_lo = _hi_lo(x)
    return _bdot(x_hi, w_hi) + _bdot(x_lo, w_hi) + _bdot(x_hi, w_lo)


def _mm(x, w_refs, rows=slice(None)):
    if len(w_refs) == 1:
        return _bdot(x.astype(BF16), w_refs[0][rows, :])
    return _split_dot(x.astype(F32), w_refs[0][rows, :], w_refs[1][rows, :])


def _wsplit(w, precise):
    w = w.astype(F32)
    if not precise:
        return (w.astype(BF16),)
    return _hi_lo(w)


def _ada_kernel(c_ref, w_ref, b_ref, o_ref):
    w_hi, w_lo = _hi_lo(w_ref[...])
    o_ref[...] = _split_dot(jax.nn.silu(c_ref[...]), w_hi, w_lo) + b_ref[...]


def _ada(c_all, w_ada, b_ada):
    depth, _, n = w_ada.shape
    rows = c_all.shape[0]
    tn = 1536
    return pl.pallas_call(
        _ada_kernel,
        grid=(depth, n // tn),
        in_specs=[
            pl.BlockSpec((rows, D_MODEL), lambda l, j: (0, 0)),
            pl.BlockSpec((None, D_MODEL, tn), lambda l, j: (l, 0, j)),
            pl.BlockSpec((None, 1, tn), lambda l, j: (l, 0, j)),
        ],
        out_specs=pl.BlockSpec((None, rows, tn), lambda l, j: (l, 0, j)),
        out_shape=jax.ShapeDtypeStruct((depth, rows, n), F32),
        compiler_params=_params("parallel", "parallel"),
        name="ada",
    )(c_all, w_ada, b_ada.reshape(depth, 1, n))


def _vec_spec(arr, tm, tiles_per_seq):
    if arr.shape[1] == 1:
        return pl.BlockSpec((None, 1, arr.shape[2]), lambda i, *_: (i // tiles_per_seq, 0, 0))
    return pl.BlockSpec((None, tm, arr.shape[2]), lambda i, *_: (0, 0, 0))


def _pos_spec(arr, tm, tiles_per_seq):
    if arr.shape[0] == 1:
        return pl.BlockSpec((1, arr.shape[1]), lambda i, *_: (0, 0))
    return pl.BlockSpec((tm, arr.shape[1]), lambda i, *_: (i % tiles_per_seq, 0))


def _full_spec(arr):
    nd = arr.ndim
    return pl.BlockSpec(arr.shape, lambda *_: (0,) * nd)


def _in_even_kernel(x_ref, g_ref, sh_ref, sc_ref, cos_ref, sin_ref, gq_ref, gk_ref,
                    seg_ref, *rest, n_w):
    w_refs = rest[:n_w]
    ret_ref, qb_ref, kb_ref, vb_ref = rest[n_w:n_w + 4]
    vt_ref, kt_ref = rest[n_w + 4:n_w + 6] if len(rest) > n_w + 4 else (None, None)
    y = _mm(_modulate(x_ref[...], g_ref[...], sh_ref[...], sc_ref[...]), w_refs)
    cos = cos_ref[...]
    sin = sin_ref[...]
    for part in range(2):
        for hd in range(H_RET):
            lo = part * W_RET + hd * DK_RET
            blk = y[:, lo:lo + DK_RET]
            rot = blk * cos + pltpu.roll(blk, DK_RET // 2, axis=1) * sin
            if part == 1:
                rot = rot * (DK_RET ** -0.5)
            ret_ref[:, lo:lo + DK_RET] = rot
    ret_ref[:, 2 * W_RET:4 * W_RET] = y[:, 2 * W_RET:4 * W_RET]
    seg = seg_ref[...]
    base = 4 * W_RET

    def qk_norm(blk, g):
        sq_hi, sq_lo = _hi_lo(blk * blk)
        ms = (_bdot(sq_hi, seg) + _bdot(sq_lo, seg)) * (1.0 / DH_DIFF)
        return blk * lax.rsqrt(ms + NORM_EPS) * g

    qb_ref[...] = qk_norm(y[:, base:base + W_DIFF], gq_ref[...]).astype(qb_ref.dtype)
    kn = qk_norm(y[:, base + W_DIFF:base + 2 * W_DIFF], gk_ref[...])
    kb_ref[...] = kn
    v = y[:, base + 2 * W_DIFF:base + 3 * W_DIFF]
    vb_ref[...] = v
    if vt_ref is not None:
        vt_ref[...] = v.T.astype(BF16)
        kt_ref[...] = kn.T


def _in_even(x2d, g, sh, sc, cos_t, sin_t, ws, gq, gk, seg, tm, tiles_per_seq, q_dtype, emit_vt):
    m = x2d.shape[0]
    row = lambda n: pl.BlockSpec((tm, n), lambda i: (i, 0))
    out_specs = [row(4 * W_RET), row(W_DIFF), row(W_DIFF), row(W_DIFF)]
    out_shape = [
        jax.ShapeDtypeStruct((m, 4 * W_RET), F32),
        jax.ShapeDtypeStruct((m, W_DIFF), q_dtype),
        jax.ShapeDtypeStruct((m, W_DIFF), F32),
        jax.ShapeDtypeStruct((m, W_DIFF), F32),
    ]
    if emit_vt:
        out_specs.append(pl.BlockSpec((W_DIFF, tm), lambda i: (0, i)))
        out_shape.append(jax.ShapeDtypeStruct((W_DIFF, m), BF16))
        out_specs.append(pl.BlockSpec((None, W_DIFF, tm), lambda i: (i // tiles_per_seq, 0, i % tiles_per_seq)))
        out_shape.append(jax.ShapeDtypeStruct((m // (tm * tiles_per_seq), W_DIFF, tm * tiles_per_seq), F32))
    return pl.pallas_call(
        functools.partial(_in_even_kernel, n_w=len(ws)),
        grid=(m // tm,),
        in_specs=[
            row(D_MODEL), _full_spec(g), _vec_spec(sh, tm, tiles_per_seq), _vec_spec(sc, tm, tiles_per_seq),
            _pos_spec(cos_t, tm, tiles_per_seq), _pos_spec(sin_t, tm, tiles_per_seq),
            _full_spec(gq), _full_spec(gk), _full_spec(seg),
        ] + [_full_spec(w) for w in ws],
        out_specs=out_specs,
        out_shape=out_shape,
        compiler_params=_params("parallel"),
        name="in_even",
    )(x2d, g, sh, sc, cos_t, sin_t, gq, gk, seg, *ws)


def _ret_kernel(q_ref, k_ref, v_ref, ga_ref, dm_ref, qd_ref, kd_ref, cd_ref, g_ref,
                o_ref, sfin_ref, s_scr, *, n_chunks):
    j = pl.program_id(1)

    @pl.when(j == 0)
    def _():
        s_scr[...] = jnp.zeros_like(s_scr)

    g = g_ref[...]

    def body(c, carry):
        r = pl.multiple_of(c * RET_CHUNK, RET_CHUNK)
        for hd in range(H_RET):
            sl = slice(hd * DK_RET, (hd + 1) * DK_RET)
            q = q_ref[pl.ds(r, RET_CHUNK), sl]
            k = k_ref[pl.ds(r, RET_CHUNK), sl]
            vb = v_ref[pl.ds(r, RET_CHUNK), sl].astype(BF16)
            s = s_scr[hd]
            att = _dot_nt(q.astype(BF16), k.astype(BF16)) * dm_ref[hd]
            o = _bdot(att.astype(BF16), vb) + _bdot((q * qd_ref[hd]).astype(BF16), s.astype(BF16))
            s_scr[hd] = cd_ref[hd] * s + _dot_tn((k * kd_ref[hd]).astype(BF16), vb)
            on = _rms(o) * g
            o_ref[pl.ds(r, RET_CHUNK), sl] = (on * jax.nn.silu(ga_ref[pl.ds(r, RET_CHUNK), sl])).astype(BF16)
        return carry

    lax.fori_loop(0, n_chunks, body, 0)

    @pl.when(j == pl.num_programs(1) - 1)
    def _():
        sfin_ref[...] = s_scr[...]


def _retention_prompt(ret_in, bsz, seq_len, dm, qd, kd, cd, g_ret):
    n_chunks = min(8, seq_len // RET_CHUNK)
    rows = n_chunks * RET_CHUNK
    nj = seq_len // rows
    col = lambda c: pl.BlockSpec((rows, W_RET), lambda b, j: (b * nj + j, c))
    return pl.pallas_call(
        functools.partial(_ret_kernel, n_chunks=n_chunks),
        grid=(bsz, nj),
        in_specs=[col(0), col(1), col(2), col(3), _full_spec(dm), _full_spec(qd), _full_spec(kd),
                  _full_spec(cd), _full_spec(g_ret)],
        out_specs=[
            pl.BlockSpec((rows, W_RET), lambda b, j: (b * nj + j, 0)),
            pl.BlockSpec((None, H_RET, DK_RET, DV_RET), lambda b, j: (b, 0, 0, 0)),
        ],
        out_shape=[
            jax.ShapeDtypeStruct((bsz * seq_len, W_RET), BF16),
            jax.ShapeDtypeStruct((bsz, H_RET, DK_RET, DV_RET), F32),
        ],
        scratch_shapes=[pltpu.VMEM((H_RET, DK_RET, DV_RET), F32)],
        compiler_params=_params("parallel", "arbitrary"),
        name="retention_prompt",
    )(ret_in, ret_in, ret_in, ret_in, dm, qd, kd, cd, g_ret)


def _ret_step_kernel(ret_ref, s_ref, qd_ref, cd_ref, g_ref, o_ref, snew_ref):
    g = g_ref[...]
    rows = lax.broadcasted_iota(jnp.int32, (DK_RET, DK_RET), 0)
    cols = lax.broadcasted_iota(jnp.int32, (DK_RET, DK_RET), 1)
    diag = rows == cols
    for hd in range(H_RET):
        lo = hd * DK_RET
        q = ret_ref[:, lo:lo + DK_RET]
        k = ret_ref[:, W_RET + lo:W_RET + lo + DK_RET]
        v = ret_ref[:, 2 * W_RET + lo:2 * W_RET + lo + DK_RET]
        ga = ret_ref[:, 3 * W_RET + lo:3 * W_RET + lo + DK_RET]
        s = s_ref[hd]
        s_hi, s_lo = _hi_lo(s)
        att = jnp.sum(q * k, axis=-1, keepdims=True)
        q8 = jnp.broadcast_to(q * qd_ref[hd], (8, DK_RET))
        o = att * v + _split_dot(q8, s_hi, s_lo)[0:1, :]
        k_diag = jnp.where(diag, jnp.broadcast_to(k, (DK_RET, DK_RET)), 0.0)
        v_rows = jnp.broadcast_to(v, (DK_RET, DV_RET))
        v_hi, v_lo = _hi_lo(v_rows)
        snew_ref[hd] = cd_ref[hd] * s + _split_dot(k_diag, v_hi, v_lo)
        on = _rms(o) * g
        o_ref[:, lo:lo + DK_RET] = on * jax.nn.silu(ga)


def _retention_step(ret_in, state, qd1, cd, g_ret):
    bsz = state.shape[0]
    ret3 = ret_in.reshape(bsz, 1, 4 * W_RET)
    o, s_new = pl.pallas_call(
        _ret_step_kernel,
        grid=(bsz,),
        in_specs=[
            pl.BlockSpec((None, 1, 4 * W_RET), lambda b: (b, 0, 0)),
            pl.BlockSpec((None, H_RET, DK_RET, DV_RET), lambda b: (b, 0, 0, 0)),
            _full_spec(qd1), _full_spec(cd), _full_spec(g_ret),
        ],
        out_specs=[
            pl.BlockSpec((None, 1, W_RET), lambda b: (b, 0, 0)),
            pl.BlockSpec((None, H_RET, DK_RET, DV_RET), lambda b: (b, 0, 0, 0)),
        ],
        out_shape=[
            jax.ShapeDtypeStruct((bsz, 1, W_RET), F32),
            jax.ShapeDtypeStruct((bsz, H_RET, DK_RET, DV_RET), F32),
        ],
        compiler_params=_params("parallel"),
        name="retention_step",
    )(ret3, state, qd1, cd, g_ret)
    return o.reshape(bsz, W_RET), s_new


def _lambda_value(lam_ref, lam_init):
    lv = lam_ref[...]
    s1 = jnp.sum(lv[0:1, :] * lv[1:2, :], axis=-1, keepdims=True)
    s2 = jnp.sum(lv[2:3, :] * lv[3:4, :], axis=-1, keepdims=True)
    return jnp.exp(s1) - jnp.exp(s2) + lam_init


Q_GROUP = 512
LOG2E = 1.4426950408889634


def _diff_kernel(qi_ref, ki_ref, q_ref, k_ref, vt_ref, lam_ref, g_ref, o_ref,
                 qs_scr, k_scr, m_scr, l_scr, acc_scr, *, t, lam_init):
    p = pl.program_id(2)
    qi = qi_ref[p]
    ki = ki_ref[p]
    qg = min(Q_GROUP, t)
    ng = t // qg

    @pl.when(ki == 0)
    def _():
        q = q_ref[...].astype(F32) * (DH_DIFF ** -0.5 * LOG2E)
        lane = lax.broadcasted_iota(jnp.int32, q.shape, 1)
        q_maps = (jnp.where(lane < DH_DIFF, q, 0.0).astype(BF16), jnp.where(lane >= DH_DIFF, q, 0.0).astype(BF16))
        for mp in range(2):
            for gi in range(ng):
                qs_scr[mp * ng + gi] = q_maps[mp][gi * qg:(gi + 1) * qg, :]
        m_scr[...] = jnp.full_like(m_scr, NEG_BIG)
        l_scr[...] = jnp.zeros_like(l_scr)
        acc_scr[...] = jnp.zeros_like(acc_scr)

    k_scr[...] = k_ref[...].astype(BF16)

    def group(g, n_keys, q_off):
        s = _dot_nt(k_scr[0:n_keys, :], qs_scr[g])
        if q_off is not None:
            key = lax.broadcasted_iota(jnp.int32, s.shape, 0)
            qpos = lax.broadcasted_iota(jnp.int32, s.shape, 1) + q_off
            s = jnp.where(key <= qpos, s, NEG_BIG)
        m_prev = m_scr[g]
        m_new = jnp.maximum(m_prev, jnp.max(s, axis=0, keepdims=True))
        alpha = jnp.exp2(m_prev - m_new)
        pr = jnp.exp2(s - m_new)
        l_scr[g] = alpha * l_scr[g] + jnp.sum(pr, axis=0, keepdims=True)
        acc_scr[g] = alpha * acc_scr[g] + _bdot(vt_ref[:, 0:n_keys], pr.astype(BF16))
        m_scr[g] = m_new

    @pl.when(ki < qi)
    def _():
        def body(i, carry):
            ga, gb = 2 * i, 2 * i + 1
            k = k_scr[...]
            s_a = _dot_nt(k, qs_scr[ga])
            s_b = _dot_nt(k, qs_scr[gb])
            for g, s in ((ga, s_a), (gb, s_b)):
                m_prev = m_scr[g]
                m_new = jnp.maximum(m_prev, jnp.max(s, axis=0, keepdims=True))
                alpha = jnp.exp2(m_prev - m_new)
                pr = jnp.exp2(s - m_new)
                l_scr[g] = alpha * l_scr[g] + jnp.sum(pr, axis=0, keepdims=True)
                acc_scr[g] = alpha * acc_scr[g] + _bdot(vt_ref[...], pr.astype(BF16))
                m_scr[g] = m_new
            return carry
        lax.fori_loop(0, ng, body, 0)

    @pl.when(ki == qi)
    def _():
        for g in range(2 * ng):
            q_off = (g % ng) * qg
            group(g, q_off + qg, q_off)
        lam = _lambda_value(lam_ref, lam_init)
        g_col = g_ref[...]
        for c in range(t // LANES):
            g1, lanes = (c * LANES) // qg, slice((c * LANES) % qg, (c * LANES) % qg + LANES)
            a1 = acc_scr[g1][:, lanes] * (1.0 / l_scr[g1][:, lanes])
            a2 = acc_scr[ng + g1][:, lanes] * (1.0 / l_scr[ng + g1][:, lanes])
            o = a1 - lam * a2
            o = o * lax.rsqrt(jnp.mean(o * o, axis=0, keepdims=True) + NORM_EPS) * g_col * (1.0 - lam_init)
            o_ref[c * LANES:(c + 1) * LANES, :] = o.T.astype(BF16)


def _diff_prompt(qb, kb, vt, lam_vec, g_d, bsz, seq_len, lam_init):
    t = min(1024, seq_len)
    nq = seq_len // t
    pairs = [(i, j) for i in range(nq) for j in range(i + 1)]
    qi = jnp.asarray([a for a, _ in pairs], jnp.int32)
    ki = jnp.asarray([b for _, b in pairs], jnp.int32)
    g_col = g_d.reshape(DV_DIFF, 1)
    qg = min(Q_GROUP, t)
    n_grp = 2 * t // qg
    grid_spec = pltpu.PrefetchScalarGridSpec(
        num_scalar_prefetch=2,
        grid=(bsz, H_DIFF, len(pairs)),
        in_specs=[
            pl.BlockSpec((t, 2 * DH_DIFF), lambda b, h, p, qi, ki: (b * nq + qi[p], h)),
            pl.BlockSpec((t, 2 * DH_DIFF), lambda b, h, p, qi, ki: (b * nq + ki[p], h)),
            pl.BlockSpec((DV_DIFF, t), lambda b, h, p, qi, ki: (h, b * nq + ki[p])),
            pl.BlockSpec(lam_vec.shape, lambda b, h, p, qi, ki: (0, 0)),
            pl.BlockSpec(g_col.shape, lambda b, h, p, qi, ki: (0, 0)),
        ],
        out_specs=pl.BlockSpec((t, DV_DIFF), lambda b, h, p, qi, ki: (b * nq + qi[p], h)),
        scratch_shapes=[
            pltpu.VMEM((n_grp, qg, 2 * DH_DIFF), BF16),
            pltpu.VMEM((t, 2 * DH_DIFF), BF16),
            pltpu.VMEM((n_grp, 1, qg), F32),
            pltpu.VMEM((n_grp, 1, qg), F32),
            pltpu.VMEM((n_grp, DV_DIFF, qg), F32),
        ],
    )
    return pl.pallas_call(
        functools.partial(_diff_kernel, t=t, lam_init=lam_init),
        grid_spec=grid_spec,
        out_shape=jax.ShapeDtypeStruct((bsz * seq_len, H_DIFF * DV_DIFF), BF16),
        compiler_params=_params("parallel", "parallel", "arbitrary"),
        name="diff_prompt",
    )(qi, ki, qb, kb, vt, lam_vec, g_col)


PAGES_PER_STEP = 8
PAGES_PER_UPDATE = 4
N_MAPS = 2 * H_DIFF


def _group_sums(prod):
    return jnp.concatenate(
        [jnp.sum(prod[g * DH_DIFF:(g + 1) * DH_DIFF, :], axis=0, keepdims=True) for g in range(N_MAPS)], axis=0)


def _diff_step_kernel(pt_ref, q_ref, kn_ref, vn_ref, lam_ref, g_ref, *rest, lam_init):
    pps = PAGES_PER_STEP
    k_refs = rest[:pps]
    v_refs = rest[pps:2 * pps]
    o_ref = rest[2 * pps]
    qb_scr, m_scr, l_scr, acc_scr = rest[2 * pps + 1:]
    j = pl.program_id(1)
    scale = DH_DIFF ** -0.5

    @pl.when(j == 0)
    def _():
        qb_scr[...] = jnp.broadcast_to(q_ref[...], (W_DIFF, PAGE_SIZE))
        m_scr[...] = jnp.full_like(m_scr, NEG_BIG)
        l_scr[...] = jnp.zeros_like(l_scr)
        acc_scr[...] = jnp.zeros_like(acc_scr)

    row_head = lax.broadcasted_iota(jnp.int32, (N_MAPS, DV_DIFF), 0) // 2
    for first in range(0, pps, PAGES_PER_UPDATE):
        pages = range(first, first + PAGES_PER_UPDATE)
        scores = {pg: _group_sums(k_refs[pg][...] * qb_scr[...]) * scale for pg in pages}
        s_max = scores[first]
        for pg in pages:
            s_max = jnp.maximum(s_max, scores[pg])
        m_prev = m_scr[...]
        m_new = jnp.maximum(m_prev, jnp.max(s_max, axis=-1, keepdims=True))
        alpha = jnp.exp(m_prev - m_new)
        p_sum = jnp.zeros((N_MAPS, PAGE_SIZE), F32)
        pv = jnp.zeros((N_MAPS, DV_DIFF), F32)
        for pg in pages:
            pr = jnp.exp(scores[pg] - m_new)
            p_sum = p_sum + pr
            p_bits = lax.bitcast_convert_type(pr, jnp.uint32) & jnp.uint32(0xFFFF0000)
            p_top = lax.bitcast_convert_type(p_bits, F32)
            p_hi = p_top.astype(BF16)
            p_both = jnp.concatenate([p_top, pr - p_top], axis=0).astype(BF16)
            for hd in range(H_DIFF):
                v_hi, v_lo = _hi_lo(v_refs[pg][pl.ds(hd, PAGE_SIZE, stride=H_DIFF), :])
                r = _bdot(p_both, jnp.concatenate([v_hi, v_lo], axis=1))
                r = r[0:N_MAPS, 0:DV_DIFF] + r[N_MAPS:2 * N_MAPS, 0:DV_DIFF] + r[0:N_MAPS, DV_DIFF:2 * DV_DIFF]
                pv = pv + jnp.where(row_head == hd, r, 0.0)
        l_scr[...] = alpha * l_scr[...] + jnp.sum(p_sum, axis=-1, keepdims=True)
        acc_scr[...] = alpha * acc_scr[...] + pv
        m_scr[...] = m_new

    @pl.when(j == pl.num_programs(1) - 1)
    def _():
        s = _group_sums(q_ref[...] * kn_ref[...]) * scale
        m_old = m_scr[...]
        m_fin = jnp.maximum(m_old, s)
        a_fin = jnp.exp(m_old - m_fin)
        pr = jnp.exp(s - m_fin)
        l_fin = a_fin * l_scr[...] + pr
        a = (a_fin * acc_scr[...] + pr * vn_ref[...]) / l_fin
        lam = _lambda_value(lam_ref, lam_init)
        g = g_ref[...]
        for hd in range(H_DIFF):
            d = a[2 * hd:2 * hd + 1, :] - lam * a[2 * hd + 1:2 * hd + 2, :]
            o_ref[hd:hd + 1, :] = _rms(d) * g * (1.0 - lam_init)


def _diff_step(qb, kb_new, vb_new, cache_k, cache_v, layer_idx, page_table, lam_vec, g_d, lam_init):
    bsz, n_pages = page_table.shape
    n_layers, n_pool = cache_k.shape[0:2]
    pps = PAGES_PER_STEP
    assert n_pages % pps == 0
    ck = jnp.transpose(cache_k, (0, 1, 3, 4, 5, 2)).reshape(n_layers * n_pool, W_DIFF, PAGE_SIZE)
    cv = cache_v.reshape(n_layers * n_pool, PAGE_SIZE * H_DIFF, DV_DIFF)
    page_table = page_table + layer_idx * n_pool
    col_spec = pl.BlockSpec((None, W_DIFF, 1), lambda b, j, pt: (b, 0, 0))
    v_new = jnp.repeat(vb_new.reshape(bsz, H_DIFF, DV_DIFF), 2, axis=1)

    def page_spec(pg):
        return pl.BlockSpec((None, W_DIFF, PAGE_SIZE), lambda b, j, pt, pg=pg: (pt[b, j * pps + pg], 0, 0))

    grid_spec = pltpu.PrefetchScalarGridSpec(
        num_scalar_prefetch=1,
        grid=(bsz, n_pages // pps),
        in_specs=[col_spec, col_spec,
                  pl.BlockSpec((None, N_MAPS, DV_DIFF), lambda b, j, pt: (b, 0, 0)),
                  pl.BlockSpec(lam_vec.shape, lambda b, j, pt: (0, 0)),
                  pl.BlockSpec(g_d.shape, lambda b, j, pt: (0, 0))]
                 + [page_spec(pg) for pg in range(pps)] + [page_spec(pg) for pg in range(pps)],
        out_specs=pl.BlockSpec((None, H_DIFF, DV_DIFF), lambda b, j, pt: (b, 0, 0)),
        scratch_shapes=[
            pltpu.VMEM((W_DIFF, PAGE_SIZE), F32),
            pltpu.VMEM((N_MAPS, 1), F32),
            pltpu.VMEM((N_MAPS, 1), F32),
            pltpu.VMEM((N_MAPS, DV_DIFF), F32),
        ],
    )
    o = pl.pallas_call(
        functools.partial(_diff_step_kernel, lam_init=lam_init),
        grid_spec=grid_spec,
        out_shape=jax.ShapeDtypeStruct((bsz, H_DIFF, DV_DIFF), F32),
        compiler_params=_params("parallel", "arbitrary"),
        name="diff_step",
    )(page_table, qb.reshape(bsz, W_DIFF, 1), kb_new.reshape(bsz, W_DIFF, 1), v_new, lam_vec, g_d,
      *([ck] * pps), *([cv] * pps))
    return o.reshape(bsz, W_DIFF)


def _out_even_kernel(a_ref, b_ref, x_ref, gt_ref, *rest):
    w_refs = rest[:-1]
    o_ref = rest[-1]
    m = _mm(a_ref[...], w_refs, slice(0, W_RET)) + _mm(b_ref[...], w_refs, slice(W_RET, 2 * W_RET))
    o_ref[...] = x_ref[...] + gt_ref[...] * m


def _out_even(o_ret, o_diff, ws, x2d, gt, tm, tiles_per_seq):
    m = x2d.shape[0]
    row = lambda n: pl.BlockSpec((tm, n), lambda i: (i, 0))
    return pl.pallas_call(
        _out_even_kernel,
        grid=(m // tm,),
        in_specs=[row(W_RET), row(W_RET), row(D_MODEL), _vec_spec(gt, tm, tiles_per_seq)]
                 + [_full_spec(w) for w in ws],
        out_specs=row(D_MODEL),
        out_shape=jax.ShapeDtypeStruct((m, D_MODEL), F32),
        compiler_params=_params("parallel"),
        name="out_even",
    )(o_ret, o_diff, x2d, gt, *ws)


def _ffn_kernel(x_ref, g_ref, sh_ref, sc_ref, gt_ref, *rest, n_w):
    w1, w3, w2 = rest[0:n_w], rest[n_w:2 * n_w], rest[2 * n_w:3 * n_w]
    o_ref = rest[3 * n_w]
    x = x_ref[...]
    h = _modulate(x, g_ref[...], sh_ref[...], sc_ref[...])
    if n_w == 1:
        h = h.astype(BF16)
    u = jax.nn.silu(_mm(h, w1)) * _mm(h, w3)
    o_ref[...] = x + gt_ref[...] * _mm(u, w2)


def _ffn(x2d, g, sh, sc, gt, w1s, w3s, w2s, tm, tiles_per_seq):
    m = x2d.shape[0]
    row = pl.BlockSpec((tm, D_MODEL), lambda i: (i, 0))
    vs = lambda a: _vec_spec(a, tm, tiles_per_seq)
    ws = (*w1s, *w3s, *w2s)
    return pl.pallas_call(
        functools.partial(_ffn_kernel, n_w=len(w1s)),
        grid=(m // tm,),
        in_specs=[row, _full_spec(g), vs(sh), vs(sc), vs(gt)] + [_full_spec(w) for w in ws],
        out_specs=row,
        out_shape=jax.ShapeDtypeStruct((m, D_MODEL), F32),
        compiler_params=_params("parallel"),
        name="ffn",
    )(x2d, g, sh, sc, gt, *ws)


def _modmm_kernel(x_ref, g_ref, sh_ref, sc_ref, *rest):
    w_refs = rest[:-1]
    o_ref = rest[-1]
    o_ref[...] = _mm(_modulate(x_ref[...], g_ref[...], sh_ref[...], sc_ref[...]), w_refs)


def _modmm(x2d, g, sh, sc, ws, tm, tiles_per_seq):
    m = x2d.shape[0]
    n = ws[0].shape[1]
    vs = lambda a: _vec_spec(a, tm, tiles_per_seq)
    return pl.pallas_call(
        _modmm_kernel,
        grid=(m // tm,),
        in_specs=[pl.BlockSpec((tm, D_MODEL), lambda i: (i, 0)), _full_spec(g), vs(sh), vs(sc)]
                 + [_full_spec(w) for w in ws],
        out_specs=pl.BlockSpec((tm, n), lambda i: (i, 0)),
        out_shape=jax.ShapeDtypeStruct((m, n), F32),
        compiler_params=_params("parallel"),
        name="modmm",
    )(x2d, g, sh, sc, *ws)


def _mm_at(x, w_refs, idx):
    if len(w_refs) == 1:
        return _bdot(x.astype(BF16), w_refs[0][idx])
    return _split_dot(x, w_refs[0][idx], w_refs[1][idx])


def _s5_input_drive(u, bmat_refs, j):
    return _mm_at(u[:, j * LANES:(j + 1) * LANES], bmat_refs, j)


def _s5_output_block(h_re_of, h_im_of, cmat_refs, jb):
    acc = None
    for q in range(4):
        r = 4 * jb + q
        t = (_mm_at(h_re_of(r), cmat_refs, (r, slice(0, LANES), slice(None)))
             + _mm_at(h_im_of(r), cmat_refs, (r, slice(LANES, 2 * LANES), slice(None))))
        acc = t if acc is None else acc + t
    return acc


def _s5_prompt_kernel(u_ref, are_ref, aim_ref, bmat_ref, cmat_ref, d_ref, z_ref, hre_ref, him_ref,
                      sre, sim, st_re, st_im, *, tc):
    j = pl.program_id(1)

    @pl.when(j == 0)
    def _():
        st_re[...] = jnp.zeros_like(st_re)
        st_im[...] = jnp.zeros_like(st_im)

    u = u_ref[...]
    half = 4 * LANES
    for jb in range(8):
        res = _s5_input_drive(u, (bmat_ref,), jb)
        for q in range(4):
            r = 4 * jb + q
            sre[pl.ds(r, tc, stride=STEP_PITCH), :] = res[:, q * LANES:(q + 1) * LANES]
            sim[pl.ds(r, tc, stride=STEP_PITCH), :] = res[:, half + q * LANES:half + (q + 1) * LANES]

    a_re = are_ref[...]
    a_im = aim_ref[...]

    def step(t, carry):
        h_re, h_im = carry
        base = pl.multiple_of(t * STEP_PITCH, 8)
        n_re = a_re * h_re - a_im * h_im + sre[pl.ds(base, N_PAIRS), :]
        n_im = a_re * h_im + a_im * h_re + sim[pl.ds(base, N_PAIRS), :]
        sre[pl.ds(base, N_PAIRS), :] = n_re
        sim[pl.ds(base, N_PAIRS), :] = n_im
        return n_re, n_im

    h_re, h_im = lax.fori_loop(0, tc, step, (st_re[...], st_im[...]))
    st_re[...] = h_re
    st_im[...] = h_im

    for jb in range(8):
        y = _s5_output_block(lambda r: sre[pl.ds(r, tc, stride=STEP_PITCH), :],
                             lambda r: sim[pl.ds(r, tc, stride=STEP_PITCH), :], (cmat_ref,), jb)
        sl = slice(jb * LANES, (jb + 1) * LANES)
        y = y + d_ref[:, sl] * u[:, sl]
        z_ref[:, sl] = jax.nn.gelu(y, approximate=True).astype(BF16)

    @pl.when(j == pl.num_programs(1) - 1)
    def _():
        hre_ref[...] = h_re
        him_ref[...] = h_im


def _s5_prompt(u2d, bsz, seq_len, a_re, a_im, bmat, cmat, d_row):
    tc = min(256, seq_len)
    nj = seq_len // tc
    state_spec = pl.BlockSpec((None, N_PAIRS, LANES), lambda b, j: (b, 0, 0))
    return pl.pallas_call(
        functools.partial(_s5_prompt_kernel, tc=tc),
        grid=(bsz, nj),
        in_specs=[pl.BlockSpec((tc, D_MODEL), lambda b, j: (b * nj + j, 0)),
                  _full_spec(a_re), _full_spec(a_im), _full_spec(bmat), _full_spec(cmat), _full_spec(d_row)],
        out_specs=[pl.BlockSpec((tc, D_MODEL), lambda b, j: (b * nj + j, 0)), state_spec, state_spec],
        out_shape=[
            jax.ShapeDtypeStruct((bsz * seq_len, D_MODEL), BF16),
            jax.ShapeDtypeStruct((bsz, N_PAIRS, LANES), F32),
            jax.ShapeDtypeStruct((bsz, N_PAIRS, LANES), F32),
        ],
        scratch_shapes=[
            pltpu.VMEM((tc * STEP_PITCH, LANES), F32),
            pltpu.VMEM((tc * STEP_PITCH, LANES), F32),
            pltpu.VMEM((N_PAIRS, LANES), F32),
            pltpu.VMEM((N_PAIRS, LANES), F32),
        ],
        compiler_params=_params("parallel", "arbitrary"),
        name="s5_prompt",
    )(u2d, a_re, a_im, bmat, cmat, d_row)


def _s5_step_kernel(u_ref, h0re_ref, h0im_ref, are_ref, aim_ref, bhi_ref, blo_ref, chi_ref, clo_ref, d_ref,
                    z_ref, hre_ref, him_ref):
    u = u_ref[...]
    half = 4 * LANES
    bmat_refs = (bhi_ref, blo_ref)
    cmat_refs = (chi_ref, clo_ref)
    for jb in range(8):
        res = _s5_input_drive(u, bmat_refs, jb)
        sl = slice(jb * half, (jb + 1) * half)
        a_re = are_ref[:, sl]
        a_im = aim_ref[:, sl]
        h_re = h0re_ref[:, sl]
        h_im = h0im_ref[:, sl]
        hre_ref[:, sl] = a_re * h_re - a_im * h_im + res[:, 0:half]
        him_ref[:, sl] = a_re * h_im + a_im * h_re + res[:, half:2 * half]
    for jb in range(8):
        y = _s5_output_block(lambda r: hre_ref[:, r * LANES:(r + 1) * LANES],
                             lambda r: him_ref[:, r * LANES:(r + 1) * LANES], cmat_refs, jb)
        sl = slice(jb * LANES, (jb + 1) * LANES)
        y = y + d_ref[:, sl] * u[:, sl]
        z_ref[:, sl] = jax.nn.gelu(y, approximate=True)


def _s5_step(u2d, h0_re, h0_im, a_re_row, a_im_row, bmats, cmats, d_row):
    bsz = u2d.shape[0]
    n_state = N_GROUPS * SSM_STATE
    args = (u2d, h0_re, h0_im, a_re_row, a_im_row, *bmats, *cmats, d_row)
    return pl.pallas_call(
        _s5_step_kernel,
        grid=(1,),
        in_specs=[_full_spec(a) for a in args],
        out_specs=[pl.BlockSpec((bsz, D_MODEL), lambda i: (0, 0)),
                   pl.BlockSpec((bsz, n_state), lambda i: (0, 0)),
                   pl.BlockSpec((bsz, n_state), lambda i: (0, 0))],
        out_shape=[
            jax.ShapeDtypeStruct((bsz, D_MODEL), F32),
            jax.ShapeDtypeStruct((bsz, n_state), F32),
            jax.ShapeDtypeStruct((bsz, n_state), F32),
        ],
        compiler_params=_params("arbitrary"),
        name="s5_step",
    )(*args)


def _glu_router_kernel(z_ref, x_ref, gt_ref, g_ref, sh_ref, sc_ref, wrh_ref, wrl_ref, br_ref, *rest):
    wg_refs = rest[:-3]
    x_out, h_out, route_out = rest[-3:]
    vg = _mm(z_ref[...], wg_refs)
    val = vg[:, 0:D_MODEL]
    gate = vg[:, D_MODEL:2 * D_MODEL]
    x = x_ref[...] + gt_ref[...] * (val * jax.nn.sigmoid(gate))
    x_out[...] = x
    h = _modulate(x, g_ref[...], sh_ref[...], sc_ref[...])
    h_out[...] = h
    logits = _split_dot(h, wrh_ref[...], wrl_ref[...]) + br_ref[...]
    lane = lax.broadcasted_iota(jnp.int32, logits.shape, 1)
    v1 = jnp.max(logits, axis=-1, keepdims=True)
    i1 = jnp.min(jnp.where(logits == v1, lane, LANES), axis=-1, keepdims=True)
    others = jnp.where(lane == i1, NEG_BIG, logits)
    v2 = jnp.max(others, axis=-1, keepdims=True)
    i2 = jnp.min(jnp.where(others == v2, lane, LANES), axis=-1, keepdims=True)
    e = jnp.exp(v2 - v1)
    g1 = 1.0 / (1.0 + e)
    g2 = e / (1.0 + e)
    route = jnp.where(lane == i1, g1, 0.0) + jnp.where(lane == i2, g2, 0.0)
    flag = jnp.logical_or(lane == i1 + N_EXPERTS, lane == i2 + N_EXPERTS)
    route_out[...] = route + jnp.where(flag, 1.0, 0.0)


def _glu_router(z, wgs, x2d, gt, g, sh, sc, wr_hi, wr_lo, br, tm, tiles_per_seq):
    m = x2d.shape[0]
    row = lambda n: pl.BlockSpec((tm, n), lambda i: (i, 0))
    vs = lambda a: _vec_spec(a, tm, tiles_per_seq)
    return pl.pallas_call(
        _glu_router_kernel,
        grid=(m // tm,),
        in_specs=[row(D_MODEL), row(D_MODEL), vs(gt), _full_spec(g), vs(sh), vs(sc),
                  _full_spec(wr_hi), _full_spec(wr_lo), _full_spec(br)] + [_full_spec(w) for w in wgs],
        out_specs=[row(D_MODEL), row(D_MODEL), row(LANES)],
        out_shape=[
            jax.ShapeDtypeStruct((m, D_MODEL), F32),
            jax.ShapeDtypeStruct((m, D_MODEL), F32),
            jax.ShapeDtypeStruct((m, LANES), F32),
        ],
        compiler_params=_params("parallel"),
        name="glu_router",
    )(z, x2d, gt, g, sh, sc, wr_hi, wr_lo, br, *wgs)


def _moe_kernel(te_ref, nv_ref, xs_ref, w1_ref, w3_ref, w2_ref, o_ref, x_scr, acc_scr):
    i = pl.program_id(0)
    f = pl.program_id(1)
    nf = pl.num_programs(1)
    valid = i < nv_ref[0]

    @pl.when(jnp.logical_and(valid, f == 0))
    def _():
        x_scr[...] = xs_ref[...].astype(BF16)

    @pl.when(valid)
    def _():
        x = x_scr[...]
        a = _bdot(x, w1_ref[...].astype(BF16))
        b = _bdot(x, w3_ref[...].astype(BF16))
        u = (jax.nn.silu(a) * b).astype(BF16)
        contrib = _bdot(u, w2_ref[...].astype(BF16))

        @pl.when(f == 0)
        def _():
            acc_scr[...] = contrib

        @pl.when(f > 0)
        def _():
            acc_scr[...] = acc_scr[...] + contrib

        @pl.when(f == nf - 1)
        def _():
            o_ref[...] = acc_scr[...]

    @pl.when(jnp.logical_and(jnp.logical_not(valid), f == nf - 1))
    def _():
        o_ref[...] = jnp.zeros_like(o_ref)


def _moe_experts(xs, tile_expert, n_valid, w1, w3, w2, tm):
    rows = xs.shape[0]
    n_tiles = rows // tm
    tf = 1792
    nf = D_FF_EXPERT // tf

    def fsel(i, f, nv):
        return jnp.where(i < nv[0], f, nf - 1)

    grid_spec = pltpu.PrefetchScalarGridSpec(
        num_scalar_prefetch=2,
        grid=(n_tiles, nf),
        in_specs=[
            pl.BlockSpec((tm, D_MODEL), lambda i, f, te, nv: (i, 0)),
            pl.BlockSpec((None, D_MODEL, tf), lambda i, f, te, nv: (te[i], 0, fsel(i, f, nv))),
            pl.BlockSpec((None, D_MODEL, tf), lambda i, f, te, nv: (te[i], 0, fsel(i, f, nv))),
            pl.BlockSpec((None, tf, D_MODEL), lambda i, f, te, nv: (te[i], fsel(i, f, nv), 0)),
        ],
        out_specs=pl.BlockSpec((tm, D_MODEL), lambda i, f, te, nv: (i, 0)),
        scratch_shapes=[pltpu.VMEM((tm, D_MODEL), BF16), pltpu.VMEM((tm, D_MODEL), F32)],
    )
    return pl.pallas_call(
        _moe_kernel,
        grid_spec=grid_spec,
        out_shape=jax.ShapeDtypeStruct((rows, D_MODEL), F32),
        compiler_params=_params("parallel", "arbitrary"),
        name="moe_experts",
    )(tile_expert, n_valid, xs, w1, w3, w2)


def _row_params(*sem):
    return pltpu.CompilerParams(dimension_semantics=sem, vmem_limit_bytes=VMEM_LIMIT_BYTES,
                                disable_bounds_checks=True)


def _dispatch_kernel(sa_ref, sb_ref, h_ref, xs_in_ref, xs_ref, sem, *, tm):
    del xs_in_ref
    base = pl.program_id(0) * tm

    def copies(t):
        tok = base + t
        src = h_ref.at[pl.ds(t, 1)]
        return (pltpu.make_async_copy(src, xs_ref.at[pl.ds(sa_ref[tok], 1)], sem.at[0]),
                pltpu.make_async_copy(src, xs_ref.at[pl.ds(sb_ref[tok], 1)], sem.at[1]))

    def issue(t, carry):
        for cp in copies(t):
            cp.start()
        return carry

    def drain(t, carry):
        for cp in copies(t):
            cp.wait()
        return carry

    lax.fori_loop(0, tm, issue, 0)
    lax.fori_loop(0, tm, drain, 0)


def _moe_dispatch(h, slot_a, slot_b, n_rows, tm):
    m = h.shape[0]
    grid_spec = pltpu.PrefetchScalarGridSpec(
        num_scalar_prefetch=2,
        grid=(m // tm,),
        in_specs=[pl.BlockSpec((tm, D_MODEL), lambda i, sa, sb: (i, 0)), pl.BlockSpec(memory_space=pl.ANY)],
        out_specs=pl.BlockSpec(memory_space=pl.ANY),
        scratch_shapes=[pltpu.SemaphoreType.DMA((2,))],
    )
    return pl.pallas_call(
        functools.partial(_dispatch_kernel, tm=tm),
        grid_spec=grid_spec,
        out_shape=jax.ShapeDtypeStruct((n_rows, D_MODEL), F32),
        input_output_aliases={3: 0},
        compiler_params=_row_params("arbitrary"),
        name="moe_dispatch",
    )(slot_a, slot_b, h, jnp.zeros((n_rows, D_MODEL), F32))


def _combine_kernel(sa_ref, sb_ref, x_ref, gt_ref, ga_ref, gb_ref, ys_ref, o_ref, buf_a, buf_b, sem, *, tm):
    base = pl.program_id(0) * tm

    def copies(t):
        tok = base + t
        return (pltpu.make_async_copy(ys_ref.at[pl.ds(sa_ref[tok], 1)], buf_a.at[pl.ds(t, 1)], sem.at[0]),
                pltpu.make_async_copy(ys_ref.at[pl.ds(sb_ref[tok], 1)], buf_b.at[pl.ds(t, 1)], sem.at[1]))

    def issue(t, carry):
        for cp in copies(t):
            cp.start()
        return carry

    def drain(t, carry):
        for cp in copies(t):
            cp.wait()
        return carry

    lax.fori_loop(0, tm, issue, 0)
    lax.fori_loop(0, tm, drain, 0)
    o_ref[...] = x_ref[...] + gt_ref[...] * (ga_ref[...] * buf_a[...] + gb_ref[...] * buf_b[...])


def _moe_combine(x2d, gt, gate_a, gate_b, ys, slot_a, slot_b, tm, tiles_per_seq):
    m = x2d.shape[0]
    row = pl.BlockSpec((tm, D_MODEL), lambda i, sa, sb: (i, 0))
    col = pl.BlockSpec((tm, 1), lambda i, sa, sb: (i, 0))
    grid_spec = pltpu.PrefetchScalarGridSpec(
        num_scalar_prefetch=2,
        grid=(m // tm,),
        in_specs=[row, _vec_spec(gt, tm, tiles_per_seq), col, col, pl.BlockSpec(memory_space=pl.ANY)],
        out_specs=row,
        scratch_shapes=[pltpu.VMEM((tm, D_MODEL), F32), pltpu.VMEM((tm, D_MODEL), F32),
                        pltpu.SemaphoreType.DMA((2,))],
    )
    return pl.pallas_call(
        functools.partial(_combine_kernel, tm=tm),
        grid_spec=grid_spec,
        out_shape=jax.ShapeDtypeStruct((m, D_MODEL), F32),
        compiler_params=_row_params("arbitrary"),
        name="moe_combine",
    )(slot_a, slot_b, x2d, gt, gate_a, gate_b, ys)


def _moe(h, route, x2d, gt, w1, w3, w2, tm_rows, tm, tiles_per_seq):
    m = h.shape[0]
    comb = route[:, 0:N_EXPERTS]
    sel = route[:, N_EXPERTS:2 * N_EXPERTS] > 0.5
    seli = sel.astype(jnp.int32)
    counts = jnp.sum(seli, axis=0)
    padded = ((counts + tm_rows - 1) // tm_rows) * tm_rows
    ends = jnp.cumsum(padded)
    offsets = ends - padded
    rank = jnp.cumsum(seli, axis=0) - 1
    pos = offsets[None, :] + rank
    n_rows = ((2 * m + N_EXPERTS * (tm_rows - 1)) // tm_rows + 1) * tm_rows
    n_tiles = n_rows // tm_rows
    tile_start = jnp.arange(n_tiles, dtype=jnp.int32) * tm_rows
    n_valid = (ends[-1] // tm_rows).astype(jnp.int32)
    first_row = jnp.minimum(tile_start, ends[-1] - 1)
    tile_expert = jnp.sum((ends[None, :] <= first_row[:, None]).astype(jnp.int32), axis=1)
    slot_a = jnp.min(jnp.where(sel, pos, n_rows), axis=1).astype(jnp.int32)
    slot_b = jnp.max(jnp.where(sel, pos, -1), axis=1).astype(jnp.int32)
    expert_id = jnp.arange(N_EXPERTS, dtype=jnp.int32)[None, :]
    e_lo = jnp.min(jnp.where(sel, expert_id, N_EXPERTS), axis=1, keepdims=True)
    e_hi = jnp.max(jnp.where(sel, expert_id, -1), axis=1, keepdims=True)
    gate_a = jnp.sum(jnp.where(expert_id == e_lo, comb, 0.0), axis=1, keepdims=True)
    gate_b = jnp.sum(jnp.where(expert_id == e_hi, comb, 0.0), axis=1, keepdims=True)
    tm_dispatch = min(1024, m)
    xs = _moe_dispatch(h, slot_a, slot_b, n_rows, tm_dispatch)
    ys = _moe_experts(xs, tile_expert, n_valid.reshape(1), w1, w3, w2, tm_rows)
    return _moe_combine(x2d, gt, gate_a, gate_b, ys, slot_a, slot_b, tm, tiles_per_seq)


def _retention_constants(chunk):
    log_g = jnp.log1p(-jnp.exp2(-5.0 - jnp.arange(H_RET, dtype=F32)))
    idx = jnp.arange(chunk, dtype=F32)
    rel = idx[:, None] - idx[None, :]
    dm = jnp.where(rel >= 0, jnp.exp(log_g[:, None, None] * jnp.maximum(rel, 0.0)), 0.0)
    q_decay = jnp.exp(log_g[:, None] * (idx[None, :] + 1.0))
    k_decay = jnp.exp(log_g[:, None] * (chunk - 1.0 - idx[None, :]))
    chunk_decay = jnp.exp(log_g * chunk)
    qd = jnp.broadcast_to(q_decay[:, :, None], (H_RET, chunk, DK_RET))
    kd = jnp.broadcast_to(k_decay[:, :, None], (H_RET, chunk, DK_RET))
    cd = jnp.broadcast_to(chunk_decay[:, None, None], (H_RET, 1, DV_RET))
    return dm, qd, kd, cd


def _rotary_tables(pos):
    half = DK_RET // 2
    inv = np.float64(ROPE_BASE) ** (-np.arange(half, dtype=np.float64) / half)
    inv_hi = inv.astype(np.float32)
    inv_lo = (inv - inv_hi.astype(np.float64)).astype(np.float32)
    p = jnp.asarray(pos).astype(F32)[:, None]
    ang = p * jnp.asarray(inv_hi)[None, :] + p * jnp.asarray(inv_lo)[None, :]
    cos = jnp.cos(ang)
    sin = jnp.sin(ang)
    return jnp.concatenate([cos, cos], axis=-1), jnp.concatenate([-sin, sin], axis=-1)


def _s5_matrices(lam_re, lam_im, log_step, b_re, b_im, c_re, c_im, d_skip):
    lr = lam_re.astype(F32)
    li = lam_im.astype(F32)
    step = jnp.exp(log_step.astype(F32))[:, None]
    mag = jnp.exp(lr * step)
    ab_re = mag * jnp.cos(li * step)
    ab_im = mag * jnp.sin(li * step)
    nr = ab_re - 1.0
    ni = ab_im
    den = lr * lr + li * li
    kr = (nr * lr + ni * li) / den
    ki = (ni * lr - nr * li) / den
    br = b_re.astype(F32)
    bi = b_im.astype(F32)
    bb_re = kr[..., None] * br - ki[..., None] * bi
    bb_im = kr[..., None] * bi + ki[..., None] * br
    eye8 = jnp.eye(8, dtype=F32)

    def drive(bb):
        t = bb.reshape(8, 8, SSM_STATE, SSM_GROUP).transpose(0, 1, 3, 2)
        t = t[:, :, :, None, :] * eye8[None, :, None, :, None]
        return t.reshape(8, LANES, 8 * SSM_STATE)

    bmat = jnp.concatenate([drive(bb_re), drive(bb_im)], axis=-1)
    r_idx = jnp.arange(N_PAIRS)[:, None, None]
    gl_idx = jnp.arange(2)[None, :, None]
    gb_idx = jnp.arange(8)[None, None, :]
    onehot = (gb_idx == 2 * (r_idx % 4) + gl_idx).astype(F32)

    def readout(c):
        t = c.astype(F32).reshape(N_PAIRS, 2, SSM_GROUP, SSM_STATE).transpose(0, 1, 3, 2)
        t = t[:, :, :, None, :] * onehot[:, :, None, :, None]
        return t.reshape(N_PAIRS, LANES, LANES)

    cmat = jnp.concatenate([readout(c_re), -readout(c_im)], axis=1)
    return (ab_re.reshape(N_PAIRS, LANES), ab_im.reshape(N_PAIRS, LANES), bmat, cmat,
            d_skip.astype(F32).reshape(1, D_MODEL))


def _trunk(x, mods, pos, ret0, ssm0, paged, p):
    bsz, seq_len, _ = x.shape
    m = bsz * seq_len
    prompt = seq_len > 1
    depth = p["w_ada"].shape[0]
    x2d = x.reshape(m, D_MODEL)
    if prompt:
        tm = min(512, seq_len)
        tps = seq_len // tm
        tm_ffn = min(256, seq_len)
        tps_ffn = seq_len // tm_ffn
        vec = lambda v: v.reshape(bsz, 1, D_MODEL)
    else:
        tm = tm_ffn = m
        tps = tps_ffn = 1
        vec = lambda v: v.reshape(1, bsz, D_MODEL)
    row = lambda v: v.reshape(1, -1).astype(F32)
    wsp = lambda w: _wsplit(w, not prompt)
    rets, ks_new, vs_new, srs, sis = [], [], [], [], []
    for layer in range(depth):
        sh1, sc1, gt1, sh2, sc2, gt2 = [vec(v) for v in jnp.split(mods[layer], N_ADA, axis=-1)]
        i = layer // 2
        if layer % 2 == 0:
            lam_init = 0.8 - 0.6 * math.exp(-0.3 * layer)
            cos_t, sin_t = _rotary_tables(pos)
            seg = (jnp.arange(W_DIFF)[:, None] // DH_DIFF == jnp.arange(W_DIFF)[None, :] // DH_DIFF).astype(BF16)
            gq = jnp.tile(row(p["g_qnorm"][i]), (1, W_DIFF // DH_DIFF))
            gk = jnp.tile(row(p["g_knorm"][i]), (1, W_DIFF // DH_DIFF))
            ret_in, qb, kb, vb, *vt = _in_even(
                x2d, row(p["g_norm1"][layer]), sh1, sc1, cos_t, sin_t, wsp(p["w_in_mix"][i]),
                gq, gk, seg, tm, tps, BF16 if prompt else F32, prompt)
            lam_vec = jnp.stack([p["lam_q1"][i], p["lam_k1"][i], p["lam_q2"][i], p["lam_k2"][i]]).astype(F32)
            g_ret = row(p["g_ret"][i])
            g_d = row(p["g_dnorm"][i])
            if prompt:
                dm, qd, kd, cd = _retention_constants(math.gcd(seq_len, RET_CHUNK))
                o_ret, s_new = _retention_prompt(ret_in, bsz, seq_len, dm, qd, kd, cd, g_ret)
                o_diff = _diff_prompt(qb, kb, vt[0], lam_vec, g_d, bsz, seq_len, lam_init)
            else:
                _, qd, _, cd = _retention_constants(1)
                o_ret, s_new = _retention_step(ret_in, ret0[i].astype(F32), qd, cd, g_ret)
                o_diff = _diff_step(qb, kb, vb, paged["cache_k"], paged["cache_v"], i,
                                    paged["page_table"], lam_vec, g_d, lam_init)
            rets.append(s_new)
            if prompt:
                k_out = jnp.transpose(vt[1].reshape(bsz, H_DIFF, 2, DH_DIFF, seq_len), (0, 4, 1, 2, 3))
            else:
                k_out = kb.reshape(bsz, seq_len, H_DIFF, 2, DH_DIFF)
            ks_new.append(k_out)
            vs_new.append(vb.reshape(bsz, seq_len, H_DIFF, DV_DIFF))
            x2d = _out_even(o_ret, o_diff, wsp(p["w_out_mix"][i]), x2d, gt1, tm, tps)
            x2d = _ffn(x2d, row(p["g_norm2"][layer]), sh2, sc2, gt2, wsp(p["w_ffn1"][i]),
                       wsp(p["w_ffn3"][i]), wsp(p["w_ffn2"][i]), tm_ffn, tps_ffn)
        else:
            a_re, a_im, bmat, cmat, d_row = _s5_matrices(
                p["lam_re"][i], p["lam_im"][i], p["log_step"][i], p["b_re"][i], p["b_im"][i],
                p["c_re"][i], p["c_im"][i], p["d_skip"][i])
            u2d = _modmm(x2d, row(p["g_norm1"][layer]), sh1, sc1, wsp(p["w_ssm_in"][i]), tm, tps)
            n_state = N_GROUPS * SSM_STATE
            if prompt:
                z, h_re, h_im = _s5_prompt(u2d, bsz, seq_len, a_re, a_im, bmat.astype(BF16), cmat.astype(BF16),
                                           d_row)
            else:
                z, h_re, h_im = _s5_step(
                    u2d, ssm0[0][i].astype(F32).reshape(bsz, n_state), ssm0[1][i].astype(F32).reshape(bsz, n_state),
                    a_re.reshape(1, n_state), a_im.reshape(1, n_state), wsp(bmat), wsp(cmat), d_row)
            srs.append(h_re.reshape(bsz, N_GROUPS, SSM_STATE))
            sis.append(h_im.reshape(bsz, N_GROUPS, SSM_STATE))
            wr = jnp.zeros((D_MODEL, LANES), F32).at[:, 0:N_EXPERTS].set(p["w_router"][i].astype(F32))
            wr_hi, wr_lo = _wsplit(wr, True)
            br = jnp.full((1, LANES), NEG_BIG, F32).at[0, 0:N_EXPERTS].set(p["b_router"][i].astype(F32))
            x2d, h_mod, route = _glu_router(
                z, wsp(p["w_glu"][i]), x2d, gt1, row(p["g_norm2"][layer]), sh2, sc2,
                wr_hi, wr_lo, br, tm_ffn, tps_ffn)
            x2d = _moe(h_mod, route, x2d, gt2, p["w_moe1"][i].astype(BF16), p["w_moe3"][i].astype(BF16),
                       p["w_moe2"][i].astype(BF16),
                       512 if prompt else 32, tm, tps)
    return (x2d.reshape(bsz, seq_len, D_MODEL), jnp.stack(rets), jnp.stack(ks_new), jnp.stack(vs_new),
            jnp.stack(srs), jnp.stack(sis))


def kernel(x_prompt, x_sample, state_ret, cache_k, cache_v, state_ssm_re, state_ssm_im, page_table, c_prompt, c_sample, w_ada, b_ada, g_norm1, g_norm2, w_in_mix, w_out_mix, g_ret, g_qnorm, g_knorm, g_dnorm, lam_q1, lam_k1, lam_q2, lam_k2, w_ffn1, w_ffn3, w_ffn2, w_ssm_in, lam_re, lam_im, log_step, b_re, b_im, c_re, c_im, d_skip, w_glu, w_router, b_router, w_moe1, w_moe3, w_moe2):
    p = dict(w_ada=w_ada, g_norm1=g_norm1, g_norm2=g_norm2, w_in_mix=w_in_mix, w_out_mix=w_out_mix,
             g_ret=g_ret, g_qnorm=g_qnorm, g_knorm=g_knorm, g_dnorm=g_dnorm, lam_q1=lam_q1, lam_k1=lam_k1,
             lam_q2=lam_q2, lam_k2=lam_k2, w_ffn1=w_ffn1, w_ffn3=w_ffn3, w_ffn2=w_ffn2, w_ssm_in=w_ssm_in,
             lam_re=lam_re, lam_im=lam_im, log_step=log_step, b_re=b_re, b_im=b_im, c_re=c_re, c_im=c_im,
             d_skip=d_skip, w_glu=w_glu, w_router=w_router, b_router=b_router,
             w_moe1=w_moe1, w_moe3=w_moe3, w_moe2=w_moe2)
    n_prompt = c_prompt.shape[0]
    n_sample = c_sample.shape[0]
    n_all = n_prompt + n_sample
    rows = ((n_all + 7) // 8) * 8
    c_all = jnp.zeros((rows, D_MODEL), F32).at[0:n_prompt].set(c_prompt).at[n_prompt:n_all].set(c_sample)
    mods = _ada(c_all, w_ada, b_ada)
    seq_len = x_prompt.shape[1]
    past_len = page_table.shape[1] * PAGE_SIZE
    y_p, ret_p, k_p, v_p, sr_p, si_p = _trunk(
        x_prompt, mods[:, 0:n_prompt], np.arange(seq_len), None, None, None, p)
    paged = dict(cache_k=cache_k, cache_v=cache_v, page_table=page_table)
    y_s, ret_s, k_s, v_s, sr_s, si_s = _trunk(
        x_sample, mods[:, n_prompt:n_all], past_len + np.arange(x_sample.shape[1]),
        state_ret, (state_ssm_re, state_ssm_im), paged, p)
    return (y_p, y_s, ret_p, ret_s, k_p, v_p, k_s, v_s, sr_p, si_p, sr_s, si_s)
```

```python
import functools
import math

import jax
import jax.numpy as jnp
import numpy as np
from jax import lax
from jax.experimental import pallas as pl
from jax.experimental.pallas import tpu as pltpu

F32 = jnp.float32
BF16 = jnp.bfloat16

D_MODEL = 1024
H_RET = 4
DK_RET = 128
DV_RET = 128
RET_CHUNK = 128
ROPE_BASE = 10000.0
H_DIFF = 4
DH_DIFF = 64
DV_DIFF = 128
PAGE_SIZE = 128
W_RET = H_RET * DK_RET
W_DIFF = H_DIFF * 2 * DH_DIFF
IN_EVEN = 4 * W_RET + 3 * W_DIFF
SSM_GROUP = 16
N_GROUPS = D_MODEL // SSM_GROUP
SSM_STATE = 64
N_PAIRS = N_GROUPS // 2
STEP_PITCH = N_PAIRS + 8
D_FF = 2816
N_EXPERTS = 8
D_FF_EXPERT = 3584
N_ADA = 6
NORM_EPS = 1e-6
LANES = 128
NEG_BIG = -1e30

VMEM_LIMIT_BYTES = 56 * 1024 * 1024


def _params(*sem):
    return pltpu.CompilerParams(dimension_semantics=sem, vmem_limit_bytes=VMEM_LIMIT_BYTES)


def _bdot(a, b):
    return jnp.dot(a, b, preferred_element_type=F32)


def _dot_nt(a, b):
    return lax.dot_general(a, b, (((1,), (1,)), ((), ())), preferred_element_type=F32)


def _dot_tn(a, b):
    return lax.dot_general(a, b, (((0,), (0,)), ((), ())), preferred_element_type=F32)


def _rms(x):
    return x * lax.rsqrt(jnp.mean(x * x, axis=-1, keepdims=True) + NORM_EPS)


def _modulate(x, g, shift, scale):
    return (_rms(x) * g) * (1.0 + scale) + shift


def _hi_lo(x):
    bits = lax.bitcast_convert_type(x, jnp.uint32) & jnp.uint32(0xFFFF0000)
    hi = lax.bitcast_convert_type(bits, F32)
    return hi.astype(BF16), (x - hi).astype(BF16)


def _split_dot(x, w_hi, w_lo):
    x_hi, x_lo = _hi_lo(x)
    return _bdot(x_hi, w_hi) + _bdot(x_lo, w_hi) + _bdot(x_hi, w_lo)


def _mm(x, w_refs, rows=slice(None)):
    if len(w_refs) == 1:
        return _bdot(x.astype(BF16), w_refs[0][rows, :])
    return _split_dot(x.astype(F32), w_refs[0][rows, :], w_refs[1][rows, :])


def _wsplit(w, precise):
    w = w.astype(F32)
    if not precise:
        return (w.astype(BF16),)
    return _hi_lo(w)


def _ada_kernel(c_ref, w_ref, b_ref, o_ref):
    w_hi, w_lo = _hi_lo(w_ref[...])
    o_ref[...] = _split_dot(jax.nn.silu(c_ref[...]), w_hi, w_lo) + b_ref[...]


def _ada(c_all, w_ada, b_ada):
    depth, _, n = w_ada.shape
    rows = c_all.shape[0]
    tn = 1536
    return pl.pallas_call(
        _ada_kernel,
        grid=(depth, n // tn),
        in_specs=[
            pl.BlockSpec((rows, D_MODEL), lambda l, j: (0, 0)),
            pl.BlockSpec((None, D_MODEL, tn), lambda l, j: (l, 0, j)),
            pl.BlockSpec((None, 1, tn), lambda l, j: (l, 0, j)),
        ],
        out_specs=pl.BlockSpec((None, rows, tn), lambda l, j: (l, 0, j)),
        out_shape=jax.ShapeDtypeStruct((depth, rows, n), F32),
        compiler_params=_params("parallel", "parallel"),
        name="ada",
    )(c_all, w_ada, b_ada.reshape(depth, 1, n))


def _vec_spec(arr, tm, tiles_per_seq):
    if arr.shape[1] == 1:
        return pl.BlockSpec((None, 1, arr.shape[2]), lambda i, *_: (i // tiles_per_seq, 0, 0))
    return pl.BlockSpec((None, tm, arr.shape[2]), lambda i, *_: (0, 0, 0))


def _pos_spec(arr, tm, tiles_per_seq):
    if arr.shape[0] == 1:
        return pl.BlockSpec((1, arr.shape[1]), lambda i, *_: (0, 0))
    return pl.BlockSpec((tm, arr.shape[1]), lambda i, *_: (i % tiles_per_seq, 0))


def _full_spec(arr):
    nd = arr.ndim
    return pl.BlockSpec(arr.shape, lambda *_: (0,) * nd)


def _in_even_kernel(x_ref, g_ref, sh_ref, sc_ref, cos_ref, sin_ref, gq_ref, gk_ref,
                    seg_ref, *rest, n_w):
    w_refs = rest[:n_w]
    ret_ref, qb_ref, kb_ref, vb_ref = rest[n_w:n_w + 4]
    vt_ref, kt_ref = rest[n_w + 4:n_w + 6] if len(rest) > n_w + 4 else (None, None)
    y = _mm(_modulate(x_ref[...], g_ref[...], sh_ref[...], sc_ref[...]), w_refs)
    cos = cos_ref[...]
    sin = sin_ref[...]
    for part in range(2):
        for hd in range(H_RET):
            lo = part * W_RET + hd * DK_RET
            blk = y[:, lo:lo + DK_RET]
            rot = blk * cos + pltpu.roll(blk, DK_RET // 2, axis=1) * sin
            if part == 1:
                rot = rot * (DK_RET ** -0.5)
            ret_ref[:, lo:lo + DK_RET] = rot
    ret_ref[:, 2 * W_RET:4 * W_RET] = y[:, 2 * W_RET:4 * W_RET]
    seg = seg_ref[...]
    base = 4 * W_RET

    def qk_norm(blk, g):
        sq_hi, sq_lo = _hi_lo(blk * blk)
        ms = (_bdot(sq_hi, seg) + _bdot(sq_lo, seg)) * (1.0 / DH_DIFF)
        return blk * lax.rsqrt(ms + NORM_EPS) * g

    qb_ref[...] = qk_norm(y[:, base:base + W_DIFF], gq_ref[...]).astype(qb_ref.dtype)
    kn = qk_norm(y[:, base + W_DIFF:base + 2 * W_DIFF], gk_ref[...])
    kb_ref[...] = kn
    v = y[:, base + 2 * W_DIFF:base + 3 * W_DIFF]
    vb_ref[...] = v
    if vt_ref is not None:
        vt_ref[...] = v.T.astype(BF16)
        kt_ref[...] = kn.T


def _in_even(x2d, g, sh, sc, cos_t, sin_t, ws, gq, gk, seg, tm, tiles_per_seq, q_dtype, emit_vt):
    m = x2d.shape[0]
    row = lambda n: pl.BlockSpec((tm, n), lambda i: (i, 0))
    out_specs = [row(4 * W_RET), row(W_DIFF), row(W_DIFF), row(W_DIFF)]
    out_shape = [
        jax.ShapeDtypeStruct((m, 4 * W_RET), F32),
        jax.ShapeDtypeStruct((m, W_DIFF), q_dtype),
        jax.ShapeDtypeStruct((m, W_DIFF), F32),
        jax.ShapeDtypeStruct((m, W_DIFF), F32),
    ]
    if emit_vt:
        out_specs.append(pl.BlockSpec((W_DIFF, tm), lambda i: (0, i)))
        out_shape.append(jax.ShapeDtypeStruct((W_DIFF, m), BF16))
        out_specs.append(pl.BlockSpec((None, W_DIFF, tm), lambda i: (i // tiles_per_seq, 0, i % tiles_per_seq)))
        out_shape.append(jax.ShapeDtypeStruct((m // (tm * tiles_per_seq), W_DIFF, tm * tiles_per_seq), F32))
    return pl.pallas_call(
        functools.partial(_in_even_kernel, n_w=len(ws)),
        grid=(m // tm,),
        in_specs=[
            row(D_MODEL), _full_spec(g), _vec_spec(sh, tm, tiles_per_seq), _vec_spec(sc, tm, tiles_per_seq),
            _pos_spec(cos_t, tm, tiles_per_seq), _pos_spec(sin_t, tm, tiles_per_seq),
            _full_spec(gq), _full_spec(gk), _full_spec(seg),
        ] + [_full_spec(w) for w in ws],
        out_specs=out_specs,
        out_shape=out_shape,
        compiler_params=_params("parallel"),
        name="in_even",
    )(x2d, g, sh, sc, cos_t, sin_t, gq, gk, seg, *ws)


def _ret_kernel(q_ref, k_ref, v_ref, ga_ref, dm_ref, qd_ref, kd_ref, cd_ref, g_ref,
                o_ref, sfin_ref, s_scr, *, n_chunks):
    j = pl.program_id(1)

    @pl.when(j == 0)
    def _():
        s_scr[...] = jnp.zeros_like(s_scr)

    g = g_ref[...]

    def body(c, carry):
        r = pl.multiple_of(c * RET_CHUNK, RET_CHUNK)
        for hd in range(H_RET):
            sl = slice(hd * DK_RET, (hd + 1) * DK_RET)
            q = q_ref[pl.ds(r, RET_CHUNK), sl]
            k = k_ref[pl.ds(r, RET_CHUNK), sl]
            vb = v_ref[pl.ds(r, RET_CHUNK), sl].astype(BF16)
            s = s_scr[hd]
            att = _dot_nt(q.astype(BF16), k.astype(BF16)) * dm_ref[hd]
            o = _bdot(att.astype(BF16), vb) + _bdot((q * qd_ref[hd]).astype(BF16), s.astype(BF16))
            s_scr[hd] = cd_ref[hd] * s + _dot_tn((k * kd_ref[hd]).astype(BF16), vb)
            on = _rms(o) * g
            o_ref[pl.ds(r, RET_CHUNK), sl] = (on * jax.nn.silu(ga_ref[pl.ds(r, RET_CHUNK), sl])).astype(BF16)
        return carry

    lax.fori_loop(0, n_chunks, body, 0)

    @pl.when(j == pl.num_programs(1) - 1)
    def _():
        sfin_ref[...] = s_scr[...]


def _retention_prompt(ret_in, bsz, seq_len, dm, qd, kd, cd, g_ret):
    n_chunks = min(8, seq_len // RET_CHUNK)
    rows = n_chunks * RET_CHUNK
    nj = seq_len // rows
    col = lambda c: pl.BlockSpec((rows, W_RET), lambda b, j: (b * nj + j, c))
    return pl.pallas_call(
        functools.partial(_ret_kernel, n_chunks=n_chunks),
        grid=(bsz, nj),
        in_specs=[col(0), col(1), col(2), col(3), _full_spec(dm), _full_spec(qd), _full_spec(kd),
                  _full_spec(cd), _full_spec(g_ret)],
        out_specs=[
            pl.BlockSpec((rows, W_RET), lambda b, j: (b * nj + j, 0)),
            pl.BlockSpec((None, H_RET, DK_RET, DV_RET), lambda b, j: (b, 0, 0, 0)),
        ],
        out_shape=[
            jax.ShapeDtypeStruct((bsz * seq_len, W_RET), BF16),
            jax.ShapeDtypeStruct((bsz, H_RET, DK_RET, DV_RET), F32),
        ],
        scratch_shapes=[pltpu.VMEM((H_RET, DK_RET, DV_RET), F32)],
        compiler_params=_params("parallel", "arbitrary"),
        name="retention_prompt",
    )(ret_in, ret_in, ret_in, ret_in, dm, qd, kd, cd, g_ret)


def _ret_step_kernel(ret_ref, s_ref, qd_ref, cd_ref, g_ref, o_ref, snew_ref):
    g = g_ref[...]
    rows = lax.broadcasted_iota(jnp.int32, (DK_RET, DK_RET), 0)
    cols = lax.broadcasted_iota(jnp.int32, (DK_RET, DK_RET), 1)
    diag = rows == cols
    for hd in range(H_RET):
        lo = hd * DK_RET
        q = ret_ref[:, lo:lo + DK_RET]
        k = ret_ref[:, W_RET + lo:W_RET + lo + DK_RET]
        v = ret_ref[:, 2 * W_RET + lo:2 * W_RET + lo + DK_RET]
        ga = ret_ref[:, 3 * W_RET + lo:3 * W_RET + lo + DK_RET]
        s = s_ref[hd]
        s_hi, s_lo = _hi_lo(s)
        att = jnp.sum(q * k, axis=-1, keepdims=True)
        q8 = jnp.broadcast_to(q * qd_ref[hd], (8, DK_RET))
        o = att * v + _split_dot(q8, s_hi, s_lo)[0:1, :]
        k_diag = jnp.where(diag, jnp.broadcast_to(k, (DK_RET, DK_RET)), 0.0)
        v_rows = jnp.broadcast_to(v, (DK_RET, DV_RET))
        v_hi, v_lo = _hi_lo(v_rows)
        snew_ref[hd] = cd_ref[hd] * s + _split_dot(k_diag, v_hi, v_lo)
        on = _rms(o) * g
        o_ref[:, lo:lo + DK_RET] = on * jax.nn.silu(ga)


def _retention_step(ret_in, state, qd1, cd, g_ret):
    bsz = state.shape[0]
    ret3 = ret_in.reshape(bsz, 1, 4 * W_RET)
    o, s_new = pl.pallas_call(
        _ret_step_kernel,
        grid=(bsz,),
        in_specs=[
            pl.BlockSpec((None, 1, 4 * W_RET), lambda b: (b, 0, 0)),
            pl.BlockSpec((None, H_RET, DK_RET, DV_RET), lambda b: (b, 0, 0, 0)),
            _full_spec(qd1), _full_spec(cd), _full_spec(g_ret),
        ],
        out_specs=[
            pl.BlockSpec((None, 1, W_RET), lambda b: (b, 0, 0)),
            pl.BlockSpec((None, H_RET, DK_RET, DV_RET), lambda b: (b, 0, 0, 0)),
        ],
        out_shape=[
            jax.ShapeDtypeStruct((bsz, 1, W_RET), F32),
            jax.ShapeDtypeStruct((bsz, H_RET, DK_RET, DV_RET), F32),
        ],
        compiler_params=_params("parallel"),
        name="retention_step",
    )(ret3, state, qd1, cd, g_ret)
    return o.reshape(bsz, W_RET), s_new


def _lambda_value(lam_ref, lam_init):
    lv = lam_ref[...]
    s1 = jnp.sum(lv[0:1, :] * lv[1:2, :], axis=-1, keepdims=True)
    s2 = jnp.sum(lv[2:3, :] * lv[3:4, :], axis=-1, keepdims=True)
    return jnp.exp(s1) - jnp.exp(s2) + lam_init


Q_GROUP = 512
LOG2E = 1.4426950408889634


def _diff_kernel(qi_ref, ki_ref, q_ref, k_ref, vt_ref, lam_ref, g_ref, o_ref,
                 qs_scr, k_scr, m_scr, l_scr, acc_scr, *, t, lam_init):
    p = pl.program_id(2)
    qi = qi_ref[p]
    ki = ki_ref[p]
    qg = min(Q_GROUP, t)
    ng = t // qg

    @pl.when(ki == 0)
    def _():
        q = q_ref[...].astype(F32) * (DH_DIFF ** -0.5 * LOG2E)
        lane = lax.broadcasted_iota(jnp.int32, q.shape, 1)
        q_maps = (jnp.where(lane < DH_DIFF, q, 0.0).astype(BF16), jnp.where(lane >= DH_DIFF, q, 0.0).astype(BF16))
        for mp in range(2):
            for gi in range(ng):
                qs_scr[mp * ng + gi] = q_maps[mp][gi * qg:(gi + 1) * qg, :]
        m_scr[...] = jnp.full_like(m_scr, NEG_BIG)
        l_scr[...] = jnp.zeros_like(l_scr)
        acc_scr[...] = jnp.zeros_like(acc_scr)

    k_scr[...] = k_ref[...].astype(BF16)

    def group(g, n_keys, q_off):
        s = _dot_nt(k_scr[0:n_keys, :], qs_scr[g])
        if q_off is not None:
            key = lax.broadcasted_iota(jnp.int32, s.shape, 0)
            qpos = lax.broadcasted_iota(jnp.int32, s.shape, 1) + q_off
            s = jnp.where(key <= qpos, s, NEG_BIG)
        m_prev = m_scr[g]
        m_new = jnp.maximum(m_prev, jnp.max(s, axis=0, keepdims=True))
        alpha = jnp.exp2(m_prev - m_new)
        pr = jnp.exp2(s - m_new)
        l_scr[g] = alpha * l_scr[g] + jnp.sum(pr, axis=0, keepdims=True)
        acc_scr[g] = alpha * acc_scr[g] + _bdot(vt_ref[:, 0:n_keys], pr.astype(BF16))
        m_scr[g] = m_new

    @pl.when(ki < qi)
    def _():
        def body(i, carry):
            ga, gb = 2 * i, 2 * i + 1
            k = k_scr[...]
            s_a = _dot_nt(k, qs_scr[ga])
            s_b = _dot_nt(k, qs_scr[gb])
            for g, s in ((ga, s_a), (gb, s_b)):
                m_prev = m_scr[g]
                m_new = jnp.maximum(m_prev, jnp.max(s, axis=0, keepdims=True))
                alpha = jnp.exp2(m_prev - m_new)
                pr = jnp.exp2(s - m_new)
                l_scr[g] = alpha * l_scr[g] + jnp.sum(pr, axis=0, keepdims=True)
                acc_scr[g] = alpha * acc_scr[g] + _bdot(vt_ref[...], pr.astype(BF16))
                m_scr[g] = m_new
            return carry
        lax.fori_loop(0, ng, body, 0)

    @pl.when(ki == qi)
    def _():
        for g in range(2 * ng):
            q_off = (g % ng) * qg
            group(g, q_off + qg, q_off)
        lam = _lambda_value(lam_ref, lam_init)
        g_col = g_ref[...]
        for c in range(t // LANES):
            g1, lanes = (c * LANES) // qg, slice((c * LANES) % qg, (c * LANES) % qg + LANES)
            a1 = acc_scr[g1][:, lanes] * (1.0 / l_scr[g1][:, lanes])
            a2 = acc_scr[ng + g1][:, lanes] * (1.0 / l_scr[ng + g1][:, lanes])
            o = a1 - lam * a2
            o = o * lax.rsqrt(jnp.mean(o * o, axis=0, keepdims=True) + NORM_EPS) * g_col * (1.0 - lam_init)
            o_ref[c * LANES:(c + 1) * LANES, :] = o.T.astype(BF16)


def _diff_prompt(qb, kb, vt, lam_vec, g_d, bsz, seq_len, lam_init):
    t = min(1024, seq_len)
    nq = seq_len // t
    pairs = [(i, j) for i in range(nq) for j in range(i + 1)]
    qi = jnp.asarray([a for a, _ in pairs], jnp.int32)
    ki = jnp.asarray([b for _, b in pairs], jnp.int32)
    g_col = g_d.reshape(DV_DIFF, 1)
    qg = min(Q_GROUP, t)
    n_grp = 2 * t // qg
    grid_spec = pltpu.PrefetchScalarGridSpec(
        num_scalar_prefetch=2,
        grid=(bsz, H_DIFF, len(pairs)),
        in_specs=[
            pl.BlockSpec((t, 2 * DH_DIFF), lambda b, h, p, qi, ki: (b * nq + qi[p], h)),
            pl.BlockSpec((t, 2 * DH_DIFF), lambda b, h, p, qi, ki: (b * nq + ki[p], h)),
            pl.BlockSpec((DV_DIFF, t), lambda b, h, p, qi, ki: (h, b * nq + ki[p])),
            pl.BlockSpec(lam_vec.shape, lambda b, h, p, qi, ki: (0, 0)),
            pl.BlockSpec(g_col.shape, lambda b, h, p, qi, ki: (0, 0)),
        ],
        out_specs=pl.BlockSpec((t, DV_DIFF), lambda b, h, p, qi, ki: (b * nq + qi[p], h)),
        scratch_shapes=[
            pltpu.VMEM((n_grp, qg, 2 * DH_DIFF), BF16),
            pltpu.VMEM((t, 2 * DH_DIFF), BF16),
            pltpu.VMEM((n_grp, 1, qg), F32),
            pltpu.VMEM((n_grp, 1, qg), F32),
            pltpu.VMEM((n_grp, DV_DIFF, qg), F32),
        ],
    )
    return pl.pallas_call(
        functools.partial(_diff_kernel, t=t, lam_init=lam_init),
        grid_spec=grid_spec,
        out_shape=jax.ShapeDtypeStruct((bsz * seq_len, H_DIFF * DV_DIFF), BF16),
        compiler_params=_params("parallel", "parallel", "arbitrary"),
        name="diff_prompt",
    )(qi, ki, qb, kb, vt, lam_vec, g_col)


PAGES_PER_STEP = 16
PAGES_PER_UPDATE = 4
N_MAPS = 2 * H_DIFF


def _group_sums(prod):
    return jnp.concatenate(
        [jnp.sum(prod[g * DH_DIFF:(g + 1) * DH_DIFF, :], axis=0, keepdims=True) for g in range(N_MAPS)], axis=0)


def _diff_step_kernel(pt_ref, q_ref, kn_ref, vn_ref, lam_ref, g_ref, *rest, pps, lam_init):
    k_refs = rest[:pps]
    v_refs = rest[pps:2 * pps]
    o_ref = rest[2 * pps]
    qb_scr, m_scr, l_scr, acc_scr = rest[2 * pps + 1:]
    j = pl.program_id(1)
    scale = DH_DIFF ** -0.5

    @pl.when(j == 0)
    def _():
        qb_scr[...] = jnp.broadcast_to(q_ref[...], (W_DIFF, PAGE_SIZE))
        m_scr[...] = jnp.full_like(m_scr, NEG_BIG)
        l_scr[...] = jnp.zeros_like(l_scr)
        acc_scr[...] = jnp.zeros_like(acc_scr)

    row_head = lax.broadcasted_iota(jnp.int32, (N_MAPS, DV_DIFF), 0) // 2
    for first in range(0, pps, PAGES_PER_UPDATE):
        pages = range(first, first + PAGES_PER_UPDATE)
        scores = {pg: _group_sums(k_refs[pg][...] * qb_scr[...]) * scale for pg in pages}
        s_max = scores[first]
        for pg in pages:
            s_max = jnp.maximum(s_max, scores[pg])
        m_prev = m_scr[...]
        m_new = jnp.maximum(m_prev, jnp.max(s_max, axis=-1, keepdims=True))
        alpha = jnp.exp(m_prev - m_new)
        p_sum = jnp.zeros((N_MAPS, PAGE_SIZE), F32)
        pv = jnp.zeros((N_MAPS, DV_DIFF), F32)
        for pg in pages:
            pr = jnp.exp(scores[pg] - m_new)
            p_sum = p_sum + pr
            p_bits = lax.bitcast_convert_type(pr, jnp.uint32) & jnp.uint32(0xFFFF0000)
            p_top = lax.bitcast_convert_type(p_bits, F32)
            p_hi = p_top.astype(BF16)
            p_both = jnp.concatenate([p_top, pr - p_top], axis=0).astype(BF16)
            for hd in range(H_DIFF):
                v_hi, v_lo = _hi_lo(v_refs[pg][pl.ds(hd, PAGE_SIZE, stride=H_DIFF), :])
                r = _bdot(p_both, jnp.concatenate([v_hi, v_lo], axis=1))
                r = r[0:N_MAPS, 0:DV_DIFF] + r[N_MAPS:2 * N_MAPS, 0:DV_DIFF] + r[0:N_MAPS, DV_DIFF:2 * DV_DIFF]
                pv = pv + jnp.where(row_head == hd, r, 0.0)
        l_scr[...] = alpha * l_scr[...] + jnp.sum(p_sum, axis=-1, keepdims=True)
        acc_scr[...] = alpha * acc_scr[...] + pv
        m_scr[...] = m_new

    @pl.when(j == pl.num_programs(1) - 1)
    def _():
        s = _group_sums(q_ref[...] * kn_ref[...]) * scale
        m_old = m_scr[...]
        m_fin = jnp.maximum(m_old, s)
        a_fin = jnp.exp(m_old - m_fin)
        pr = jnp.exp(s - m_fin)
        l_fin = a_fin * l_scr[...] + pr
        a = (a_fin * acc_scr[...] + pr * vn_ref[...]) / l_fin
        lam = _lambda_value(lam_ref, lam_init)
        g = g_ref[...]
        for hd in range(H_DIFF):
            d = a[2 * hd:2 * hd + 1, :] - lam * a[2 * hd + 1:2 * hd + 2, :]
            o_ref[hd:hd + 1, :] = _rms(d) * g * (1.0 - lam_init)


def _diff_step(qb, kb_new, vb_new, cache_k, cache_v, layer_idx, page_table, lam_vec, g_d, lam_init):
    bsz, n_pages = page_table.shape
    n_layers, n_pool = cache_k.shape[0:2]
    pps = math.gcd(n_pages, PAGES_PER_STEP)
    assert pps % PAGES_PER_UPDATE == 0
    ck = jnp.transpose(cache_k, (0, 1, 3, 4, 5, 2)).reshape(n_layers * n_pool, W_DIFF, PAGE_SIZE)
    cv = cache_v.reshape(n_layers * n_pool, PAGE_SIZE * H_DIFF, DV_DIFF)
    page_table = page_table + layer_idx * n_pool
    col_spec = pl.BlockSpec((None, W_DIFF, 1), lambda b, j, pt: (b, 0, 0))
    v_new = jnp.repeat(vb_new.reshape(bsz, H_DIFF, DV_DIFF), 2, axis=1)

    def page_spec(pg):
        return pl.BlockSpec((None, W_DIFF, PAGE_SIZE), lambda b, j, pt, pg=pg: (pt[b, j * pps + pg], 0, 0))

    grid_spec = pltpu.PrefetchScalarGridSpec(
        num_scalar_prefetch=1,
        grid=(bsz, n_pages // pps),
        in_specs=[col_spec, col_spec,
                  pl.BlockSpec((None, N_MAPS, DV_DIFF), lambda b, j, pt: (b, 0, 0)),
                  pl.BlockSpec(lam_vec.shape, lambda b, j, pt: (0, 0)),
                  pl.BlockSpec(g_d.shape, lambda b, j, pt: (0, 0))]
                 + [page_spec(pg) for pg in range(pps)] + [page_spec(pg) for pg in range(pps)],
        out_specs=pl.BlockSpec((None, H_DIFF, DV_DIFF), lambda b, j, pt: (b, 0, 0)),
        scratch_shapes=[
            pltpu.VMEM((W_DIFF, PAGE_SIZE), F32),
            pltpu.VMEM((N_MAPS, 1), F32),
            pltpu.VMEM((N_MAPS, 1), F32),
            pltpu.VMEM((N_MAPS, DV_DIFF), F32),
        ],
    )
    o = pl.pallas_call(
        functools.partial(_diff_step_kernel, pps=pps, lam_init=lam_init),
        grid_spec=grid_spec,
        out_shape=jax.ShapeDtypeStruct((bsz, H_DIFF, DV_DIFF), F32),
        compiler_params=_params("parallel", "arbitrary"),
        name="diff_step",
    )(page_table, qb.reshape(bsz, W_DIFF, 1), kb_new.reshape(bsz, W_DIFF, 1), v_new, lam_vec, g_d,
      *([ck] * pps), *([cv] * pps))
    return o.reshape(bsz, W_DIFF)


def _out_even_kernel(a_ref, b_ref, x_ref, gt_ref, *rest):
    w_refs = rest[:-1]
    o_ref = rest[-1]
    m = _mm(a_ref[...], w_refs, slice(0, W_RET)) + _mm(b_ref[...], w_refs, slice(W_RET, 2 * W_RET))
    o_ref[...] = x_ref[...] + gt_ref[...] * m


def _out_even(o_ret, o_diff, ws, x2d, gt, tm, tiles_per_seq):
    m = x2d.shape[0]
    row = lambda n: pl.BlockSpec((tm, n), lambda i: (i, 0))
    return pl.pallas_call(
        _out_even_kernel,
        grid=(m // tm,),
        in_specs=[row(W_RET), row(W_RET), row(D_MODEL), _vec_spec(gt, tm, tiles_per_seq)]
                 + [_full_spec(w) for w in ws],
        out_specs=row(D_MODEL),
        out_shape=jax.ShapeDtypeStruct((m, D_MODEL), F32),
        compiler_params=_params("parallel"),
        name="out_even",
    )(o_ret, o_diff, x2d, gt, *ws)


def _ffn_kernel(x_ref, g_ref, sh_ref, sc_ref, gt_ref, *rest, n_w):
    w1, w3, w2 = rest[0:n_w], rest[n_w:2 * n_w], rest[2 * n_w:3 * n_w]
    o_ref = rest[3 * n_w]
    x = x_ref[...]
    h = _modulate(x, g_ref[...], sh_ref[...], sc_ref[...])
    if n_w == 1:
        h = h.astype(BF16)
    u = jax.nn.silu(_mm(h, w1)) * _mm(h, w3)
    o_ref[...] = x + gt_ref[...] * _mm(u, w2)


def _ffn(x2d, g, sh, sc, gt, w1s, w3s, w2s, tm, tiles_per_seq):
    m = x2d.shape[0]
    row = pl.BlockSpec((tm, D_MODEL), lambda i: (i, 0))
    vs = lambda a: _vec_spec(a, tm, tiles_per_seq)
    ws = (*w1s, *w3s, *w2s)
    return pl.pallas_call(
        functools.partial(_ffn_kernel, n_w=len(w1s)),
        grid=(m // tm,),
        in_specs=[row, _full_spec(g), vs(sh), vs(sc), vs(gt)] + [_full_spec(w) for w in ws],
        out_specs=row,
        out_shape=jax.ShapeDtypeStruct((m, D_MODEL), F32),
        compiler_params=_params("parallel"),
        name="ffn",
    )(x2d, g, sh, sc, gt, *ws)


def _modmm_kernel(x_ref, g_ref, sh_ref, sc_ref, *rest):
    w_refs = rest[:-1]
    o_ref = rest[-1]
    o_ref[...] = _mm(_modulate(x_ref[...], g_ref[...], sh_ref[...], sc_ref[...]), w_refs)


def _modmm(x2d, g, sh, sc, ws, tm, tiles_per_seq):
    m = x2d.shape[0]
    n = ws[0].shape[1]
    vs = lambda a: _vec_spec(a, tm, tiles_per_seq)
    return pl.pallas_call(
        _modmm_kernel,
        grid=(m // tm,),
        in_specs=[pl.BlockSpec((tm, D_MODEL), lambda i: (i, 0)), _full_spec(g), vs(sh), vs(sc)]
                 + [_full_spec(w) for w in ws],
        out_specs=pl.BlockSpec((tm, n), lambda i: (i, 0)),
        out_shape=jax.ShapeDtypeStruct((m, n), F32),
        compiler_params=_params("parallel"),
        name="modmm",
    )(x2d, g, sh, sc, *ws)


def _mm_at(x, w_refs, idx):
    if len(w_refs) == 1:
        return _bdot(x.astype(BF16), w_refs[0][idx])
    return _split_dot(x, w_refs[0][idx], w_refs[1][idx])


def _s5_input_drive(u, bmat_refs, j):
    return _mm_at(u[:, j * LANES:(j + 1) * LANES], bmat_refs, j)


def _s5_output_block(h_re_of, h_im_of, cmat_refs, jb):
    acc = None
    for q in range(4):
        r = 4 * jb + q
        t = (_mm_at(h_re_of(r), cmat_refs, (r, slice(0, LANES), slice(None)))
             + _mm_at(h_im_of(r), cmat_refs, (r, slice(LANES, 2 * LANES), slice(None))))
        acc = t if acc is None else acc + t
    return acc


def _s5_prompt_kernel(u_ref, are_ref, aim_ref, bmat_ref, cmat_ref, d_ref, z_ref, hre_ref, him_ref,
                      sre, sim, st_re, st_im, *, tc):
    j = pl.program_id(1)

    @pl.when(j == 0)
    def _():
        st_re[...] = jnp.zeros_like(st_re)
        st_im[...] = jnp.zeros_like(st_im)

    u = u_ref[...]
    half = 4 * LANES
    for jb in range(8):
        res = _s5_input_drive(u, (bmat_ref,), jb)
        for q in range(4):
            r = 4 * jb + q
            sre[pl.ds(r, tc, stride=STEP_PITCH), :] = res[:, q * LANES:(q + 1) * LANES]
            sim[pl.ds(r, tc, stride=STEP_PITCH), :] = res[:, half + q * LANES:half + (q + 1) * LANES]

    a_re = are_ref[...]
    a_im = aim_ref[...]

    def step(t, carry):
        h_re, h_im = carry
        base = pl.multiple_of(t * STEP_PITCH, 8)
        n_re = a_re * h_re - a_im * h_im + sre[pl.ds(base, N_PAIRS), :]
        n_im = a_re * h_im + a_im * h_re + sim[pl.ds(base, N_PAIRS), :]
        sre[pl.ds(base, N_PAIRS), :] = n_re
        sim[pl.ds(base, N_PAIRS), :] = n_im
        return n_re, n_im

    h_re, h_im = lax.fori_loop(0, tc, step, (st_re[...], st_im[...]))
    st_re[...] = h_re
    st_im[...] = h_im

    for jb in range(8):
        y = _s5_output_block(lambda r: sre[pl.ds(r, tc, stride=STEP_PITCH), :],
                             lambda r: sim[pl.ds(r, tc, stride=STEP_PITCH), :], (cmat_ref,), jb)
        sl = slice(jb * LANES, (jb + 1) * LANES)
        y = y + d_ref[:, sl] * u[:, sl]
        z_ref[:, sl] = jax.nn.gelu(y, approximate=True).astype(BF16)

    @pl.when(j == pl.num_programs(1) - 1)
    def _():
        hre_ref[...] = h_re
        him_ref[...] = h_im


def _s5_prompt(u2d, bsz, seq_len, a_re, a_im, bmat, cmat, d_row):
    tc = min(256, seq_len)
    nj = seq_len // tc
    state_spec = pl.BlockSpec((None, N_PAIRS, LANES), lambda b, j: (b, 0, 0))
    return pl.pallas_call(
        functools.partial(_s5_prompt_kernel, tc=tc),
        grid=(bsz, nj),
        in_specs=[pl.BlockSpec((tc, D_MODEL), lambda b, j: (b * nj + j, 0)),
                  _full_spec(a_re), _full_spec(a_im), _full_spec(bmat), _full_spec(cmat), _full_spec(d_row)],
        out_specs=[pl.BlockSpec((tc, D_MODEL), lambda b, j: (b * nj + j, 0)), state_spec, state_spec],
        out_shape=[
            jax.ShapeDtypeStruct((bsz * seq_len, D_MODEL), BF16),
            jax.ShapeDtypeStruct((bsz, N_PAIRS, LANES), F32),
            jax.ShapeDtypeStruct((bsz, N_PAIRS, LANES), F32),
        ],
        scratch_shapes=[
            pltpu.VMEM((tc * STEP_PITCH, LANES), F32),
            pltpu.VMEM((tc * STEP_PITCH, LANES), F32),
            pltpu.VMEM((N_PAIRS, LANES), F32),
            pltpu.VMEM((N_PAIRS, LANES), F32),
        ],
        compiler_params=_params("parallel", "arbitrary"),
        name="s5_prompt",
    )(u2d, a_re, a_im, bmat, cmat, d_row)


def _s5_step_kernel(u_ref, h0re_ref, h0im_ref, are_ref, aim_ref, bhi_ref, blo_ref, chi_ref, clo_ref, d_ref,
                    z_ref, hre_ref, him_ref):
    u = u_ref[...]
    half = 4 * LANES
    bmat_refs = (bhi_ref, blo_ref)
    cmat_refs = (chi_ref, clo_ref)
    for jb in range(8):
        res = _s5_input_drive(u, bmat_refs, jb)
        sl = slice(jb * half, (jb + 1) * half)
        a_re = are_ref[:, sl]
        a_im = aim_ref[:, sl]
        h_re = h0re_ref[:, sl]
        h_im = h0im_ref[:, sl]
        hre_ref[:, sl] = a_re * h_re - a_im * h_im + res[:, 0:half]
        him_ref[:, sl] = a_re * h_im + a_im * h_re + res[:, half:2 * half]
    for jb in range(8):
        y = _s5_output_block(lambda r: hre_ref[:, r * LANES:(r + 1) * LANES],
                             lambda r: him_ref[:, r * LANES:(r + 1) * LANES], cmat_refs, jb)
        sl = slice(jb * LANES, (jb + 1) * LANES)
        y = y + d_ref[:, sl] * u[:, sl]
        z_ref[:, sl] = jax.nn.gelu(y, approximate=True)


def _s5_step(u2d, h0_re, h0_im, a_re_row, a_im_row, bmats, cmats, d_row):
    bsz = u2d.shape[0]
    n_state = N_GROUPS * SSM_STATE
    args = (u2d, h0_re, h0_im, a_re_row, a_im_row, *bmats, *cmats, d_row)
    return pl.pallas_call(
        _s5_step_kernel,
        grid=(1,),
        in_specs=[_full_spec(a) for a in args],
        out_specs=[pl.BlockSpec((bsz, D_MODEL), lambda i: (0, 0)),
                   pl.BlockSpec((bsz, n_state), lambda i: (0, 0)),
                   pl.BlockSpec((bsz, n_state), lambda i: (0, 0))],
        out_shape=[
            jax.ShapeDtypeStruct((bsz, D_MODEL), F32),
            jax.ShapeDtypeStruct((bsz, n_state), F32),
            jax.ShapeDtypeStruct((bsz, n_state), F32),
        ],
        compiler_params=_params("arbitrary"),
        name="s5_step",
    )(*args)


def _glu_router_kernel(z_ref, x_ref, gt_ref, g_ref, sh_ref, sc_ref, wrh_ref, wrl_ref, br_ref, *rest):
    wg_refs = rest[:-3]
    x_out, h_out, route_out = rest[-3:]
    vg = _mm(z_ref[...], wg_refs)
    val = vg[:, 0:D_MODEL]
    gate = vg[:, D_MODEL:2 * D_MODEL]
    x = x_ref[...] + gt_ref[...] * (val * jax.nn.sigmoid(gate))
    x_out[...] = x
    h = _modulate(x, g_ref[...], sh_ref[...], sc_ref[...])
    h_out[...] = h
    logits = _split_dot(h, wrh_ref[...], wrl_ref[...]) + br_ref[...]
    lane = lax.broadcasted_iota(jnp.int32, logits.shape, 1)
    v1 = jnp.max(logits, axis=-1, keepdims=True)
    i1 = jnp.min(jnp.where(logits == v1, lane, LANES), axis=-1, keepdims=True)
    others = jnp.where(lane == i1, NEG_BIG, logits)
    v2 = jnp.max(others, axis=-1, keepdims=True)
    i2 = jnp.min(jnp.where(others == v2, lane, LANES), axis=-1, keepdims=True)
    e = jnp.exp(v2 - v1)
    g1 = 1.0 / (1.0 + e)
    g2 = e / (1.0 + e)
    route = jnp.where(lane == i1, g1, 0.0) + jnp.where(lane == i2, g2, 0.0)
    flag = jnp.logical_or(lane == i1 + N_EXPERTS, lane == i2 + N_EXPERTS)
    route_out[...] = route + jnp.where(flag, 1.0, 0.0)


def _glu_router(z, wgs, x2d, gt, g, sh, sc, wr_hi, wr_lo, br, tm, tiles_per_seq):
    m = x2d.shape[0]
    row = lambda n: pl.BlockSpec((tm, n), lambda i: (i, 0))
    vs = lambda a: _vec_spec(a, tm, tiles_per_seq)
    return pl.pallas_call(
        _glu_router_kernel,
        grid=(m // tm,),
        in_specs=[row(D_MODEL), row(D_MODEL), vs(gt), _full_spec(g), vs(sh), vs(sc),
                  _full_spec(wr_hi), _full_spec(wr_lo), _full_spec(br)] + [_full_spec(w) for w in wgs],
        out_specs=[row(D_MODEL), row(D_MODEL), row(LANES)],
        out_shape=[
            jax.ShapeDtypeStruct((m, D_MODEL), F32),
            jax.ShapeDtypeStruct((m, D_MODEL), F32),
            jax.ShapeDtypeStruct((m, LANES), F32),
        ],
        compiler_params=_params("parallel"),
        name="glu_router",
    )(z, x2d, gt, g, sh, sc, wr_hi, wr_lo, br, *wgs)


def _moe_kernel(te_ref, nv_ref, xs_ref, w1_ref, w3_ref, w2_ref, o_ref, x_scr, acc_scr):
    i = pl.program_id(0)
    f = pl.program_id(1)
    nf = pl.num_programs(1)
    valid = i < nv_ref[0]

    @pl.when(jnp.logical_and(valid, f == 0))
    def _():
        x_scr[...] = xs_ref[...].astype(BF16)

    @pl.when(valid)
    def _():
        x = x_scr[...]
        a = _bdot(x, w1_ref[...].astype(BF16))
        b = _bdot(x, w3_ref[...].astype(BF16))
        u = (jax.nn.silu(a) * b).astype(BF16)
        contrib = _bdot(u, w2_ref[...].astype(BF16))

        @pl.when(f == 0)
        def _():
            acc_scr[...] = contrib

        @pl.when(f > 0)
        def _():
            acc_scr[...] = acc_scr[...] + contrib

        @pl.when(f == nf - 1)
        def _():
            o_ref[...] = acc_scr[...]

    @pl.when(jnp.logical_and(jnp.logical_not(valid), f == nf - 1))
    def _():
        o_ref[...] = jnp.zeros_like(o_ref)


def _moe_experts(xs, tile_expert, n_valid, w1, w3, w2, tm):
    rows = xs.shape[0]
    n_tiles = rows // tm
    tf = 1792
    nf = D_FF_EXPERT // tf

    def fsel(i, f, nv):
        return jnp.where(i < nv[0], f, nf - 1)

    grid_spec = pltpu.PrefetchScalarGridSpec(
        num_scalar_prefetch=2,
        grid=(n_tiles, nf),
        in_specs=[
            pl.BlockSpec((tm, D_MODEL), lambda i, f, te, nv: (i, 0)),
            pl.BlockSpec((None, D_MODEL, tf), lambda i, f, te, nv: (te[i], 0, fsel(i, f, nv))),
            pl.BlockSpec((None, D_MODEL, tf), lambda i, f, te, nv: (te[i], 0, fsel(i, f, nv))),
            pl.BlockSpec((None, tf, D_MODEL), lambda i, f, te, nv: (te[i], fsel(i, f, nv), 0)),
        ],
        out_specs=pl.BlockSpec((tm, D_MODEL), lambda i, f, te, nv: (i, 0)),
        scratch_shapes=[pltpu.VMEM((tm, D_MODEL), BF16), pltpu.VMEM((tm, D_MODEL), F32)],
    )
    return pl.pallas_call(
        _moe_kernel,
        grid_spec=grid_spec,
        out_shape=jax.ShapeDtypeStruct((rows, D_MODEL), F32),
        compiler_params=_params("parallel", "arbitrary"),
        name="moe_experts",
    )(tile_expert, n_valid, xs, w1, w3, w2)


ROW_DMA_UNROLL = 8


def _row_params(*sem):
    return pltpu.CompilerParams(dimension_semantics=sem, vmem_limit_bytes=VMEM_LIMIT_BYTES,
                                disable_bounds_checks=True)


def _dispatch_kernel(sa_ref, sb_ref, h_ref, xs_in_ref, xs_ref, sem, *, tm):
    del xs_in_ref
    base = pl.program_id(0) * tm

    def copies(t):
        tok = base + t
        src = h_ref.at[pl.ds(t, 1)]
        return (pltpu.make_async_copy(src, xs_ref.at[pl.ds(sa_ref[tok], 1)], sem.at[0]),
                pltpu.make_async_copy(src, xs_ref.at[pl.ds(sb_ref[tok], 1)], sem.at[1]))

    def issue(t, carry):
        for cp in copies(t):
            cp.start()
        return carry

    def drain(t, carry):
        for cp in copies(t):
            cp.wait()
        return carry

    lax.fori_loop(0, tm, issue, 0, unroll=ROW_DMA_UNROLL)
    lax.fori_loop(0, tm, drain, 0, unroll=ROW_DMA_UNROLL)


def _moe_dispatch(h, slot_a, slot_b, n_rows, tm):
    m = h.shape[0]
    grid_spec = pltpu.PrefetchScalarGridSpec(
        num_scalar_prefetch=2,
        grid=(m // tm,),
        in_specs=[pl.BlockSpec((tm, D_MODEL), lambda i, sa, sb: (i, 0)), pl.BlockSpec(memory_space=pl.ANY)],
        out_specs=pl.BlockSpec(memory_space=pl.ANY),
        scratch_shapes=[pltpu.SemaphoreType.DMA((2,))],
    )
    return pl.pallas_call(
        functools.partial(_dispatch_kernel, tm=tm),
        grid_spec=grid_spec,
        out_shape=jax.ShapeDtypeStruct((n_rows, D_MODEL), F32),
        input_output_aliases={3: 0},
        compiler_params=_row_params("arbitrary"),
        name="moe_dispatch",
    )(slot_a, slot_b, h, jnp.zeros((n_rows, D_MODEL), F32))


def _combine_kernel(sa_ref, sb_ref, x_ref, gt_ref, ga_ref, gb_ref, ys_ref, o_ref, buf_a, buf_b, sem, *, tm):
    base = pl.program_id(0) * tm

    def copies(t):
        tok = base + t
        return (pltpu.make_async_copy(ys_ref.at[pl.ds(sa_ref[tok], 1)], buf_a.at[pl.ds(t, 1)], sem.at[0]),
                pltpu.make_async_copy(ys_ref.at[pl.ds(sb_ref[tok], 1)], buf_b.at[pl.ds(t, 1)], sem.at[1]))

    def issue(t, carry):
        for cp in copies(t):
            cp.start()
        return carry

    def drain(t, carry):
        for cp in copies(t):
            cp.wait()
        return carry

    lax.fori_loop(0, tm, issue, 0, unroll=ROW_DMA_UNROLL)
    lax.fori_loop(0, tm, drain, 0, unroll=ROW_DMA_UNROLL)
    o_ref[...] = x_ref[...] + gt_ref[...] * (ga_ref[...] * buf_a[...] + gb_ref[...] * buf_b[...])


def _moe_combine(x2d, gt, gate_a, gate_b, ys, slot_a, slot_b, tm, tiles_per_seq):
    m = x2d.shape[0]
    row = pl.BlockSpec((tm, D_MODEL), lambda i, sa, sb: (i, 0))
    col = pl.BlockSpec((tm, 1), lambda i, sa, sb: (i, 0))
    grid_spec = pltpu.PrefetchScalarGridSpec(
        num_scalar_prefetch=2,
        grid=(m // tm,),
        in_specs=[row, _vec_spec(gt, tm, tiles_per_seq), col, col, pl.BlockSpec(memory_space=pl.ANY)],
        out_specs=row,
        scratch_shapes=[pltpu.VMEM((tm, D_MODEL), F32), pltpu.VMEM((tm, D_MODEL), F32),
                        pltpu.SemaphoreType.DMA((2,))],
    )
    return pl.pallas_call(
        functools.partial(_combine_kernel, tm=tm),
        grid_spec=grid_spec,
        out_shape=jax.ShapeDtypeStruct((m, D_MODEL), F32),
        compiler_params=_row_params("arbitrary"),
        name="moe_combine",
    )(slot_a, slot_b, x2d, gt, gate_a, gate_b, ys)


def _moe(h, route, x2d, gt, w1, w3, w2, tm_rows, tm, tiles_per_seq):
    m = h.shape[0]
    comb = route[:, 0:N_EXPERTS]
    sel = route[:, N_EXPERTS:2 * N_EXPERTS] > 0.5
    seli = sel.astype(jnp.int32)
    counts = jnp.sum(seli, axis=0)
    padded = ((counts + tm_rows - 1) // tm_rows) * tm_rows
    ends = jnp.cumsum(padded)
    offsets = ends - padded
    rank = jnp.cumsum(seli, axis=0) - 1
    pos = offsets[None, :] + rank
    n_rows = ((2 * m + N_EXPERTS * (tm_rows - 1)) // tm_rows + 1) * tm_rows
    n_tiles = n_rows // tm_rows
    tile_start = jnp.arange(n_tiles, dtype=jnp.int32) * tm_rows
    n_valid = (ends[-1] // tm_rows).astype(jnp.int32)
    first_row = jnp.minimum(tile_start, ends[-1] - 1)
    tile_expert = jnp.sum((ends[None, :] <= first_row[:, None]).astype(jnp.int32), axis=1)
    slot_a = jnp.min(jnp.where(sel, pos, n_rows), axis=1).astype(jnp.int32)
    slot_b = jnp.max(jnp.where(sel, pos, -1), axis=1).astype(jnp.int32)
    expert_id = jnp.arange(N_EXPERTS, dtype=jnp.int32)[None, :]
    e_lo = jnp.min(jnp.where(sel, expert_id, N_EXPERTS), axis=1, keepdims=True)
    e_hi = jnp.max(jnp.where(sel, expert_id, -1), axis=1, keepdims=True)
    gate_a = jnp.sum(jnp.where(expert_id == e_lo, comb, 0.0), axis=1, keepdims=True)
    gate_b = jnp.sum(jnp.where(expert_id == e_hi, comb, 0.0), axis=1, keepdims=True)
    tm_dispatch = min(1024, m)
    xs = _moe_dispatch(h, slot_a, slot_b, n_rows, tm_dispatch)
    ys = _moe_experts(xs, tile_expert, n_valid.reshape(1), w1, w3, w2, tm_rows)
    return _moe_combine(x2d, gt, gate_a, gate_b, ys, slot_a, slot_b, tm, tiles_per_seq)


def _retention_constants(chunk):
    log_g = jnp.log1p(-jnp.exp2(-5.0 - jnp.arange(H_RET, dtype=F32)))
    idx = jnp.arange(chunk, dtype=F32)
    rel = idx[:, None] - idx[None, :]
    dm = jnp.where(rel >= 0, jnp.exp(log_g[:, None, None] * jnp.maximum(rel, 0.0)), 0.0)
    q_decay = jnp.exp(log_g[:, None] * (idx[None, :] + 1.0))
    k_decay = jnp.exp(log_g[:, None] * (chunk - 1.0 - idx[None, :]))
    chunk_decay = jnp.exp(log_g * chunk)
    qd = jnp.broadcast_to(q_decay[:, :, None], (H_RET, chunk, DK_RET))
    kd = jnp.broadcast_to(k_decay[:, :, None], (H_RET, chunk, DK_RET))
    cd = jnp.broadcast_to(chunk_decay[:, None, None], (H_RET, 1, DV_RET))
    return dm, qd, kd, cd


def _rotary_tables(pos):
    half = DK_RET // 2
    inv = np.float64(ROPE_BASE) ** (-np.arange(half, dtype=np.float64) / half)
    inv_hi = inv.astype(np.float32)
    inv_lo = (inv - inv_hi.astype(np.float64)).astype(np.float32)
    p = jnp.asarray(pos).astype(F32)[:, None]
    ang = p * jnp.asarray(inv_hi)[None, :] + p * jnp.asarray(inv_lo)[None, :]
    cos = jnp.cos(ang)
    sin = jnp.sin(ang)
    return jnp.concatenate([cos, cos], axis=-1), jnp.concatenate([-sin, sin], axis=-1)


def _s5_matrices(lam_re, lam_im, log_step, b_re, b_im, c_re, c_im, d_skip):
    lr = lam_re.astype(F32)
    li = lam_im.astype(F32)
    step = jnp.exp(log_step.astype(F32))[:, None]
    mag = jnp.exp(lr * step)
    ab_re = mag * jnp.cos(li * step)
    ab_im = mag * jnp.sin(li * step)
    nr = ab_re - 1.0
    ni = ab_im
    den = lr * lr + li * li
    kr = (nr * lr + ni * li) / den
    ki = (ni * lr - nr * li) / den
    br = b_re.astype(F32)
    bi = b_im.astype(F32)
    bb_re = kr[..., None] * br - ki[..., None] * bi
    bb_im = kr[..., None] * bi + ki[..., None] * br
    eye8 = jnp.eye(8, dtype=F32)

    def drive(bb):
        t = bb.reshape(8, 8, SSM_STATE, SSM_GROUP).transpose(0, 1, 3, 2)
        t = t[:, :, :, None, :] * eye8[None, :, None, :, None]
        return t.reshape(8, LANES, 8 * SSM_STATE)

    bmat = jnp.concatenate([drive(bb_re), drive(bb_im)], axis=-1)
    r_idx = jnp.arange(N_PAIRS)[:, None, None]
    gl_idx = jnp.arange(2)[None, :, None]
    gb_idx = jnp.arange(8)[None, None, :]
    onehot = (gb_idx == 2 * (r_idx % 4) + gl_idx).astype(F32)

    def readout(c):
        t = c.astype(F32).reshape(N_PAIRS, 2, SSM_GROUP, SSM_STATE).transpose(0, 1, 3, 2)
        t = t[:, :, :, None, :] * onehot[:, :, None, :, None]
        return t.reshape(N_PAIRS, LANES, LANES)

    cmat = jnp.concatenate([readout(c_re), -readout(c_im)], axis=1)
    return (ab_re.reshape(N_PAIRS, LANES), ab_im.reshape(N_PAIRS, LANES), bmat, cmat,
            d_skip.astype(F32).reshape(1, D_MODEL))


def _trunk(x, mods, pos, ret0, ssm0, paged, p):
    bsz, seq_len, _ = x.shape
    m = bsz * seq_len
    prompt = seq_len > 1
    depth = p["w_ada"].shape[0]
    x2d = x.reshape(m, D_MODEL)
    if prompt:
        tm = min(512, seq_len)
        tps = seq_len // tm
        tm_ffn = min(256, seq_len)
        tps_ffn = seq_len // tm_ffn
        vec = lambda v: v.reshape(bsz, 1, D_MODEL)
    else:
        tm = tm_ffn = m
        tps = tps_ffn = 1
        vec = lambda v: v.reshape(1, bsz, D_MODEL)
    row = lambda v: v.reshape(1, -1).astype(F32)
    wsp = lambda w: _wsplit(w, not prompt)
    rets, ks_new, vs_new, srs, sis = [], [], [], [], []
    for layer in range(depth):
        sh1, sc1, gt1, sh2, sc2, gt2 = [vec(v) for v in jnp.split(mods[layer], N_ADA, axis=-1)]
        i = layer // 2
        if layer % 2 == 0:
            lam_init = 0.8 - 0.6 * math.exp(-0.3 * layer)
            cos_t, sin_t = _rotary_tables(pos)
            seg = (jnp.arange(W_DIFF)[:, None] // DH_DIFF == jnp.arange(W_DIFF)[None, :] // DH_DIFF).astype(BF16)
            gq = jnp.tile(row(p["g_qnorm"][i]), (1, W_DIFF // DH_DIFF))
            gk = jnp.tile(row(p["g_knorm"][i]), (1, W_DIFF // DH_DIFF))
            ret_in, qb, kb, vb, *vt = _in_even(
                x2d, row(p["g_norm1"][layer]), sh1, sc1, cos_t, sin_t, wsp(p["w_in_mix"][i]),
                gq, gk, seg, tm, tps, BF16 if prompt else F32, prompt)
            lam_vec = jnp.stack([p["lam_q1"][i], p["lam_k1"][i], p["lam_q2"][i], p["lam_k2"][i]]).astype(F32)
            g_ret = row(p["g_ret"][i])
            g_d = row(p["g_dnorm"][i])
            if prompt:
                dm, qd, kd, cd = _retention_constants(math.gcd(seq_len, RET_CHUNK))
                o_ret, s_new = _retention_prompt(ret_in, bsz, seq_len, dm, qd, kd, cd, g_ret)
                o_diff = _diff_prompt(qb, kb, vt[0], lam_vec, g_d, bsz, seq_len, lam_init)
            else:
                _, qd, _, cd = _retention_constants(1)
                o_ret, s_new = _retention_step(ret_in, ret0[i].astype(F32), qd, cd, g_ret)
                o_diff = _diff_step(qb, kb, vb, paged["cache_k"], paged["cache_v"], i,
                                    paged["page_table"], lam_vec, g_d, lam_init)
            rets.append(s_new)
            if prompt:
                k_out = jnp.transpose(vt[1].reshape(bsz, H_DIFF, 2, DH_DIFF, seq_len), (0, 4, 1, 2, 3))
            else:
                k_out = kb.reshape(bsz, seq_len, H_DIFF, 2, DH_DIFF)
            ks_new.append(k_out)
            vs_new.append(vb.reshape(bsz, seq_len, H_DIFF, DV_DIFF))
            x2d = _out_even(o_ret, o_diff, wsp(p["w_out_mix"][i]), x2d, gt1, tm, tps)
            x2d = _ffn(x2d, row(p["g_norm2"][layer]), sh2, sc2, gt2, wsp(p["w_ffn1"][i]),
                       wsp(p["w_ffn3"][i]), wsp(p["w_ffn2"][i]), tm_ffn, tps_ffn)
        else:
            a_re, a_im, bmat, cmat, d_row = _s5_matrices(
                p["lam_re"][i], p["lam_im"][i], p["log_step"][i], p["b_re"][i], p["b_im"][i],
                p["c_re"][i], p["c_im"][i], p["d_skip"][i])
            u2d = _modmm(x2d, row(p["g_norm1"][layer]), sh1, sc1, wsp(p["w_ssm_in"][i]), tm, tps)
            n_state = N_GROUPS * SSM_STATE
            if prompt:
                z, h_re, h_im = _s5_prompt(u2d, bsz, seq_len, a_re, a_im, bmat.astype(BF16), cmat.astype(BF16),
                                           d_row)
            else:
                z, h_re, h_im = _s5_step(
                    u2d, ssm0[0][i].astype(F32).reshape(bsz, n_state), ssm0[1][i].astype(F32).reshape(bsz, n_state),
                    a_re.reshape(1, n_state), a_im.reshape(1, n_state), wsp(bmat), wsp(cmat), d_row)
            srs.append(h_re.reshape(bsz, N_GROUPS, SSM_STATE))
            sis.append(h_im.reshape(bsz, N_GROUPS, SSM_STATE))
            wr = jnp.zeros((D_MODEL, LANES), F32).at[:, 0:N_EXPERTS].set(p["w_router"][i].astype(F32))
            wr_hi, wr_lo = _wsplit(wr, True)
            br = jnp.full((1, LANES), NEG_BIG, F32).at[0, 0:N_EXPERTS].set(p["b_router"][i].astype(F32))
            x2d, h_mod, route = _glu_router(
                z, wsp(p["w_glu"][i]), x2d, gt1, row(p["g_norm2"][layer]), sh2, sc2,
                wr_hi, wr_lo, br, tm_ffn, tps_ffn)
            x2d = _moe(h_mod, route, x2d, gt2, p["w_moe1"][i].astype(BF16), p["w_moe3"][i].astype(BF16),
                       p["w_moe2"][i].astype(BF16),
                       512 if prompt else 32, tm, tps)
    return (x2d.reshape(bsz, seq_len, D_MODEL), jnp.stack(rets), jnp.stack(ks_new), jnp.stack(vs_new),
            jnp.stack(srs), jnp.stack(sis))


def kernel(x_prompt, x_sample, state_ret, cache_k, cache_v, state_ssm_re, state_ssm_im, page_table, c_prompt, c_sample, w_ada, b_ada, g_norm1, g_norm2, w_in_mix, w_out_mix, g_ret, g_qnorm, g_knorm, g_dnorm, lam_q1, lam_k1, lam_q2, lam_k2, w_ffn1, w_ffn3, w_ffn2, w_ssm_in, lam_re, lam_im, log_step, b_re, b_im, c_re, c_im, d_skip, w_glu, w_router, b_router, w_moe1, w_moe3, w_moe2):
    p = dict(w_ada=w_ada, g_norm1=g_norm1, g_norm2=g_norm2, w_in_mix=w_in_mix, w_out_mix=w_out_mix,
             g_ret=g_ret, g_qnorm=g_qnorm, g_knorm=g_knorm, g_dnorm=g_dnorm, lam_q1=lam_q1, lam_k1=lam_k1,
             lam_q2=lam_q2, lam_k2=lam_k2, w_ffn1=w_ffn1, w_ffn3=w_ffn3, w_ffn2=w_ffn2, w_ssm_in=w_ssm_in,
             lam_re=lam_re, lam_im=lam_im, log_step=log_step, b_re=b_re, b_im=b_im, c_re=c_re, c_im=c_im,
             d_skip=d_skip, w_glu=w_glu, w_router=w_router, b_router=b_router,
             w_moe1=w_moe1, w_moe3=w_moe3, w_moe2=w_moe2)
    n_prompt = c_prompt.shape[0]
    n_sample = c_sample.shape[0]
    n_all = n_prompt + n_sample
    rows = ((n_all + 7) // 8) * 8
    c_all = jnp.zeros((rows, D_MODEL), F32).at[0:n_prompt].set(c_prompt).at[n_prompt:n_all].set(c_sample)
    mods = _ada(c_all, w_ada, b_ada)
    seq_len = x_prompt.shape[1]
    past_len = page_table.shape[1] * PAGE_SIZE
    y_p, ret_p, k_p, v_p, sr_p, si_p = _trunk(
        x_prompt, mods[:, 0:n_prompt], np.arange(seq_len), None, None, None, p)
    paged = dict(cache_k=cache_k, cache_v=cache_v, page_table=page_table)
    y_s, ret_s, k_s, v_s, sr_s, si_s = _trunk(
        x_sample, mods[:, n_prompt:n_all], past_len + np.arange(x_sample.shape[1]),
        state_ret, (state_ssm_re, state_ssm_im), paged, p)
    return (y_p, y_s, ret_p, ret_s, k_p, v_p, k_s, v_s, sr_p, si_p, sr_s, si_s)
```

```python
import functools
import math

import jax
import jax.numpy as jnp
import numpy as np
from jax import lax
from jax.experimental import pallas as pl
from jax.experimental.pallas import tpu as pltpu

F32 = jnp.float32
BF16 = jnp.bfloat16

D_MODEL = 1024
H_RET = 4
DK_RET = 128
DV_RET = 128
RET_CHUNK = 128
ROPE_BASE = 10000.0
H_DIFF = 4
DH_DIFF = 64
DV_DIFF = 128
PAGE_SIZE = 128
W_RET = H_RET * DK_RET
W_DIFF = H_DIFF * 2 * DH_DIFF
IN_EVEN = 4 * W_RET + 3 * W_DIFF
SSM_GROUP = 16
N_GROUPS = D_MODEL // SSM_GROUP
SSM_STATE = 64
N_PAIRS = N_GROUPS // 2
STEP_PITCH = N_PAIRS + 8
D_FF = 2816
N_EXPERTS = 8
D_FF_EXPERT = 3584
N_ADA = 6
NORM_EPS = 1e-6
LANES = 128
NEG_BIG = -1e30

VMEM_LIMIT_BYTES = 56 * 1024 * 1024


def _params(*sem):
    return pltpu.CompilerParams(dimension_semantics=sem, vmem_limit_bytes=VMEM_LIMIT_BYTES)


def _bdot(a, b):
    return jnp.dot(a, b, preferred_element_type=F32)


def _dot_nt(a, b):
    return lax.dot_general(a, b, (((1,), (1,)), ((), ())), preferred_element_type=F32)


def _dot_tn(a, b):
    return lax.dot_general(a, b, (((0,), (0,)), ((), ())), preferred_element_type=F32)


def _rms(x):
    return x * lax.rsqrt(jnp.mean(x * x, axis=-1, keepdims=True) + NORM_EPS)


def _modulate(x, g, shift, scale):
    return (_rms(x) * g) * (1.0 + scale) + shift


def _hi_lo(x):
    bits = lax.bitcast_convert_type(x, jnp.uint32) & jnp.uint32(0xFFFF0000)
    hi = lax.bitcast_convert_type(bits, F32)
    return hi.astype(BF16), (x - hi).astype(BF16)


def _split_dot(x, w_hi, w_lo):
    x_hi, x_lo = _hi_lo(x)
    return _bdot(x_hi, w_hi) + _bdot(x_lo, w_hi) + _bdot(x_hi, w_lo)


def _mm(x, w_refs, rows=slice(None)):
    if len(w_refs) == 1:
        return _bdot(x.astype(BF16), w_refs[0][rows, :])
    return _split_dot(x.astype(F32), w_refs[0][rows, :], w_refs[1][rows, :])


def _wsplit(w, precise):
    w = w.astype(F32)
    if not precise:
        return (w.astype(BF16),)
    return _hi_lo(w)


def _ada_kernel(c_ref, w_ref, b_ref, o_ref):
    w_hi, w_lo = _hi_lo(w_ref[...])
    o_ref[...] = _split_dot(jax.nn.silu(c_ref[...]), w_hi, w_lo) + b_ref[...]


def _ada(c_all, w_ada, b_ada):
    depth, _, n = w_ada.shape
    rows = c_all.shape[0]
    tn = 1536
    return pl.pallas_call(
        _ada_kernel,
        grid=(depth, n // tn),
        in_specs=[
            pl.BlockSpec((rows, D_MODEL), lambda l, j: (0, 0)),
            pl.BlockSpec((None, D_MODEL, tn), lambda l, j: (l, 0, j)),
            pl.BlockSpec((None, 1, tn), lambda l, j: (l, 0, j)),
        ],
        out_specs=pl.BlockSpec((None, rows, tn), lambda l, j: (l, 0, j)),
        out_shape=jax.ShapeDtypeStruct((depth, rows, n), F32),
        compiler_params=_params("parallel", "parallel"),
        name="ada",
    )(c_all, w_ada, b_ada.reshape(depth, 1, n))


def _vec_spec(arr, tm, tiles_per_seq):
    if arr.shape[1] == 1:
        return pl.BlockSpec((None, 1, arr.shape[2]), lambda i, *_: (i // tiles_per_seq, 0, 0))
    return pl.BlockSpec((None, tm, arr.shape[2]), lambda i, *_: (0, 0, 0))


def _pos_spec(arr, tm, tiles_per_seq):
    if arr.shape[0] == 1:
        return pl.BlockSpec((1, arr.shape[1]), lambda i, *_: (0, 0))
    return pl.BlockSpec((tm, arr.shape[1]), lambda i, *_: (i % tiles_per_seq, 0))


def _full_spec(arr):
    nd = arr.ndim
    return pl.BlockSpec(arr.shape, lambda *_: (0,) * nd)


def _in_even_kernel(x_ref, g_ref, sh_ref, sc_ref, cos_ref, sin_ref, gq_ref, gk_ref,
                    seg_ref, *rest, n_w):
    w_refs = rest[:n_w]
    ret_ref, qb_ref, kb_ref, vb_ref = rest[n_w:n_w + 4]
    vt_ref, kt_ref = rest[n_w + 4:n_w + 6] if len(rest) > n_w + 4 else (None, None)
    y = _mm(_modulate(x_ref[...], g_ref[...], sh_ref[...], sc_ref[...]), w_refs)
    cos = cos_ref[...]
    sin = sin_ref[...]
    for part in range(2):
        for hd in range(H_RET):
            lo = part * W_RET + hd * DK_RET
            blk = y[:, lo:lo + DK_RET]
            rot = blk * cos + pltpu.roll(blk, DK_RET // 2, axis=1) * sin
            if part == 1:
                rot = rot * (DK_RET ** -0.5)
            ret_ref[:, lo:lo + DK_RET] = rot
    ret_ref[:, 2 * W_RET:4 * W_RET] = y[:, 2 * W_RET:4 * W_RET]
    seg = seg_ref[...]
    base = 4 * W_RET

    def qk_norm(blk, g):
        sq_hi, sq_lo = _hi_lo(blk * blk)
        ms = (_bdot(sq_hi, seg) + _bdot(sq_lo, seg)) * (1.0 / DH_DIFF)
        return blk * lax.rsqrt(ms + NORM_EPS) * g

    qb_ref[...] = qk_norm(y[:, base:base + W_DIFF], gq_ref[...]).astype(qb_ref.dtype)
    kn = qk_norm(y[:, base + W_DIFF:base + 2 * W_DIFF], gk_ref[...])
    kb_ref[...] = kn
    v = y[:, base + 2 * W_DIFF:base + 3 * W_DIFF]
    vb_ref[...] = v
    if vt_ref is not None:
        vt_ref[...] = v.T.astype(BF16)
        kt_ref[...] = kn.T


def _in_even(x2d, g, sh, sc, cos_t, sin_t, ws, gq, gk, seg, tm, tiles_per_seq, q_dtype, emit_vt):
    m = x2d.shape[0]
    row = lambda n: pl.BlockSpec((tm, n), lambda i: (i, 0))
    out_specs = [row(4 * W_RET), row(W_DIFF), row(W_DIFF), row(W_DIFF)]
    out_shape = [
        jax.ShapeDtypeStruct((m, 4 * W_RET), F32),
        jax.ShapeDtypeStruct((m, W_DIFF), q_dtype),
        jax.ShapeDtypeStruct((m, W_DIFF), F32),
        jax.ShapeDtypeStruct((m, W_DIFF), F32),
    ]
    if emit_vt:
        out_specs.append(pl.BlockSpec((W_DIFF, tm), lambda i: (0, i)))
        out_shape.append(jax.ShapeDtypeStruct((W_DIFF, m), BF16))
        out_specs.append(pl.BlockSpec((None, W_DIFF, tm), lambda i: (i // tiles_per_seq, 0, i % tiles_per_seq)))
        out_shape.append(jax.ShapeDtypeStruct((m // (tm * tiles_per_seq), W_DIFF, tm * tiles_per_seq), F32))
    return pl.pallas_call(
        functools.partial(_in_even_kernel, n_w=len(ws)),
        grid=(m // tm,),
        in_specs=[
            row(D_MODEL), _full_spec(g), _vec_spec(sh, tm, tiles_per_seq), _vec_spec(sc, tm, tiles_per_seq),
            _pos_spec(cos_t, tm, tiles_per_seq), _pos_spec(sin_t, tm, tiles_per_seq),
            _full_spec(gq), _full_spec(gk), _full_spec(seg),
        ] + [_full_spec(w) for w in ws],
        out_specs=out_specs,
        out_shape=out_shape,
        compiler_params=_params("parallel"),
        name="in_even",
    )(x2d, g, sh, sc, cos_t, sin_t, gq, gk, seg, *ws)


def _ret_kernel(q_ref, k_ref, v_ref, ga_ref, dm_ref, qd_ref, kd_ref, cd_ref, g_ref,
                o_ref, sfin_ref, s_scr, *, n_chunks):
    j = pl.program_id(1)

    @pl.when(j == 0)
    def _():
        s_scr[...] = jnp.zeros_like(s_scr)

    g = g_ref[...]

    def body(c, carry):
        r = pl.multiple_of(c * RET_CHUNK, RET_CHUNK)
        for hd in range(H_RET):
            sl = slice(hd * DK_RET, (hd + 1) * DK_RET)
            q = q_ref[pl.ds(r, RET_CHUNK), sl]
            k = k_ref[pl.ds(r, RET_CHUNK), sl]
            vb = v_ref[pl.ds(r, RET_CHUNK), sl].astype(BF16)
            s = s_scr[hd]
            att = _dot_nt(q.astype(BF16), k.astype(BF16)) * dm_ref[hd]
            o = _bdot(att.astype(BF16), vb) + _bdot((q * qd_ref[hd]).astype(BF16), s.astype(BF16))
            s_scr[hd] = cd_ref[hd] * s + _dot_tn((k * kd_ref[hd]).astype(BF16), vb)
            on = _rms(o) * g
            o_ref[pl.ds(r, RET_CHUNK), sl] = (on * jax.nn.silu(ga_ref[pl.ds(r, RET_CHUNK), sl])).astype(BF16)
        return carry

    lax.fori_loop(0, n_chunks, body, 0)

    @pl.when(j == pl.num_programs(1) - 1)
    def _():
        sfin_ref[...] = s_scr[...]


def _retention_prompt(ret_in, bsz, seq_len, dm, qd, kd, cd, g_ret):
    n_chunks = min(8, seq_len // RET_CHUNK)
    rows = n_chunks * RET_CHUNK
    nj = seq_len // rows
    col = lambda c: pl.BlockSpec((rows, W_RET), lambda b, j: (b * nj + j, c))
    return pl.pallas_call(
        functools.partial(_ret_kernel, n_chunks=n_chunks),
        grid=(bsz, nj),
        in_specs=[col(0), col(1), col(2), col(3), _full_spec(dm), _full_spec(qd), _full_spec(kd),
                  _full_spec(cd), _full_spec(g_ret)],
        out_specs=[
            pl.BlockSpec((rows, W_RET), lambda b, j: (b * nj + j, 0)),
            pl.BlockSpec((None, H_RET, DK_RET, DV_RET), lambda b, j: (b, 0, 0, 0)),
        ],
        out_shape=[
            jax.ShapeDtypeStruct((bsz * seq_len, W_RET), BF16),
            jax.ShapeDtypeStruct((bsz, H_RET, DK_RET, DV_RET), F32),
        ],
        scratch_shapes=[pltpu.VMEM((H_RET, DK_RET, DV_RET), F32)],
        compiler_params=_params("parallel", "arbitrary"),
        name="retention_prompt",
    )(ret_in, ret_in, ret_in, ret_in, dm, qd, kd, cd, g_ret)


def _ret_step_kernel(ret_ref, s_ref, qd_ref, cd_ref, g_ref, o_ref, snew_ref):
    g = g_ref[...]
    rows = lax.broadcasted_iota(jnp.int32, (DK_RET, DK_RET), 0)
    cols = lax.broadcasted_iota(jnp.int32, (DK_RET, DK_RET), 1)
    diag = rows == cols
    for hd in range(H_RET):
        lo = hd * DK_RET
        q = ret_ref[:, lo:lo + DK_RET]
        k = ret_ref[:, W_RET + lo:W_RET + lo + DK_RET]
        v = ret_ref[:, 2 * W_RET + lo:2 * W_RET + lo + DK_RET]
        ga = ret_ref[:, 3 * W_RET + lo:3 * W_RET + lo + DK_RET]
        s = s_ref[hd]
        s_hi, s_lo = _hi_lo(s)
        att = jnp.sum(q * k, axis=-1, keepdims=True)
        q8 = jnp.broadcast_to(q * qd_ref[hd], (8, DK_RET))
        o = att * v + _split_dot(q8, s_hi, s_lo)[0:1, :]
        k_diag = jnp.where(diag, jnp.broadcast_to(k, (DK_RET, DK_RET)), 0.0)
        v_rows = jnp.broadcast_to(v, (DK_RET, DV_RET))
        v_hi, v_lo = _hi_lo(v_rows)
        snew_ref[hd] = cd_ref[hd] * s + _split_dot(k_diag, v_hi, v_lo)
        on = _rms(o) * g
        o_ref[:, lo:lo + DK_RET] = on * jax.nn.silu(ga)


def _retention_step(ret_in, state, qd1, cd, g_ret):
    bsz = state.shape[0]
    ret3 = ret_in.reshape(bsz, 1, 4 * W_RET)
    o, s_new = pl.pallas_call(
        _ret_step_kernel,
        grid=(bsz,),
        in_specs=[
            pl.BlockSpec((None, 1, 4 * W_RET), lambda b: (b, 0, 0)),
            pl.BlockSpec((None, H_RET, DK_RET, DV_RET), lambda b: (b, 0, 0, 0)),
            _full_spec(qd1), _full_spec(cd), _full_spec(g_ret),
        ],
        out_specs=[
            pl.BlockSpec((None, 1, W_RET), lambda b: (b, 0, 0)),
            pl.BlockSpec((None, H_RET, DK_RET, DV_RET), lambda b: (b, 0, 0, 0)),
        ],
        out_shape=[
            jax.ShapeDtypeStruct((bsz, 1, W_RET), F32),
            jax.ShapeDtypeStruct((bsz, H_RET, DK_RET, DV_RET), F32),
        ],
        compiler_params=_params("parallel"),
        name="retention_step",
    )(ret3, state, qd1, cd, g_ret)
    return o.reshape(bsz, W_RET), s_new


def _lambda_value(lam_ref, lam_init):
    lv = lam_ref[...]
    s1 = jnp.sum(lv[0:1, :] * lv[1:2, :], axis=-1, keepdims=True)
    s2 = jnp.sum(lv[2:3, :] * lv[3:4, :], axis=-1, keepdims=True)
    return jnp.exp(s1) - jnp.exp(s2) + lam_init


Q_GROUP = 512
LOG2E = 1.4426950408889634


def _diff_kernel(qi_ref, ki_ref, q_ref, k_ref, vt_ref, lam_ref, g_ref, o_ref,
                 qs_scr, k_scr, m_scr, l_scr, acc_scr, *, t, lam_init):
    p = pl.program_id(2)
    qi = qi_ref[p]
    ki = ki_ref[p]
    qg = min(Q_GROUP, t)
    ng = t // qg

    @pl.when(ki == 0)
    def _():
        q = q_ref[...].astype(F32) * (DH_DIFF ** -0.5 * LOG2E)
        lane = lax.broadcasted_iota(jnp.int32, q.shape, 1)
        q_maps = (jnp.where(lane < DH_DIFF, q, 0.0).astype(BF16), jnp.where(lane >= DH_DIFF, q, 0.0).astype(BF16))
        for mp in range(2):
            for gi in range(ng):
                qs_scr[mp * ng + gi] = q_maps[mp][gi * qg:(gi + 1) * qg, :]
        m_scr[...] = jnp.full_like(m_scr, NEG_BIG)
        l_scr[...] = jnp.zeros_like(l_scr)
        acc_scr[...] = jnp.zeros_like(acc_scr)

    k_scr[...] = k_ref[...].astype(BF16)

    def scores(g, n_keys):
        return _dot_nt(k_scr[0:n_keys, :], qs_scr[g])

    def update(g, s, n_keys):
        m_prev = m_scr[g]
        m_new = jnp.maximum(m_prev, jnp.max(s, axis=0, keepdims=True))
        alpha = jnp.exp2(m_prev - m_new)
        pr = jnp.exp2(s - m_new)
        l_scr[g] = alpha * l_scr[g] + jnp.sum(pr, axis=0, keepdims=True)
        acc_scr[g] = alpha * acc_scr[g] + _bdot(vt_ref[:, 0:n_keys], pr.astype(BF16))
        m_scr[g] = m_new

    @pl.when(ki < qi)
    def _():
        def body(i, carry):
            s_a = scores(2 * i, t)
            s_b = scores(2 * i + 1, t)
            update(2 * i, s_a, t)
            update(2 * i + 1, s_b, t)
            return carry
        lax.fori_loop(0, ng, body, 0)

    @pl.when(ki == qi)
    def _():
        for g in range(2 * ng):
            q_off = (g % ng) * qg
            n_keys = q_off + qg
            s = scores(g, n_keys)
            key = lax.broadcasted_iota(jnp.int32, s.shape, 0)
            qpos = lax.broadcasted_iota(jnp.int32, s.shape, 1) + q_off
            update(g, jnp.where(key <= qpos, s, NEG_BIG), n_keys)
        lam = _lambda_value(lam_ref, lam_init)
        g_col = g_ref[...]
        for c in range(t // LANES):
            g1, lanes = (c * LANES) // qg, slice((c * LANES) % qg, (c * LANES) % qg + LANES)
            a1 = acc_scr[g1][:, lanes] * (1.0 / l_scr[g1][:, lanes])
            a2 = acc_scr[ng + g1][:, lanes] * (1.0 / l_scr[ng + g1][:, lanes])
            o = a1 - lam * a2
            o = o * lax.rsqrt(jnp.mean(o * o, axis=0, keepdims=True) + NORM_EPS) * g_col * (1.0 - lam_init)
            o_ref[c * LANES:(c + 1) * LANES, :] = o.T.astype(BF16)


def _diff_prompt(qb, kb, vt, lam_vec, g_d, bsz, seq_len, lam_init):
    t = min(1024, seq_len)
    nq = seq_len // t
    pairs = [(i, j) for i in range(nq) for j in range(i + 1)]
    qi = jnp.asarray([a for a, _ in pairs], jnp.int32)
    ki = jnp.asarray([b for _, b in pairs], jnp.int32)
    g_col = g_d.reshape(DV_DIFF, 1)
    qg = min(Q_GROUP, t)
    n_grp = 2 * t // qg
    grid_spec = pltpu.PrefetchScalarGridSpec(
        num_scalar_prefetch=2,
        grid=(bsz, H_DIFF, len(pairs)),
        in_specs=[
            pl.BlockSpec((t, 2 * DH_DIFF), lambda b, h, p, qi, ki: (b * nq + qi[p], h)),
            pl.BlockSpec((t, 2 * DH_DIFF), lambda b, h, p, qi, ki: (b * nq + ki[p], h)),
            pl.BlockSpec((DV_DIFF, t), lambda b, h, p, qi, ki: (h, b * nq + ki[p])),
            pl.BlockSpec(lam_vec.shape, lambda b, h, p, qi, ki: (0, 0)),
            pl.BlockSpec(g_col.shape, lambda b, h, p, qi, ki: (0, 0)),
        ],
        out_specs=pl.BlockSpec((t, DV_DIFF), lambda b, h, p, qi, ki: (b * nq + qi[p], h)),
        scratch_shapes=[
            pltpu.VMEM((n_grp, qg, 2 * DH_DIFF), BF16),
            pltpu.VMEM((t, 2 * DH_DIFF), BF16),
            pltpu.VMEM((n_grp, 1, qg), F32),
            pltpu.VMEM((n_grp, 1, qg), F32),
            pltpu.VMEM((n_grp, DV_DIFF, qg), F32),
        ],
    )
    return pl.pallas_call(
        functools.partial(_diff_kernel, t=t, lam_init=lam_init),
        grid_spec=grid_spec,
        out_shape=jax.ShapeDtypeStruct((bsz * seq_len, H_DIFF * DV_DIFF), BF16),
        compiler_params=_params("parallel", "parallel", "arbitrary"),
        name="diff_prompt",
    )(qi, ki, qb, kb, vt, lam_vec, g_col)


PAGES_PER_STEP = 32
PAGES_PER_UPDATE = 4
N_MAPS = 2 * H_DIFF


def _group_sums(prod):
    return jnp.concatenate(
        [jnp.sum(prod[g * DH_DIFF:(g + 1) * DH_DIFF, :], axis=0, keepdims=True) for g in range(N_MAPS)], axis=0)


def _diff_step_kernel(pt_ref, q_ref, kn_ref, vn_ref, lam_ref, g_ref, *rest, pps, lam_init):
    k_refs = rest[:pps]
    v_refs = rest[pps:2 * pps]
    o_ref = rest[2 * pps]
    qb_scr, m_scr, l_scr, acc_scr = rest[2 * pps + 1:]
    j = pl.program_id(1)
    scale = DH_DIFF ** -0.5

    @pl.when(j == 0)
    def _():
        qb_scr[...] = jnp.broadcast_to(q_ref[...], (W_DIFF, PAGE_SIZE))
        m_scr[...] = jnp.full_like(m_scr, NEG_BIG)
        l_scr[...] = jnp.zeros_like(l_scr)
        acc_scr[...] = jnp.zeros_like(acc_scr)

    row_head = lax.broadcasted_iota(jnp.int32, (N_MAPS, DV_DIFF), 0) // 2
    for first in range(0, pps, PAGES_PER_UPDATE):
        pages = range(first, first + PAGES_PER_UPDATE)
        scores = {pg: _group_sums(k_refs[pg][...] * qb_scr[...]) * scale for pg in pages}
        s_max = scores[first]
        for pg in pages:
            s_max = jnp.maximum(s_max, scores[pg])
        m_prev = m_scr[...]
        m_new = jnp.maximum(m_prev, jnp.max(s_max, axis=-1, keepdims=True))
        alpha = jnp.exp(m_prev - m_new)
        p_sum = jnp.zeros((N_MAPS, PAGE_SIZE), F32)
        pv = jnp.zeros((N_MAPS, DV_DIFF), F32)
        for pg in pages:
            pr = jnp.exp(scores[pg] - m_new)
            p_sum = p_sum + pr
            p_bits = lax.bitcast_convert_type(pr, jnp.uint32) & jnp.uint32(0xFFFF0000)
            p_top = lax.bitcast_convert_type(p_bits, F32)
            p_hi = p_top.astype(BF16)
            p_both = jnp.concatenate([p_top, pr - p_top], axis=0).astype(BF16)
            for hd in range(H_DIFF):
                v_hi, v_lo = _hi_lo(v_refs[pg][pl.ds(hd, PAGE_SIZE, stride=H_DIFF), :])
                r = _bdot(p_both, jnp.concatenate([v_hi, v_lo], axis=1))
                r = r[0:N_MAPS, 0:DV_DIFF] + r[N_MAPS:2 * N_MAPS, 0:DV_DIFF] + r[0:N_MAPS, DV_DIFF:2 * DV_DIFF]
                pv = pv + jnp.where(row_head == hd, r, 0.0)
        l_scr[...] = alpha * l_scr[...] + jnp.sum(p_sum, axis=-1, keepdims=True)
        acc_scr[...] = alpha * acc_scr[...] + pv
        m_scr[...] = m_new

    @pl.when(j == pl.num_programs(1) - 1)
    def _():
        s = _group_sums(q_ref[...] * kn_ref[...]) * scale
        m_old = m_scr[...]
        m_fin = jnp.maximum(m_old, s)
        a_fin = jnp.exp(m_old - m_fin)
        pr = jnp.exp(s - m_fin)
        l_fin = a_fin * l_scr[...] + pr
        a = (a_fin * acc_scr[...] + pr * vn_ref[...]) / l_fin
        lam = _lambda_value(lam_ref, lam_init)
        g = g_ref[...]
        for hd in range(H_DIFF):
            d = a[2 * hd:2 * hd + 1, :] - lam * a[2 * hd + 1:2 * hd + 2, :]
            o_ref[hd:hd + 1, :] = _rms(d) * g * (1.0 - lam_init)


def _diff_step(qb, kb_new, vb_new, cache_k, cache_v, layer_idx, page_table, lam_vec, g_d, lam_init):
    bsz, n_pages = page_table.shape
    n_layers, n_pool = cache_k.shape[0:2]
    pps = math.gcd(n_pages, PAGES_PER_STEP)
    assert pps % PAGES_PER_UPDATE == 0
    ck = jnp.transpose(cache_k, (0, 1, 3, 4, 5, 2)).reshape(n_layers * n_pool, W_DIFF, PAGE_SIZE)
    cv = cache_v.reshape(n_layers * n_pool, PAGE_SIZE * H_DIFF, DV_DIFF)
    page_table = page_table + layer_idx * n_pool
    col_spec = pl.BlockSpec((None, W_DIFF, 1), lambda b, j, pt: (b, 0, 0))
    v_new = jnp.repeat(vb_new.reshape(bsz, H_DIFF, DV_DIFF), 2, axis=1)

    def page_spec(pg):
        return pl.BlockSpec((None, W_DIFF, PAGE_SIZE), lambda b, j, pt, pg=pg: (pt[b, j * pps + pg], 0, 0))

    grid_spec = pltpu.PrefetchScalarGridSpec(
        num_scalar_prefetch=1,
        grid=(bsz, n_pages // pps),
        in_specs=[col_spec, col_spec,
                  pl.BlockSpec((None, N_MAPS, DV_DIFF), lambda b, j, pt: (b, 0, 0)),
                  pl.BlockSpec(lam_vec.shape, lambda b, j, pt: (0, 0)),
                  pl.BlockSpec(g_d.shape, lambda b, j, pt: (0, 0))]
                 + [page_spec(pg) for pg in range(pps)] + [page_spec(pg) for pg in range(pps)],
        out_specs=pl.BlockSpec((None, H_DIFF, DV_DIFF), lambda b, j, pt: (b, 0, 0)),
        scratch_shapes=[
            pltpu.VMEM((W_DIFF, PAGE_SIZE), F32),
            pltpu.VMEM((N_MAPS, 1), F32),
            pltpu.VMEM((N_MAPS, 1), F32),
            pltpu.VMEM((N_MAPS, DV_DIFF), F32),
        ],
    )
    o = pl.pallas_call(
        functools.partial(_diff_step_kernel, pps=pps, lam_init=lam_init),
        grid_spec=grid_spec,
        out_shape=jax.ShapeDtypeStruct((bsz, H_DIFF, DV_DIFF), F32),
        compiler_params=_params("parallel", "arbitrary"),
        name="diff_step",
    )(page_table, qb.reshape(bsz, W_DIFF, 1), kb_new.reshape(bsz, W_DIFF, 1), v_new, lam_vec, g_d,
      *([ck] * pps), *([cv] * pps))
    return o.reshape(bsz, W_DIFF)


def _out_even_kernel(a_ref, b_ref, x_ref, gt_ref, *rest):
    w_refs = rest[:-1]
    o_ref = rest[-1]
    m = _mm(a_ref[...], w_refs, slice(0, W_RET)) + _mm(b_ref[...], w_refs, slice(W_RET, 2 * W_RET))
    o_ref[...] = x_ref[...] + gt_ref[...] * m


def _out_even(o_ret, o_diff, ws, x2d, gt, tm, tiles_per_seq):
    m = x2d.shape[0]
    row = lambda n: pl.BlockSpec((tm, n), lambda i: (i, 0))
    return pl.pallas_call(
        _out_even_kernel,
        grid=(m // tm,),
        in_specs=[row(W_RET), row(W_RET), row(D_MODEL), _vec_spec(gt, tm, tiles_per_seq)]
                 + [_full_spec(w) for w in ws],
        out_specs=row(D_MODEL),
        out_shape=jax.ShapeDtypeStruct((m, D_MODEL), F32),
        compiler_params=_params("parallel"),
        name="out_even",
    )(o_ret, o_diff, x2d, gt, *ws)


def _ffn_kernel(x_ref, g_ref, sh_ref, sc_ref, gt_ref, *rest, n_w):
    w1, w3, w2 = rest[0:n_w], rest[n_w:2 * n_w], rest[2 * n_w:3 * n_w]
    o_ref = rest[3 * n_w]
    x = x_ref[...]
    h = _modulate(x, g_ref[...], sh_ref[...], sc_ref[...])
    if n_w == 1:
        h = h.astype(BF16)
    u = jax.nn.silu(_mm(h, w1)) * _mm(h, w3)
    o_ref[...] = x + gt_ref[...] * _mm(u, w2)


def _ffn(x2d, g, sh, sc, gt, w1s, w3s, w2s, tm, tiles_per_seq):
    m = x2d.shape[0]
    row = pl.BlockSpec((tm, D_MODEL), lambda i: (i, 0))
    vs = lambda a: _vec_spec(a, tm, tiles_per_seq)
    ws = (*w1s, *w3s, *w2s)
    return pl.pallas_call(
        functools.partial(_ffn_kernel, n_w=len(w1s)),
        grid=(m // tm,),
        in_specs=[row, _full_spec(g), vs(sh), vs(sc), vs(gt)] + [_full_spec(w) for w in ws],
        out_specs=row,
        out_shape=jax.ShapeDtypeStruct((m, D_MODEL), F32),
        compiler_params=_params("parallel"),
        name="ffn",
    )(x2d, g, sh, sc, gt, *ws)


def _modmm_kernel(x_ref, g_ref, sh_ref, sc_ref, *rest):
    w_refs = rest[:-1]
    o_ref = rest[-1]
    o_ref[...] = _mm(_modulate(x_ref[...], g_ref[...], sh_ref[...], sc_ref[...]), w_refs)


def _modmm(x2d, g, sh, sc, ws, tm, tiles_per_seq):
    m = x2d.shape[0]
    n = ws[0].shape[1]
    vs = lambda a: _vec_spec(a, tm, tiles_per_seq)
    return pl.pallas_call(
        _modmm_kernel,
        grid=(m // tm,),
        in_specs=[pl.BlockSpec((tm, D_MODEL), lambda i: (i, 0)), _full_spec(g), vs(sh), vs(sc)]
                 + [_full_spec(w) for w in ws],
        out_specs=pl.BlockSpec((tm, n), lambda i: (i, 0)),
        out_shape=jax.ShapeDtypeStruct((m, n), F32),
        compiler_params=_params("parallel"),
        name="modmm",
    )(x2d, g, sh, sc, *ws)


def _mm_at(x, w_refs, idx):
    if len(w_refs) == 1:
        return _bdot(x.astype(BF16), w_refs[0][idx])
    return _split_dot(x, w_refs[0][idx], w_refs[1][idx])


def _s5_input_drive(u, bmat_refs, j):
    return _mm_at(u[:, j * LANES:(j + 1) * LANES], bmat_refs, j)


def _s5_output_block(h_re_of, h_im_of, cmat_refs, jb):
    acc = None
    for q in range(4):
        r = 4 * jb + q
        t = (_mm_at(h_re_of(r), cmat_refs, (r, slice(0, LANES), slice(None)))
             + _mm_at(h_im_of(r), cmat_refs, (r, slice(LANES, 2 * LANES), slice(None))))
        acc = t if acc is None else acc + t
    return acc


def _s5_prompt_kernel(u_ref, are_ref, aim_ref, bmat_ref, cmat_ref, d_ref, z_ref, hre_ref, him_ref,
                      sre, sim, st_re, st_im, *, tc):
    j = pl.program_id(1)

    @pl.when(j == 0)
    def _():
        st_re[...] = jnp.zeros_like(st_re)
        st_im[...] = jnp.zeros_like(st_im)

    u = u_ref[...]
    half = 4 * LANES
    for jb in range(8):
        res = _s5_input_drive(u, (bmat_ref,), jb)
        for q in range(4):
            r = 4 * jb + q
            sre[pl.ds(r, tc, stride=STEP_PITCH), :] = res[:, q * LANES:(q + 1) * LANES]
            sim[pl.ds(r, tc, stride=STEP_PITCH), :] = res[:, half + q * LANES:half + (q + 1) * LANES]

    a_re = are_ref[...]
    a_im = aim_ref[...]

    def step(t, carry):
        h_re, h_im = carry
        base = pl.multiple_of(t * STEP_PITCH, 8)
        n_re = a_re * h_re - a_im * h_im + sre[pl.ds(base, N_PAIRS), :]
        n_im = a_re * h_im + a_im * h_re + sim[pl.ds(base, N_PAIRS), :]
        sre[pl.ds(base, N_PAIRS), :] = n_re
        sim[pl.ds(base, N_PAIRS), :] = n_im
        return n_re, n_im

    h_re, h_im = lax.fori_loop(0, tc, step, (st_re[...], st_im[...]))
    st_re[...] = h_re
    st_im[...] = h_im

    for jb in range(8):
        y = _s5_output_block(lambda r: sre[pl.ds(r, tc, stride=STEP_PITCH), :],
                             lambda r: sim[pl.ds(r, tc, stride=STEP_PITCH), :], (cmat_ref,), jb)
        sl = slice(jb * LANES, (jb + 1) * LANES)
        y = y + d_ref[:, sl] * u[:, sl]
        z_ref[:, sl] = jax.nn.gelu(y, approximate=True).astype(BF16)

    @pl.when(j == pl.num_programs(1) - 1)
    def _():
        hre_ref[...] = h_re
        him_ref[...] = h_im


def _s5_prompt(u2d, bsz, seq_len, a_re, a_im, bmat, cmat, d_row):
    tc = min(256, seq_len)
    nj = seq_len // tc
    state_spec = pl.BlockSpec((None, N_PAIRS, LANES), lambda b, j: (b, 0, 0))
    return pl.pallas_call(
        functools.partial(_s5_prompt_kernel, tc=tc),
        grid=(bsz, nj),
        in_specs=[pl.BlockSpec((tc, D_MODEL), lambda b, j: (b * nj + j, 0)),
                  _full_spec(a_re), _full_spec(a_im), _full_spec(bmat), _full_spec(cmat), _full_spec(d_row)],
        out_specs=[pl.BlockSpec((tc, D_MODEL), lambda b, j: (b * nj + j, 0)), state_spec, state_spec],
        out_shape=[
            jax.ShapeDtypeStruct((bsz * seq_len, D_MODEL), BF16),
            jax.ShapeDtypeStruct((bsz, N_PAIRS, LANES), F32),
            jax.ShapeDtypeStruct((bsz, N_PAIRS, LANES), F32),
        ],
        scratch_shapes=[
            pltpu.VMEM((tc * STEP_PITCH, LANES), F32),
            pltpu.VMEM((tc * STEP_PITCH, LANES), F32),
            pltpu.VMEM((N_PAIRS, LANES), F32),
            pltpu.VMEM((N_PAIRS, LANES), F32),
        ],
        compiler_params=_params("parallel", "arbitrary"),
        name="s5_prompt",
    )(u2d, a_re, a_im, bmat, cmat, d_row)


def _s5_step_kernel(u_ref, h0re_ref, h0im_ref, are_ref, aim_ref, bhi_ref, blo_ref, chi_ref, clo_ref, d_ref,
                    z_ref, hre_ref, him_ref):
    u = u_ref[...]
    half = 4 * LANES
    bmat_refs = (bhi_ref, blo_ref)
    cmat_refs = (chi_ref, clo_ref)
    for jb in range(8):
        res = _s5_input_drive(u, bmat_refs, jb)
        sl = slice(jb * half, (jb + 1) * half)
        a_re = are_ref[:, sl]
        a_im = aim_ref[:, sl]
        h_re = h0re_ref[:, sl]
        h_im = h0im_ref[:, sl]
        hre_ref[:, sl] = a_re * h_re - a_im * h_im + res[:, 0:half]
        him_ref[:, sl] = a_re * h_im + a_im * h_re + res[:, half:2 * half]
    for jb in range(8):
        y = _s5_output_block(lambda r: hre_ref[:, r * LANES:(r + 1) * LANES],
                             lambda r: him_ref[:, r * LANES:(r + 1) * LANES], cmat_refs, jb)
        sl = slice(jb * LANES, (jb + 1) * LANES)
        y = y + d_ref[:, sl] * u[:, sl]
        z_ref[:, sl] = jax.nn.gelu(y, approximate=True)


def _s5_step(u2d, h0_re, h0_im, a_re_row, a_im_row, bmats, cmats, d_row):
    bsz = u2d.shape[0]
    n_state = N_GROUPS * SSM_STATE
    args = (u2d, h0_re, h0_im, a_re_row, a_im_row, *bmats, *cmats, d_row)
    return pl.pallas_call(
        _s5_step_kernel,
        grid=(1,),
        in_specs=[_full_spec(a) for a in args],
        out_specs=[pl.BlockSpec((bsz, D_MODEL), lambda i: (0, 0)),
                   pl.BlockSpec((bsz, n_state), lambda i: (0, 0)),
                   pl.BlockSpec((bsz, n_state), lambda i: (0, 0))],
        out_shape=[
            jax.ShapeDtypeStruct((bsz, D_MODEL), F32),
            jax.ShapeDtypeStruct((bsz, n_state), F32),
            jax.ShapeDtypeStruct((bsz, n_state), F32),
        ],
        compiler_params=_params("arbitrary"),
        name="s5_step",
    )(*args)


def _glu_router_kernel(z_ref, x_ref, gt_ref, g_ref, sh_ref, sc_ref, wrh_ref, wrl_ref, br_ref, *rest):
    wg_refs = rest[:-3]
    x_out, h_out, route_out = rest[-3:]
    vg = _mm(z_ref[...], wg_refs)
    val = vg[:, 0:D_MODEL]
    gate = vg[:, D_MODEL:2 * D_MODEL]
    x = x_ref[...] + gt_ref[...] * (val * jax.nn.sigmoid(gate))
    x_out[...] = x
    h = _modulate(x, g_ref[...], sh_ref[...], sc_ref[...])
    h_out[...] = h
    logits = _split_dot(h, wrh_ref[...], wrl_ref[...]) + br_ref[...]
    lane = lax.broadcasted_iota(jnp.int32, logits.shape, 1)
    v1 = jnp.max(logits, axis=-1, keepdims=True)
    i1 = jnp.min(jnp.where(logits == v1, lane, LANES), axis=-1, keepdims=True)
    others = jnp.where(lane == i1, NEG_BIG, logits)
    v2 = jnp.max(others, axis=-1, keepdims=True)
    i2 = jnp.min(jnp.where(others == v2, lane, LANES), axis=-1, keepdims=True)
    e = jnp.exp(v2 - v1)
    g1 = 1.0 / (1.0 + e)
    g2 = e / (1.0 + e)
    route = jnp.where(lane == i1, g1, 0.0) + jnp.where(lane == i2, g2, 0.0)
    flag = jnp.logical_or(lane == i1 + N_EXPERTS, lane == i2 + N_EXPERTS)
    route_out[...] = route + jnp.where(flag, 1.0, 0.0)


def _glu_router(z, wgs, x2d, gt, g, sh, sc, wr_hi, wr_lo, br, tm, tiles_per_seq):
    m = x2d.shape[0]
    row = lambda n: pl.BlockSpec((tm, n), lambda i: (i, 0))
    vs = lambda a: _vec_spec(a, tm, tiles_per_seq)
    return pl.pallas_call(
        _glu_router_kernel,
        grid=(m // tm,),
        in_specs=[row(D_MODEL), row(D_MODEL), vs(gt), _full_spec(g), vs(sh), vs(sc),
                  _full_spec(wr_hi), _full_spec(wr_lo), _full_spec(br)] + [_full_spec(w) for w in wgs],
        out_specs=[row(D_MODEL), row(D_MODEL), row(LANES)],
        out_shape=[
            jax.ShapeDtypeStruct((m, D_MODEL), F32),
            jax.ShapeDtypeStruct((m, D_MODEL), F32),
            jax.ShapeDtypeStruct((m, LANES), F32),
        ],
        compiler_params=_params("parallel"),
        name="glu_router",
    )(z, x2d, gt, g, sh, sc, wr_hi, wr_lo, br, *wgs)


def _moe_kernel(te_ref, nv_ref, xs_ref, w1_ref, w3_ref, w2_ref, o_ref, x_scr, acc_scr):
    i = pl.program_id(0)
    f = pl.program_id(1)
    nf = pl.num_programs(1)
    valid = i < nv_ref[0]

    @pl.when(jnp.logical_and(valid, f == 0))
    def _():
        x_scr[...] = xs_ref[...].astype(BF16)

    @pl.when(valid)
    def _():
        x = x_scr[...]
        a = _bdot(x, w1_ref[...].astype(BF16))
        b = _bdot(x, w3_ref[...].astype(BF16))
        u = (jax.nn.silu(a) * b).astype(BF16)
        contrib = _bdot(u, w2_ref[...].astype(BF16))

        @pl.when(f == 0)
        def _():
            acc_scr[...] = contrib

        @pl.when(f > 0)
        def _():
            acc_scr[...] = acc_scr[...] + contrib

        @pl.when(f == nf - 1)
        def _():
            o_ref[...] = acc_scr[...]

    @pl.when(jnp.logical_and(jnp.logical_not(valid), f == nf - 1))
    def _():
        o_ref[...] = jnp.zeros_like(o_ref)


def _moe_experts(xs, tile_expert, n_valid, w1, w3, w2, tm):
    rows = xs.shape[0]
    n_tiles = rows // tm
    tf = 1792
    nf = D_FF_EXPERT // tf

    def fsel(i, f, nv):
        return jnp.where(i < nv[0], f, nf - 1)

    grid_spec = pltpu.PrefetchScalarGridSpec(
        num_scalar_prefetch=2,
        grid=(n_tiles, nf),
        in_specs=[
            pl.BlockSpec((tm, D_MODEL), lambda i, f, te, nv: (i, 0)),
            pl.BlockSpec((None, D_MODEL, tf), lambda i, f, te, nv: (te[i], 0, fsel(i, f, nv))),
            pl.BlockSpec((None, D_MODEL, tf), lambda i, f, te, nv: (te[i], 0, fsel(i, f, nv))),
            pl.BlockSpec((None, tf, D_MODEL), lambda i, f, te, nv: (te[i], fsel(i, f, nv), 0)),
        ],
        out_specs=pl.BlockSpec((tm, D_MODEL), lambda i, f, te, nv: (i, 0)),
        scratch_shapes=[pltpu.VMEM((tm, D_MODEL), BF16), pltpu.VMEM((tm, D_MODEL), F32)],
    )
    return pl.pallas_call(
        _moe_kernel,
        grid_spec=grid_spec,
        out_shape=jax.ShapeDtypeStruct((rows, D_MODEL), F32),
        compiler_params=_params("parallel", "arbitrary"),
        name="moe_experts",
    )(tile_expert, n_valid, xs, w1, w3, w2)


ROW_DMA_UNROLL = 8


def _row_params(*sem):
    return pltpu.CompilerParams(dimension_semantics=sem, vmem_limit_bytes=VMEM_LIMIT_BYTES,
                                disable_bounds_checks=True)


def _dispatch_kernel(sa_ref, sb_ref, h_ref, xs_in_ref, xs_ref, sem, *, tm):
    del xs_in_ref
    base = pl.program_id(0) * tm

    def copies(t):
        tok = base + t
        src = h_ref.at[pl.ds(t, 1)]
        return (pltpu.make_async_copy(src, xs_ref.at[pl.ds(sa_ref[tok], 1)], sem.at[0]),
                pltpu.make_async_copy(src, xs_ref.at[pl.ds(sb_ref[tok], 1)], sem.at[1]))

    def issue(t, carry):
        for cp in copies(t):
            cp.start()
        return carry

    def drain(t, carry):
        for cp in copies(t):
            cp.wait()
        return carry

    lax.fori_loop(0, tm, issue, 0, unroll=ROW_DMA_UNROLL)
    lax.fori_loop(0, tm, drain, 0, unroll=ROW_DMA_UNROLL)


def _moe_dispatch(h, slot_a, slot_b, n_rows, tm):
    m = h.shape[0]
    grid_spec = pltpu.PrefetchScalarGridSpec(
        num_scalar_prefetch=2,
        grid=(m // tm,),
        in_specs=[pl.BlockSpec((tm, D_MODEL), lambda i, sa, sb: (i, 0)), pl.BlockSpec(memory_space=pl.ANY)],
        out_specs=pl.BlockSpec(memory_space=pl.ANY),
        scratch_shapes=[pltpu.SemaphoreType.DMA((2,))],
    )
    return pl.pallas_call(
        functools.partial(_dispatch_kernel, tm=tm),
        grid_spec=grid_spec,
        out_shape=jax.ShapeDtypeStruct((n_rows, D_MODEL), F32),
        input_output_aliases={3: 0},
        compiler_params=_row_params("arbitrary"),
        name="moe_dispatch",
    )(slot_a, slot_b, h, jnp.zeros((n_rows, D_MODEL), F32))


def _combine_kernel(sa_ref, sb_ref, x_ref, gt_ref, ga_ref, gb_ref, ys_ref, o_ref, buf_a, buf_b, sem, *, tm):
    base = pl.program_id(0) * tm

    def copies(t):
        tok = base + t
        return (pltpu.make_async_copy(ys_ref.at[pl.ds(sa_ref[tok], 1)], buf_a.at[pl.ds(t, 1)], sem.at[0]),
                pltpu.make_async_copy(ys_ref.at[pl.ds(sb_ref[tok], 1)], buf_b.at[pl.ds(t, 1)], sem.at[1]))

    def issue(t, carry):
        for cp in copies(t):
            cp.start()
        return carry

    def drain(t, carry):
        for cp in copies(t):
            cp.wait()
        return carry

    lax.fori_loop(0, tm, issue, 0, unroll=ROW_DMA_UNROLL)
    lax.fori_loop(0, tm, drain, 0, unroll=ROW_DMA_UNROLL)
    o_ref[...] = x_ref[...] + gt_ref[...] * (ga_ref[...] * buf_a[...] + gb_ref[...] * buf_b[...])


def _moe_combine(x2d, gt, gate_a, gate_b, ys, slot_a, slot_b, tm, tiles_per_seq):
    m = x2d.shape[0]
    row = pl.BlockSpec((tm, D_MODEL), lambda i, sa, sb: (i, 0))
    col = pl.BlockSpec((tm, 1), lambda i, sa, sb: (i, 0))
    grid_spec = pltpu.PrefetchScalarGridSpec(
        num_scalar_prefetch=2,
        grid=(m // tm,),
        in_specs=[row, _vec_spec(gt, tm, tiles_per_seq), col, col, pl.BlockSpec(memory_space=pl.ANY)],
        out_specs=row,
        scratch_shapes=[pltpu.VMEM((tm, D_MODEL), F32), pltpu.VMEM((tm, D_MODEL), F32),
                        pltpu.SemaphoreType.DMA((2,))],
    )
    return pl.pallas_call(
        functools.partial(_combine_kernel, tm=tm),
        grid_spec=grid_spec,
        out_shape=jax.ShapeDtypeStruct((m, D_MODEL), F32),
        compiler_params=_row_params("arbitrary"),
        name="moe_combine",
    )(slot_a, slot_b, x2d, gt, gate_a, gate_b, ys)


def _moe(h, route, x2d, gt, w1, w3, w2, tm_rows, tm, tiles_per_seq):
    m = h.shape[0]
    comb = route[:, 0:N_EXPERTS]
    sel = route[:, N_EXPERTS:2 * N_EXPERTS] > 0.5
    seli = sel.astype(jnp.int32)
    counts = jnp.sum(seli, axis=0)
    padded = ((counts + tm_rows - 1) // tm_rows) * tm_rows
    ends = jnp.cumsum(padded)
    offsets = ends - padded
    rank = jnp.cumsum(seli, axis=0) - 1
    pos = offsets[None, :] + rank
    n_rows = ((2 * m + N_EXPERTS * (tm_rows - 1)) // tm_rows + 1) * tm_rows
    n_tiles = n_rows // tm_rows
    tile_start = jnp.arange(n_tiles, dtype=jnp.int32) * tm_rows
    n_valid = (ends[-1] // tm_rows).astype(jnp.int32)
    first_row = jnp.minimum(tile_start, ends[-1] - 1)
    tile_expert = jnp.sum((ends[None, :] <= first_row[:, None]).astype(jnp.int32), axis=1)
    slot_a = jnp.min(jnp.where(sel, pos, n_rows), axis=1).astype(jnp.int32)
    slot_b = jnp.max(jnp.where(sel, pos, -1), axis=1).astype(jnp.int32)
    expert_id = jnp.arange(N_EXPERTS, dtype=jnp.int32)[None, :]
    e_lo = jnp.min(jnp.where(sel, expert_id, N_EXPERTS), axis=1, keepdims=True)
    e_hi = jnp.max(jnp.where(sel, expert_id, -1), axis=1, keepdims=True)
    gate_a = jnp.sum(jnp.where(expert_id == e_lo, comb, 0.0), axis=1, keepdims=True)
    gate_b = jnp.sum(jnp.where(expert_id == e_hi, comb, 0.0), axis=1, keepdims=True)
    tm_dispatch = min(1024, m)
    xs = _moe_dispatch(h, slot_a, slot_b, n_rows, tm_dispatch)
    ys = _moe_experts(xs, tile_expert, n_valid.reshape(1), w1, w3, w2, tm_rows)
    return _moe_combine(x2d, gt, gate_a, gate_b, ys, slot_a, slot_b, tm, tiles_per_seq)


def _retention_constants(chunk):
    log_g = jnp.log1p(-jnp.exp2(-5.0 - jnp.arange(H_RET, dtype=F32)))
    idx = jnp.arange(chunk, dtype=F32)
    rel = idx[:, None] - idx[None, :]
    dm = jnp.where(rel >= 0, jnp.exp(log_g[:, None, None] * jnp.maximum(rel, 0.0)), 0.0)
    q_decay = jnp.exp(log_g[:, None] * (idx[None, :] + 1.0))
    k_decay = jnp.exp(log_g[:, None] * (chunk - 1.0 - idx[None, :]))
    chunk_decay = jnp.exp(log_g * chunk)
    qd = jnp.broadcast_to(q_decay[:, :, None], (H_RET, chunk, DK_RET))
    kd = jnp.broadcast_to(k_decay[:, :, None], (H_RET, chunk, DK_RET))
    cd = jnp.broadcast_to(chunk_decay[:, None, None], (H_RET, 1, DV_RET))
    return dm, qd, kd, cd


def _rotary_tables(pos):
    half = DK_RET // 2
    inv = np.float64(ROPE_BASE) ** (-np.arange(half, dtype=np.float64) / half)
    inv_hi = inv.astype(np.float32)
    inv_lo = (inv - inv_hi.astype(np.float64)).astype(np.float32)
    p = jnp.asarray(pos).astype(F32)[:, None]
    ang = p * jnp.asarray(inv_hi)[None, :] + p * jnp.asarray(inv_lo)[None, :]
    cos = jnp.cos(ang)
    sin = jnp.sin(ang)
    return jnp.concatenate([cos, cos], axis=-1), jnp.concatenate([-sin, sin], axis=-1)


def _s5_matrices(lam_re, lam_im, log_step, b_re, b_im, c_re, c_im, d_skip):
    lr = lam_re.astype(F32)
    li = lam_im.astype(F32)
    step = jnp.exp(log_step.astype(F32))[:, None]
    mag = jnp.exp(lr * step)
    ab_re = mag * jnp.cos(li * step)
    ab_im = mag * jnp.sin(li * step)
    nr = ab_re - 1.0
    ni = ab_im
    den = lr * lr + li * li
    kr = (nr * lr + ni * li) / den
    ki = (ni * lr - nr * li) / den
    br = b_re.astype(F32)
    bi = b_im.astype(F32)
    bb_re = kr[..., None] * br - ki[..., None] * bi
    bb_im = kr[..., None] * bi + ki[..., None] * br
    eye8 = jnp.eye(8, dtype=F32)

    def drive(bb):
        t = bb.reshape(8, 8, SSM_STATE, SSM_GROUP).transpose(0, 1, 3, 2)
        t = t[:, :, :, None, :] * eye8[None, :, None, :, None]
        return t.reshape(8, LANES, 8 * SSM_STATE)

    bmat = jnp.concatenate([drive(bb_re), drive(bb_im)], axis=-1)
    r_idx = jnp.arange(N_PAIRS)[:, None, None]
    gl_idx = jnp.arange(2)[None, :, None]
    gb_idx = jnp.arange(8)[None, None, :]
    onehot = (gb_idx == 2 * (r_idx % 4) + gl_idx).astype(F32)

    def readout(c):
        t = c.astype(F32).reshape(N_PAIRS, 2, SSM_GROUP, SSM_STATE).transpose(0, 1, 3, 2)
        t = t[:, :, :, None, :] * onehot[:, :, None, :, None]
        return t.reshape(N_PAIRS, LANES, LANES)

    cmat = jnp.concatenate([readout(c_re), -readout(c_im)], axis=1)
    return (ab_re.reshape(N_PAIRS, LANES), ab_im.reshape(N_PAIRS, LANES), bmat, cmat,
            d_skip.astype(F32).reshape(1, D_MODEL))


def _trunk(x, mods, pos, ret0, ssm0, paged, p):
    bsz, seq_len, _ = x.shape
    m = bsz * seq_len
    prompt = seq_len > 1
    depth = p["w_ada"].shape[0]
    x2d = x.reshape(m, D_MODEL)
    if prompt:
        tm = min(512, seq_len)
        tps = seq_len // tm
        tm_ffn = min(256, seq_len)
        tps_ffn = seq_len // tm_ffn
        vec = lambda v: v.reshape(bsz, 1, D_MODEL)
    else:
        tm = tm_ffn = m
        tps = tps_ffn = 1
        vec = lambda v: v.reshape(1, bsz, D_MODEL)
    row = lambda v: v.reshape(1, -1).astype(F32)
    wsp = lambda w: _wsplit(w, not prompt)
    rets, ks_new, vs_new, srs, sis = [], [], [], [], []
    for layer in range(depth):
        sh1, sc1, gt1, sh2, sc2, gt2 = [vec(v) for v in jnp.split(mods[layer], N_ADA, axis=-1)]
        i = layer // 2
        if layer % 2 == 0:
            lam_init = 0.8 - 0.6 * math.exp(-0.3 * layer)
            cos_t, sin_t = _rotary_tables(pos)
            seg = (jnp.arange(W_DIFF)[:, None] // DH_DIFF == jnp.arange(W_DIFF)[None, :] // DH_DIFF).astype(BF16)
            gq = jnp.tile(row(p["g_qnorm"][i]), (1, W_DIFF // DH_DIFF))
            gk = jnp.tile(row(p["g_knorm"][i]), (1, W_DIFF // DH_DIFF))
            ret_in, qb, kb, vb, *vt = _in_even(
                x2d, row(p["g_norm1"][layer]), sh1, sc1, cos_t, sin_t, wsp(p["w_in_mix"][i]),
                gq, gk, seg, tm, tps, BF16 if prompt else F32, prompt)
            lam_vec = jnp.stack([p["lam_q1"][i], p["lam_k1"][i], p["lam_q2"][i], p["lam_k2"][i]]).astype(F32)
            g_ret = row(p["g_ret"][i])
            g_d = row(p["g_dnorm"][i])
            if prompt:
                dm, qd, kd, cd = _retention_constants(math.gcd(seq_len, RET_CHUNK))
                o_ret, s_new = _retention_prompt(ret_in, bsz, seq_len, dm, qd, kd, cd, g_ret)
                o_diff = _diff_prompt(qb, kb, vt[0], lam_vec, g_d, bsz, seq_len, lam_init)
            else:
                _, qd, _, cd = _retention_constants(1)
                o_ret, s_new = _retention_step(ret_in, ret0[i].astype(F32), qd, cd, g_ret)
                o_diff = _diff_step(qb, kb, vb, paged["cache_k"], paged["cache_v"], i,
                                    paged["page_table"], lam_vec, g_d, lam_init)
            rets.append(s_new)
            if prompt:
                k_out = jnp.transpose(vt[1].reshape(bsz, H_DIFF, 2, DH_DIFF, seq_len), (0, 4, 1, 2, 3))
            else:
                k_out = kb.reshape(bsz, seq_len, H_DIFF, 2, DH_DIFF)
            ks_new.append(k_out)
            vs_new.append(vb.reshape(bsz, seq_len, H_DIFF, DV_DIFF))
            x2d = _out_even(o_ret, o_diff, wsp(p["w_out_mix"][i]), x2d, gt1, tm, tps)
            x2d = _ffn(x2d, row(p["g_norm2"][layer]), sh2, sc2, gt2, wsp(p["w_ffn1"][i]),
                       wsp(p["w_ffn3"][i]), wsp(p["w_ffn2"][i]), tm_ffn, tps_ffn)
        else:
            a_re, a_im, bmat, cmat, d_row = _s5_matrices(
                p["lam_re"][i], p["lam_im"][i], p["log_step"][i], p["b_re"][i], p["b_im"][i],
                p["c_re"][i], p["c_im"][i], p["d_skip"][i])
            u2d = _modmm(x2d, row(p["g_norm1"][layer]), sh1, sc1, wsp(p["w_ssm_in"][i]), tm, tps)
            n_state = N_GROUPS * SSM_STATE
            if prompt:
                z, h_re, h_im = _s5_prompt(u2d, bsz, seq_len, a_re, a_im, bmat.astype(BF16), cmat.astype(BF16),
                                           d_row)
            else:
                z, h_re, h_im = _s5_step(
                    u2d, ssm0[0][i].astype(F32).reshape(bsz, n_state), ssm0[1][i].astype(F32).reshape(bsz, n_state),
                    a_re.reshape(1, n_state), a_im.reshape(1, n_state), wsp(bmat), wsp(cmat), d_row)
            srs.append(h_re.reshape(bsz, N_GROUPS, SSM_STATE))
            sis.append(h_im.reshape(bsz, N_GROUPS, SSM_STATE))
            wr = jnp.zeros((D_MODEL, LANES), F32).at[:, 0:N_EXPERTS].set(p["w_router"][i].astype(F32))
            wr_hi, wr_lo = _wsplit(wr, True)
            br = jnp.full((1, LANES), NEG_BIG, F32).at[0, 0:N_EXPERTS].set(p["b_router"][i].astype(F32))
            x2d, h_mod, route = _glu_router(
                z, wsp(p["w_glu"][i]), x2d, gt1, row(p["g_norm2"][layer]), sh2, sc2,
                wr_hi, wr_lo, br, tm_ffn, tps_ffn)
            x2d = _moe(h_mod, route, x2d, gt2, p["w_moe1"][i].astype(BF16), p["w_moe3"][i].astype(BF16),
                       p["w_moe2"][i].astype(BF16),
                       512 if prompt else 32, tm, tps)
    return (x2d.reshape(bsz, seq_len, D_MODEL), jnp.stack(rets), jnp.stack(ks_new), jnp.stack(vs_new),
            jnp.stack(srs), jnp.stack(sis))


def kernel(x_prompt, x_sample, state_ret, cache_k, cache_v, state_ssm_re, state_ssm_im, page_table, c_prompt, c_sample, w_ada, b_ada, g_norm1, g_norm2, w_in_mix, w_out_mix, g_ret, g_qnorm, g_knorm, g_dnorm, lam_q1, lam_k1, lam_q2, lam_k2, w_ffn1, w_ffn3, w_ffn2, w_ssm_in, lam_re, lam_im, log_step, b_re, b_im, c_re, c_im, d_skip, w_glu, w_router, b_router, w_moe1, w_moe3, w_moe2):
    p = dict(w_ada=w_ada, g_norm1=g_norm1, g_norm2=g_norm2, w_in_mix=w_in_mix, w_out_mix=w_out_mix,
             g_ret=g_ret, g_qnorm=g_qnorm, g_knorm=g_knorm, g_dnorm=g_dnorm, lam_q1=lam_q1, lam_k1=lam_k1,
             lam_q2=lam_q2, lam_k2=lam_k2, w_ffn1=w_ffn1, w_ffn3=w_ffn3, w_ffn2=w_ffn2, w_ssm_in=w_ssm_in,
             lam_re=lam_re, lam_im=lam_im, log_step=log_step, b_re=b_re, b_im=b_im, c_re=c_re, c_im=c_im,
             d_skip=d_skip, w_glu=w_glu, w_router=w_router, b_router=b_router,
             w_moe1=w_moe1, w_moe3=w_moe3, w_moe2=w_moe2)
    n_prompt = c_prompt.shape[0]
    n_sample = c_sample.shape[0]
    n_all = n_prompt + n_sample
    rows = ((n_all + 7) // 8) * 8
    c_all = jnp.zeros((rows, D_MODEL), F32).at[0:n_prompt].set(c_prompt).at[n_prompt:n_all].set(c_sample)
    mods = _ada(c_all, w_ada, b_ada)
    seq_len = x_prompt.shape[1]
    past_len = page_table.shape[1] * PAGE_SIZE
    y_p, ret_p, k_p, v_p, sr_p, si_p = _trunk(
        x_prompt, mods[:, 0:n_prompt], np.arange(seq_len), None, None, None, p)
    paged = dict(cache_k=cache_k, cache_v=cache_v, page_table=page_table)
    y_s, ret_s, k_s, v_s, sr_s, si_s = _trunk(
        x_sample, mods[:, n_prompt:n_all], past_len + np.arange(x_sample.shape[1]),
        state_ret, (state_ssm_re, state_ssm_im), paged, p)
    return (y_p, y_s, ret_p, ret_s, k_p, v_p, k_s, v_s, sr_p, si_p, sr_s, si_s)
```
